```python
import math
import jax, jax.numpy as jnp
from jax import lax
import numpy as np

D_MODEL = 4096
BATCH = 1
SEQ = 8192
DEPTH = 2
DEC_BATCH = 1
DEC_SEQ = 16384
PAST_LEN = 128

BRANCH_W = D_MODEL // 4
N_BRANCH = 3
HY_W = BRANCH_W
HY_ORDER = 2
HY_EMB = 33
HY_BANDS = (HY_EMB - 1) // 2
HY_FILT_W = 64
HY_TARGET = 1e-2
HY_FAST_PCT = 0.3
HY_SLOW_PCT = 1.5
HY_MIN_DECAY = math.log(HY_TARGET) / HY_SLOW_PCT
HY_MAX_DECAY = math.log(HY_TARGET) / HY_FAST_PCT
ATT_HEAD_DIM = 128
ATT_Q_HEADS = BRANCH_W // ATT_HEAD_DIM
ATT_KV_HEADS = 2
ATT_GROUP = ATT_Q_HEADS // ATT_KV_HEADS
WINDOW = 128
BLOCK = 128
RET_HEADS = 8
RET_HEAD_DIM = BRANCH_W // RET_HEADS
RET_CHUNK = 128
D_FF = 11008
EPS = 1e-6
NEG = -1e30

IN_SIZES = [HY_ORDER + 1 and (HY_ORDER + 1) * HY_W,
            ATT_Q_HEADS * ATT_HEAD_DIM, ATT_KV_HEADS * ATT_HEAD_DIM, ATT_KV_HEADS * ATT_HEAD_DIM,
            BRANCH_W, BRANCH_W, BRANCH_W, BRANCH_W,
            N_BRANCH * D_MODEL]
IN_COLS = int(sum(IN_SIZES))
IN_OFFSETS = [int(o) for o in np.cumsum(IN_SIZES)[:-1]]

kernel_name = "hybrid_bidir_hyena_swa_retention_encoder"


def rms_norm(x, g):
    xf = x.astype(jnp.float32)
    y = xf * lax.rsqrt(jnp.mean(xf * xf, axis=-1, keepdims=True) + EPS) * g.astype(jnp.float32)
    return y.astype(x.dtype)


def dwconv3(x, w, b):
    xp = jnp.pad(x, ((0, 0), (1, 1), (0, 0)))
    return xp[:, :-2] * w[0] + xp[:, 1:-1] * w[1] + xp[:, 2:] * w[2] + b


def alibi_slopes(n):
    return 2.0 ** (-8.0 * (jnp.arange(n, dtype=jnp.float32) + 1.0) / n)


def hyena_filters(L, w1, b1, w2, b2, w3, freq):
    f32 = jnp.float32
    w1, b1, w2, b2, w3, freq = (a.astype(f32) for a in (w1, b1, w2, b2, w3, freq))
    t = jnp.linspace(0.0, 1.0, L, dtype=f32)[:, None]
    bands = jnp.linspace(1e-4, HY_BANDS - 1, HY_BANDS, dtype=f32)
    ang = (2.0 * math.pi / L) * jnp.arange(L, dtype=f32)[:, None] * bands[None, :]
    feats = jnp.concatenate([t, jnp.cos(ang), -jnp.sin(ang)], axis=-1)
    h = jnp.sin(freq * (feats @ w1 + b1))
    h = jnp.sin(freq * (h @ w2 + b2))
    h = (h @ w3).reshape(L, HY_ORDER, 2, HY_W)
    deltas = jnp.abs(jnp.linspace(HY_MIN_DECAY, HY_MAX_DECAY, HY_W, dtype=f32))
    h = h * jnp.exp(-t[:, :, None, None] * deltas)
    fwd, bwd = h[:, :, 0], h[:, :, 1]
    k = jnp.concatenate([fwd, jnp.zeros_like(fwd[:1]), bwd[1:][::-1]], axis=0)
    return k / jnp.sum(jnp.abs(k), axis=0, keepdims=True)


def fft_conv(u, k, d):
    L = u.shape[1]
    uf = u.astype(jnp.float32)
    U = jnp.fft.rfft(uf, n=2 * L, axis=1)
    K = jnp.fft.rfft(k, n=2 * L, axis=0)
    y = jnp.fft.irfft(U * K[None], n=2 * L, axis=1)[:, :L]
    return (y + uf * d.astype(jnp.float32)).astype(u.dtype)


def hyena_mixer(z, conv_w, conv_b, w1, b1, w2, b2, w3, freq, skip):
    L = z.shape[1]
    z = dwconv3(z, conv_w, conv_b)
    v, x1, x2 = jnp.split(z, 3, axis=-1)
    k = hyena_filters(L, w1, b1, w2, b2, w3, freq)
    y = x1 * fft_conv(v, k[:, 0], skip[0])
    y = x2 * fft_conv(y, k[:, 1], skip[1])
    return y


def window_attention(q, k, v, sink):
    B, L, _ = q.shape
    nb = L // BLOCK
    q = q.reshape(B, nb, BLOCK, ATT_KV_HEADS, ATT_GROUP, ATT_HEAD_DIM)

    def band(x):
        xp = jnp.pad(x, ((0, 0), (BLOCK, BLOCK), (0, 0))).reshape(B, nb + 2, BLOCK, ATT_KV_HEADS, ATT_HEAD_DIM)
        return jnp.concatenate([xp[:, :-2], xp[:, 1:-1], xp[:, 2:]], axis=2)

    kb, vb = band(k), band(v)
    s = jnp.einsum('bnqhgd,bnshd->bnhgqs', q, kb).astype(jnp.float32) * (ATT_HEAD_DIM ** -0.5)
    q_pos = jnp.arange(nb)[:, None] * BLOCK + jnp.arange(BLOCK)[None, :]
    k_pos = jnp.arange(nb)[:, None] * BLOCK - BLOCK + jnp.arange(3 * BLOCK)[None, :]
    dist = jnp.abs(q_pos[:, :, None] - k_pos[:, None, :])
    valid = (dist <= WINDOW) & (k_pos[:, None, :] >= 0) & (k_pos[:, None, :] < L)
    slopes = alibi_slopes(ATT_Q_HEADS).reshape(ATT_KV_HEADS, ATT_GROUP)
    s = s - slopes[None, None, :, :, None, None] * dist.astype(jnp.float32)[None, :, None, None]
    s = jnp.where(valid[None, :, None, None], s, NEG)
    sink_l = jnp.broadcast_to(sink.astype(jnp.float32).reshape(1, 1, ATT_KV_HEADS, ATT_GROUP, 1, 1),
                              s.shape[:-1] + (1,))
    p = jax.nn.softmax(jnp.concatenate([s, sink_l], axis=-1), axis=-1)[..., :-1]
    o = jnp.einsum('bnhgqs,bnshd->bnqhgd', p.astype(vb.dtype), vb)
    return o.reshape(B, L, ATT_Q_HEADS * ATT_HEAD_DIM)


def retention_dir(q, k, v, gammas):
    B, L, H, dh = q.shape
    C = RET_CHUNK
    nc = L // C
    dt = q.dtype
    q = q.reshape(B, nc, C, H, dh)
    k = k.reshape(B, nc, C, H, dh)
    v = v.reshape(B, nc, C, H, dh)
    log_g = jnp.log(gammas)
    idx = jnp.arange(C, dtype=jnp.float32)
    rel = idx[:, None] - idx[None, :]
    decay_in = jnp.where(rel[None] >= 0, jnp.exp(jnp.maximum(rel, 0.0)[None] * log_g[:, None, None]), 0.0)
    inner = jnp.einsum('bnihd,bnjhd->bnhij', q, k) * decay_in.astype(dt)
    inner = jnp.einsum('bnhij,bnjhe->bnihe', inner, v)
    k_w = jnp.exp((C - 1.0 - idx)[:, None] * log_g[None, :])
    kv = jnp.einsum('bnjhd,bnjhe->bnhde', k * k_w[None, None, :, :, None].astype(dt), v).astype(jnp.float32)
    chunk_decay = jnp.exp(C * log_g)[None, :, None, None]

    def step(S, kv_n):
        return S * chunk_decay + kv_n, S

    _, S_prev = lax.scan(step, jnp.zeros((B, H, dh, dh), jnp.float32), jnp.moveaxis(kv, 1, 0))
    S_prev = jnp.moveaxis(S_prev, 0, 1)
    q_w = jnp.exp((idx + 1.0)[:, None] * log_g[None, :])
    cross = jnp.einsum('bnihd,bnhde->bnihe', q.astype(jnp.float32) * q_w[None, None, :, :, None], S_prev)
    return (inner.astype(jnp.float32) + cross).reshape(B, L, H, dh)


def retention_mixer(q, k, v, g):
    B, L, _ = q.shape
    shp = (B, L, RET_HEADS, RET_HEAD_DIM)
    q, k, v = q.reshape(shp), k.reshape(shp) * (RET_HEAD_DIM ** -0.5), v.reshape(shp)
    hidx = jnp.arange(RET_HEADS, dtype=jnp.float32)
    g_fwd = 1.0 - 2.0 ** (-5.0 - hidx)
    g_bwd = 1.0 - 2.0 ** (-5.5 - hidx)
    o = retention_dir(q, k, v, g_fwd) + jnp.flip(
        retention_dir(jnp.flip(q, 1), jnp.flip(k, 1), jnp.flip(v, 1), g_bwd), 1)
    o = o * lax.rsqrt(jnp.mean(o * o, axis=-1, keepdims=True) + EPS)
    return (jax.nn.silu(g.astype(jnp.float32)) * o.reshape(B, L, BRANCH_W)).astype(g.dtype)


def layer(x, p):
    B, L, D = x.shape
    h = rms_norm(x, p['norm_mix'])
    z = h @ p['w_in']
    hz, aq, ak, av, rq, rk, rv, rg, gz = jnp.split(z, IN_OFFSETS, axis=-1)
    y_a = hyena_mixer(hz, p['hy_conv_w'], p['hy_conv_b'], p['hy_filt_w1'], p['hy_filt_b1'],
                      p['hy_filt_w2'], p['hy_filt_b2'], p['hy_filt_w3'], p['hy_filt_freq'], p['hy_skip'])
    y_b = window_attention(aq, ak, av, p['attn_sink'])
    y_c = retention_mixer(rq, rk, rv, rg)
    gates = jax.nn.sigmoid(gz).reshape(B, L, N_BRANCH, D)
    wb = p['w_branch']
    merged = (gates[:, :, 0] * (y_a @ wb[0]) + gates[:, :, 1] * (y_b @ wb[1])
              + gates[:, :, 2] * (y_c @ wb[2]))
    x = x + merged @ p['w_out']
    h = rms_norm(x, p['norm_ffn'])
    hg, hu = jnp.split(h @ p['w_ffn_in'], 2, axis=-1)
    hg = dwconv3(hg, p['ffn_conv_w'], p['ffn_conv_b'])
    return x + (jax.nn.gelu(hg, approximate=False) * hu) @ p['w_ffn_out']


def trunk(x, params, norm_final):
    for l in range(DEPTH):
        x = layer(x, {name: arr[l] for name, arr in params.items()})
    return rms_norm(x, norm_final)


def setup_inputs(seed: int = 0) -> dict:
    key = jax.random.key(seed)
    ks = jax.random.split(key, 24)
    f32 = jnp.float32
    nrm = lambda k, shp, s: jax.random.normal(k, shp, f32) * s
    return {
        'x_prompt': nrm(ks[0], (BATCH, SEQ, D_MODEL), 1.0),
        'x_sample': nrm(ks[1], (DEC_BATCH, DEC_SEQ, D_MODEL), 1.0),
        'norm_mix': 1.0 + nrm(ks[2], (DEPTH, D_MODEL), 0.01),
        'w_in': nrm(ks[3], (DEPTH, D_MODEL, IN_COLS), D_MODEL ** -0.5),
        'hy_conv_w': nrm(ks[4], (DEPTH, 3, 3 * HY_W), 3 ** -0.5),
        'hy_conv_b': nrm(ks[5], (DEPTH, 3 * HY_W), 0.01),
        'hy_filt_w1': nrm(ks[6], (DEPTH, HY_EMB, HY_FILT_W), HY_EMB ** -0.5),
        'hy_filt_b1': nrm(ks[7], (DEPTH, HY_FILT_W), 0.1),
        'hy_filt_w2': nrm(ks[8], (DEPTH, HY_FILT_W, HY_FILT_W), HY_FILT_W ** -0.5),
        'hy_filt_b2': nrm(ks[9], (DEPTH, HY_FILT_W), 0.1),
        'hy_filt_w3': nrm(ks[10], (DEPTH, HY_FILT_W, HY_ORDER * 2 * HY_W), HY_FILT_W ** -0.5),
        'hy_filt_freq': 1.0 + nrm(ks[11], (DEPTH, HY_FILT_W), 0.01),
        'hy_skip': nrm(ks[12], (DEPTH, HY_ORDER, HY_W), 1.0),
        'attn_sink': nrm(ks[13], (DEPTH, ATT_Q_HEADS), 0.1),
        'w_branch': nrm(ks[14], (DEPTH, N_BRANCH, BRANCH_W, D_MODEL), BRANCH_W ** -0.5),
        'w_out': nrm(ks[15], (DEPTH, D_MODEL, D_MODEL), D_MODEL ** -0.5),
        'norm_ffn': 1.0 + nrm(ks[16], (DEPTH, D_MODEL), 0.01),
        'w_ffn_in': nrm(ks[17], (DEPTH, D_MODEL, 2 * D_FF), D_MODEL ** -0.5),
        'ffn_conv_w': nrm(ks[18], (DEPTH, 3, D_FF), 3 ** -0.5),
        'ffn_conv_b': nrm(ks[19], (DEPTH, D_FF), 0.01),
        'w_ffn_out': nrm(ks[20], (DEPTH, D_FF, D_MODEL), D_FF ** -0.5),
        'norm_final': 1.0 + nrm(ks[21], (D_MODEL,), 0.01),
    }


def reference(x_prompt, x_sample, norm_mix, w_in, hy_conv_w, hy_conv_b, hy_filt_w1, hy_filt_b1,
              hy_filt_w2, hy_filt_b2, hy_filt_w3, hy_filt_freq, hy_skip, attn_sink, w_branch, w_out,
              norm_ffn, w_ffn_in, ffn_conv_w, ffn_conv_b, w_ffn_out, norm_final):
    params = {
        'norm_mix': norm_mix, 'w_in': w_in, 'hy_conv_w': hy_conv_w, 'hy_conv_b': hy_conv_b,
        'hy_filt_w1': hy_filt_w1, 'hy_filt_b1': hy_filt_b1, 'hy_filt_w2': hy_filt_w2,
        'hy_filt_b2': hy_filt_b2, 'hy_filt_w3': hy_filt_w3, 'hy_filt_freq': hy_filt_freq,
        'hy_skip': hy_skip, 'attn_sink': attn_sink, 'w_branch': w_branch, 'w_out': w_out,
        'norm_ffn': norm_ffn, 'w_ffn_in': w_ffn_in, 'ffn_conv_w': ffn_conv_w,
        'ffn_conv_b': ffn_conv_b, 'w_ffn_out': w_ffn_out,
    }
    y_prompt = trunk(x_prompt, params, norm_final)
    y_sample = trunk(x_sample, params, norm_final)
    return (y_prompt, y_sample)
```

```python
import functools
import math

import jax
import jax.numpy as jnp
from jax import lax
from jax.experimental import pallas as pl
from jax.experimental.pallas import tpu as pltpu

F32 = jnp.float32
BF16 = jnp.bfloat16

EPS = 1e-6
NEG = -1e30
HEAD_DIM = 128
KV_HEADS = 2
WINDOW = 128
RET_CHUNK = 128
HY_ORDER = 2
HY_EMB = 33
HY_BANDS = (HY_EMB - 1) // 2
HY_MIN_DECAY = math.log(1e-2) / 1.5
HY_MAX_DECAY = math.log(1e-2) / 0.3
DFT_INNER = 128
FEAT_LANES = 128
HALO = 16
FF_PAD = 1024
VMEM_LIMIT = 56 * 1024 * 1024


def _cparams(*sem):
    return pltpu.CompilerParams(dimension_semantics=sem, vmem_limit_bytes=VMEM_LIMIT)


def _tile(n, pref):
    if n <= pref:
        return n
    t = (pref // 128) * 128
    while t >= 128:
        if n % t == 0:
            return t
        t -= 128
    return n


def _dot(a, b):
    return jnp.dot(a, b, preferred_element_type=F32)


def _split(x):
    hi = x.astype(BF16)
    lo = (x - hi.astype(F32)).astype(BF16)
    return hi, lo


def _dot_split(m_hi, m_lo, x):
    x_hi, x_lo = _split(x)
    return _dot(m_hi, x_hi) + (_dot(m_lo, x_hi) + _dot(m_hi, x_lo))


def _dot3(a, b):
    a_hi, a_lo = _split(a)
    b_hi, b_lo = _split(b)
    return _dot(a_hi, b_hi) + (_dot(a_lo, b_hi) + _dot(a_hi, b_lo))


def _rmsnorm_kernel(x_ref, g_ref, o_ref):
    x = x_ref[...]
    ms = jnp.mean(x * x, axis=-1, keepdims=True)
    o_ref[...] = (x * lax.rsqrt(ms + EPS) * g_ref[...]).astype(o_ref.dtype)


def _rmsnorm(x, g, out_dtype):
    m, d = x.shape
    tm = _tile(m, 256)
    return pl.pallas_call(
        _rmsnorm_kernel,
        grid=(m // tm,),
        in_specs=[pl.BlockSpec((tm, d), lambda i: (i, 0)),
                  pl.BlockSpec((1, d), lambda i: (0, 0))],
        out_specs=pl.BlockSpec((tm, d), lambda i: (i, 0)),
        out_shape=jax.ShapeDtypeStruct((m, d), out_dtype),
        compiler_params=_cparams("parallel"),
        name="rmsnorm",
    )(x, g.reshape(1, d).astype(F32))


def _matmul_kernel(*refs, nk, has_res):
    if has_res:
        a_ref, b_ref, r_ref, o_ref = refs[:4]
    else:
        a_ref, b_ref, o_ref = refs[:3]
        r_ref = None
    p = _dot(a_ref[...], b_ref[...])

    def finish(acc):
        if has_res:
            acc = acc + r_ref[...]
        o_ref[...] = acc.astype(o_ref.dtype)

    if nk == 1:
        finish(p)
        return
    acc_ref = refs[-1]
    k = pl.program_id(2)

    @pl.when(k == 0)
    def _():
        acc_ref[...] = p

    @pl.when(k > 0)
    def _():
        acc_ref[...] += p

    @pl.when(k == nk - 1)
    def _():
        finish(acc_ref[...])


def _matmul(a, b, *, tm, tn, tk=None, out_dtype, residual=None, name):
    m, kdim = a.shape
    n = b.shape[1]
    tm, tn = _tile(m, tm), _tile(n, tn)
    tk = kdim if tk is None else _tile(kdim, tk)
    nk = kdim // tk
    in_specs = [pl.BlockSpec((tm, tk), lambda i, j, k: (i, k)),
                pl.BlockSpec((tk, tn), lambda i, j, k: (k, j))]
    args = [a, b]
    if residual is not None:
        in_specs.append(pl.BlockSpec((tm, tn), lambda i, j, k: (i, j)))
        args.append(residual)
    scratch = [pltpu.VMEM((tm, tn), F32)] if nk > 1 else []
    return pl.pallas_call(
        functools.partial(_matmul_kernel, nk=nk, has_res=residual is not None),
        grid=(m // tm, n // tn, nk),
        in_specs=in_specs,
        out_specs=pl.BlockSpec((tm, tn), lambda i, j, k: (i, j)),
        out_shape=jax.ShapeDtypeStruct((m, n), out_dtype),
        scratch_shapes=scratch,
        compiler_params=_cparams("parallel", "parallel", "arbitrary"),
        name=name,
    )(*args)


def _shift_rows(x, prev_row, next_row):
    tm = x.shape[0]
    row = lax.broadcasted_iota(jnp.int32, x.shape, 0)
    down = jnp.where(row == 0, prev_row, pltpu.roll(x, 1, axis=0))
    up = jnp.where(row == tm - 1, next_row, pltpu.roll(x, tm - 1, axis=0))
    return down, up


def _hyconv_kernel(x_ref, xp_ref, xn_ref, w_ref, b_ref, o_ref, *, n_tiles):
    i = pl.program_id(0)
    x = x_ref[...].astype(F32)
    prev_row = xp_ref[...].astype(F32)[HALO - 1:HALO] * (i > 0).astype(F32)
    next_row = xn_ref[...].astype(F32)[0:1] * (i < n_tiles - 1).astype(F32)
    down, up = _shift_rows(x, prev_row, next_row)
    w = w_ref[...]
    o_ref[...] = down * w[0:1] + x * w[1:2] + up * w[2:3] + b_ref[...]


def _hyena_conv(z, conv_w, conv_b, seq, width):
    tm = _tile(seq, 512)
    n_tiles = seq // tm
    n_halo = seq // HALO
    per = tm // HALO
    return pl.pallas_call(
        functools.partial(_hyconv_kernel, n_tiles=n_tiles),
        grid=(n_tiles, 3),
        in_specs=[
            pl.BlockSpec((tm, width), lambda i, p: (i, p)),
            pl.BlockSpec((HALO, width), lambda i, p: (jnp.maximum(i * per - 1, 0), p)),
            pl.BlockSpec((HALO, width), lambda i, p: (jnp.minimum((i + 1) * per, n_halo - 1), p)),
            pl.BlockSpec((3, width), lambda i, p: (0, p)),
            pl.BlockSpec((1, width), lambda i, p: (0, p)),
        ],
        out_specs=pl.BlockSpec((None, tm, width), lambda i, p: (p, i, 0)),
        out_shape=jax.ShapeDtypeStruct((3, seq, width), F32),
        compiler_params=_cparams("parallel", "parallel"),
        name="hyena_conv3",
    )(z, z, z, conv_w.astype(F32), conv_b.reshape(1, -1).astype(F32))


def _filter_kernel(bands_ref, w1_ref, b1_ref, w2_ref, b2_ref, freq_ref, w3_ref, delta_ref,
                   k_ref, sum_ref, *, seq, tr):
    i = pl.program_id(1)
    n = 2 * seq
    r = i * tr + lax.broadcasted_iota(jnp.int32, (tr, FEAT_LANES), 0)
    pos = jnp.where(r < seq, r, n - r).astype(F32)
    t = pos / float(seq - 1)
    lane = lax.broadcasted_iota(jnp.int32, (tr, FEAT_LANES), 1)
    ang = (2.0 * math.pi / seq) * pos * bands_ref[...]
    feats = jnp.where(lane < HY_BANDS, jnp.cos(ang),
                      jnp.where(lane < 2 * HY_BANDS, -jnp.sin(ang),
                                jnp.where(lane == 2 * HY_BANDS, t, 0.0)))
    freq = freq_ref[...]
    h = jnp.sin(freq * (_dot3(feats, w1_ref[...]) + b1_ref[...]))
    h = jnp.sin(freq * (_dot3(h, w2_ref[...]) + b2_ref[...]))
    h = _dot3(h, w3_ref[...])
    r1 = i * tr + lax.broadcasted_iota(jnp.int32, (tr, 1), 0)
    t1 = jnp.where(r1 < seq, r1, n - r1).astype(F32) / float(seq - 1)
    kf = h * jnp.exp(-t1 * delta_ref[...])
    kf = jnp.where(r1 == seq, 0.0, kf)
    k_ref[...] = kf
    part = jnp.sum(jnp.abs(kf), axis=0, keepdims=True)

    @pl.when(i == 0)
    def _():
        sum_ref[...] = part

    @pl.when(i > 0)
    def _():
        sum_ref[...] += part


def _hyena_filters(seq, width, w1, b1, w2, b2, w3, freq):
    n = 2 * seq
    tr = _tile(seq, 512)
    nb = n // tr
    fw = w1.shape[1]
    bands = jnp.linspace(1e-4, HY_BANDS - 1, HY_BANDS, dtype=F32)
    bands = jnp.concatenate([bands, bands, jnp.zeros((FEAT_LANES - 2 * HY_BANDS,), F32)]).reshape(1, -1)
    w1 = w1.astype(F32)
    w1p = jnp.concatenate([w1[1:], w1[:1], jnp.zeros((FEAT_LANES - HY_EMB, fw), F32)], axis=0)
    deltas = jnp.abs(jnp.linspace(HY_MIN_DECAY, HY_MAX_DECAY, width, dtype=F32)).reshape(1, -1)
    row = lambda a: a.reshape(1, -1).astype(F32)
    const = lambda shape: pl.BlockSpec(shape, lambda o, i: (0, 0))
    return pl.pallas_call(
        functools.partial(_filter_kernel, seq=seq, tr=tr),
        grid=(HY_ORDER, nb),
        in_specs=[
            const((1, FEAT_LANES)), const((FEAT_LANES, fw)), const((1, fw)), const((fw, fw)),
            const((1, fw)), const((1, fw)),
            pl.BlockSpec((fw, width), lambda o, i: (0, 2 * o + (i >= nb // 2).astype(jnp.int32))),
            const((1, width)),
        ],
        out_specs=[pl.BlockSpec((tr, width), lambda o, i: (i, o)),
                   pl.BlockSpec((1, width), lambda o, i: (0, o))],
        out_shape=[jax.ShapeDtypeStruct((n, HY_ORDER * width), F32),
                   jax.ShapeDtypeStruct((1, HY_ORDER * width), F32)],
        compiler_params=_cparams("parallel", "arbitrary"),
        name="hyena_filters",
    )(bands, w1p, row(b1), w2.astype(F32), row(b2), row(freq), w3.astype(F32), deltas)


def _dft_tables(seq):
    n = 2 * seq
    n2 = DFT_INNER
    n1 = n // n2
    k1 = jnp.arange(n1, dtype=jnp.int32)
    idx = (k1[:, None] * k1[None, :]) % n1
    th = (2.0 * math.pi / n1) * idx.astype(F32)
    c, s = jnp.cos(th), jnp.sin(th)
    outer = jnp.concatenate([c, -s], axis=0)
    inv_outer = jnp.concatenate([c, -s], axis=1) / n1
    j = jnp.arange(n2, dtype=jnp.int32)
    m = (j[None, None, :] * (n1 * j[None, :, None] + k1[:, None, None])) % n
    ph = (2.0 * math.pi / n) * m.astype(F32)
    gr, gi = jnp.cos(ph), -jnp.sin(ph)
    slab = jnp.concatenate([jnp.concatenate([gr, -gi], axis=2),
                            jnp.concatenate([gi, gr], axis=2)], axis=1)
    inv_slab = jnp.swapaxes(slab, 1, 2) / n2
    return dict(n1=n1, n2=n2, outer=_split(outer), outer_half=_split(outer[:, :n1 // 2]),
                inv_outer=_split(inv_outer[:n1 // 2]), slab=_split(slab), inv_slab=_split(inv_slab))


def _dft_outer_kernel(x_ref, fh_ref, fl_ref, ar_ref, ai_ref, *, n1):
    p = _dot_split(fh_ref[...], fl_ref[...], x_ref[...])
    ar_ref[...] = p[:n1]
    ai_ref[...] = p[n1:]


def _dft_outer(x2d, part, rows, mats, n1, cb):
    cols = x2d.shape[1]
    cb = _tile(cols, cb)
    fh, fl = mats
    out = jax.ShapeDtypeStruct((n1, cols), F32)
    return pl.pallas_call(
        functools.partial(_dft_outer_kernel, n1=n1),
        grid=(cols // cb,),
        in_specs=[pl.BlockSpec((rows, cb), lambda j: (part, j)),
                  pl.BlockSpec(fh.shape, lambda j: (0, 0)),
                  pl.BlockSpec(fl.shape, lambda j: (0, 0))],
        out_specs=[pl.BlockSpec((n1, cb), lambda j: (0, j))] * 2,
        out_shape=[out, out],
        compiler_params=_cparams("parallel"),
        name="dft_outer",
    )(x2d, fh, fl)


def _dft_slab_filter_kernel(ar_ref, ai_ref, mh_ref, ml_ref, s_ref, kr_ref, ki_ref, *, n2):
    a = jnp.concatenate([ar_ref[...], ai_ref[...]], axis=0)
    x = _dot_split(mh_ref[...], ml_ref[...], a) * (1.0 / s_ref[...])
    kr_ref[...] = x[:n2]
    ki_ref[...] = x[n2:]


def _dft_slab_filter(ar, ai, colsum, tables, width):
    n1, n2 = tables["n1"], tables["n2"]
    mh, ml = tables["slab"]
    blk = pl.BlockSpec((n2, width), lambda k, o: (k, o))
    mat = pl.BlockSpec((None, 2 * n2, 2 * n2), lambda k, o: (k, 0, 0))
    out = jax.ShapeDtypeStruct(ar.shape, F32)
    return pl.pallas_call(
        functools.partial(_dft_slab_filter_kernel, n2=n2),
        grid=(n1, HY_ORDER),
        in_specs=[blk, blk, mat, mat, pl.BlockSpec((1, width), lambda k, o: (0, o))],
        out_specs=[blk, blk],
        out_shape=[out, out],
        compiler_params=_cparams("parallel", "parallel"),
        name="dft_slab_filter",
    )(ar, ai, mh, ml, colsum)


def _dft_slab_kernel(ar_ref, ai_ref, kr_ref, ki_ref, mh_ref, ml_ref, nh_ref, nl_ref,
                     br_ref, bi_ref, *, n2):
    a = jnp.concatenate([ar_ref[...], ai_ref[...]], axis=0)
    x = _dot_split(mh_ref[...], ml_ref[...], a)
    xr, xi = x[:n2], x[n2:]
    kr, ki = kr_ref[...], ki_ref[...]
    y = jnp.concatenate([xr * kr - xi * ki, xr * ki + xi * kr], axis=0)
    b = _dot_split(nh_ref[...], nl_ref[...], y)
    br_ref[...] = b[:n2]
    bi_ref[...] = b[n2:]


def _dft_slab(ar, ai, kr, ki, order, tables, width):
    n1, n2 = tables["n1"], tables["n2"]
    mh, ml = tables["slab"]
    nh, nl = tables["inv_slab"]
    blk = pl.BlockSpec((n2, width), lambda k: (k, 0))
    kblk = pl.BlockSpec((n2, width), lambda k: (k, order))
    mat = pl.BlockSpec((None, 2 * n2, 2 * n2), lambda k: (k, 0, 0))
    out = jax.ShapeDtypeStruct(ar.shape, F32)
    return pl.pallas_call(
        functools.partial(_dft_slab_kernel, n2=n2),
        grid=(n1,),
        in_specs=[blk, blk, kblk, kblk, mat, mat, mat, mat],
        out_specs=[blk, blk],
        out_shape=[out, out],
        compiler_params=_cparams("parallel"),
        name="dft_slab",
    )(ar, ai, kr, ki, mh, ml, nh, nl)


def _dft_inv_outer_kernel(br_ref, bi_ref, gh_ref, gl_ref, u_ref, gate_ref, skip_ref, o_ref):
    b = jnp.concatenate([br_ref[...], bi_ref[...]], axis=0)
    conv = _dot_split(gh_ref[...], gl_ref[...], b)
    u = u_ref[...]
    o_ref[...] = (gate_ref[...] * (conv + u * skip_ref[...])).astype(o_ref.dtype)


def _dft_inv_outer(br, bi, tables, u2d, u_part, gate2d, gate_part, skip_row, out_dtype, cb):
    n1 = tables["n1"]
    rows = n1 // 2
    cols = br.shape[1]
    cb = _tile(cols, cb)
    gh, gl = tables["inv_outer"]
    big = pl.BlockSpec((n1, cb), lambda j: (0, j))
    return pl.pallas_call(
        _dft_inv_outer_kernel,
        grid=(cols // cb,),
        in_specs=[big, big,
                  pl.BlockSpec(gh.shape, lambda j: (0, 0)),
                  pl.BlockSpec(gl.shape, lambda j: (0, 0)),
                  pl.BlockSpec((rows, cb), lambda j: (u_part, j)),
                  pl.BlockSpec((rows, cb), lambda j: (gate_part, j)),
                  pl.BlockSpec((1, cb), lambda j: (0, 0))],
        out_specs=pl.BlockSpec((rows, cb), lambda j: (0, j)),
        out_shape=jax.ShapeDtypeStruct((rows, cols), out_dtype),
        compiler_params=_cparams("parallel"),
        name="dft_inv_outer",
    )(br, bi, gh, gl, u2d, gate2d, jnp.tile(skip_row.reshape(1, -1).astype(F32), (1, cb // skip_row.shape[-1])))


def _hyena_mixer(z, p, tables, seq, width):
    n1, n2 = tables["n1"], tables["n2"]
    rows = n1 // 2
    cb_data = 4096
    zc = _hyena_conv(z, p["hy_conv_w"], p["hy_conv_b"], seq, width)
    zc2d = zc.reshape(3 * rows, n2 * width)
    kf, colsum = _hyena_filters(seq, width, p["hy_filt_w1"], p["hy_filt_b1"], p["hy_filt_w2"],
                                p["hy_filt_b2"], p["hy_filt_w3"], p["hy_filt_freq"])
    far, fai = _dft_outer(kf.reshape(n1, n2 * HY_ORDER * width), 0, n1, tables["outer"], n1, 2048)
    kr, ki = _dft_slab_filter(far.reshape(n1 * n2, -1), fai.reshape(n1 * n2, -1), colsum, tables, width)

    def conv(u2d, u_part, gate_part, order, out_dtype):
        ar, ai = _dft_outer(u2d, u_part, rows, tables["outer_half"], n1, cb_data)
        br, bi = _dft_slab(ar.reshape(n1 * n2, width), ai.reshape(n1 * n2, width), kr, ki, order, tables, width)
        return _dft_inv_outer(br.reshape(n1, n2 * width), bi.reshape(n1, n2 * width), tables,
                              u2d, u_part, zc2d, gate_part, p["hy_skip"][order], out_dtype, 2048)

    y1 = conv(zc2d, 0, 1, 0, F32)
    y = conv(y1, 0, 2, 1, BF16)
    return y.reshape(seq, width)


def _attn_kernel(q_ref, kp_ref, kc_ref, kn_ref, vp_ref, vc_ref, vn_ref, sink_ref, o_ref, *, seq, group):
    i = pl.program_id(0)
    h = pl.program_id(1)
    blk = WINDOW
    q = q_ref[...]
    q4 = jnp.concatenate([q[:, g * HEAD_DIM:(g + 1) * HEAD_DIM] for g in range(group)], axis=0)
    k = jnp.concatenate([kp_ref[...], kc_ref[...], kn_ref[...]], axis=0)
    v = jnp.concatenate([vp_ref[...], vc_ref[...], vn_ref[...]], axis=0)
    s = lax.dot_general(q4, k, (((1,), (1,)), ((), ())), preferred_element_type=F32)
    s = s * (HEAD_DIM ** -0.5)
    q_pos = i * blk + lax.broadcasted_iota(jnp.int32, (blk, 3 * blk), 0)
    k_pos = i * blk - blk + lax.broadcasted_iota(jnp.int32, (blk, 3 * blk), 1)
    dist = jnp.abs(q_pos - k_pos)
    valid = (dist <= WINDOW) & (k_pos >= 0) & (k_pos < seq)
    distf = dist.astype(F32)
    outs = []
    for g in range(group):
        sg = s[g * blk:(g + 1) * blk] - sink_ref[KV_HEADS + h, g] * distf
        sg = jnp.where(valid, sg, NEG)
        sink = sink_ref[h, g]
        m = jnp.maximum(jnp.max(sg, axis=-1, keepdims=True), sink)
        e = jnp.exp(sg - m)
        denom = jnp.sum(e, axis=-1, keepdims=True) + jnp.exp(sink - m)
        pg = (e / denom).astype(v.dtype)
        outs.append(_dot(pg, v))
    o_ref[...] = jnp.concatenate(outs, axis=1).astype(o_ref.dtype)


def _attention(z, sink, seq, width, q_off, k_off, v_off):
    group = width // HEAD_DIM // KV_HEADS
    gw = group * HEAD_DIM
    nb = seq // WINDOW
    qb, kb, vb = q_off // gw, k_off // HEAD_DIM, v_off // HEAD_DIM
    assert q_off % gw == 0 and k_off % HEAD_DIM == 0 and v_off % HEAD_DIM == 0
    prev = lambda i: jnp.maximum(i - 1, 0)
    nxt = lambda i: jnp.minimum(i + 1, nb - 1)
    kv = lambda base, f: pl.BlockSpec((WINDOW, HEAD_DIM), lambda i, h: (f(i), base + h))
    same = lambda i: i
    n_q = KV_HEADS * group
    slopes = 2.0 ** (-8.0 * (jnp.arange(n_q, dtype=F32) + 1.0) / n_q)
    scalars = jnp.concatenate([sink.astype(F32), slopes]).reshape(2 * KV_HEADS, group)
    return pl.pallas_call(
        functools.partial(_attn_kernel, seq=seq, group=group),
        grid=(nb, KV_HEADS),
        in_specs=[pl.BlockSpec((WINDOW, gw), lambda i, h: (i, qb + h)),
                  kv(kb, prev), kv(kb, same), kv(kb, nxt),
                  kv(vb, prev), kv(vb, same), kv(vb, nxt),
                  pl.BlockSpec(memory_space=pltpu.SMEM)],
        out_specs=pl.BlockSpec((WINDOW, gw), lambda i, h: (i, h)),
        out_shape=jax.ShapeDtypeStruct((seq, width), BF16),
        compiler_params=_cparams("parallel", "parallel"),
        name="window_attention",
    )(z, z, z, z, z, z, z, scalars)


def _retention_kernel(*refs, heads, reverse, final):
    if final:
        lg_ref, q_ref, k_ref, v_ref, prev_ref, g_ref, o_ref, s_ref = refs
    else:
        lg_ref, q_ref, k_ref, v_ref, o_ref, s_ref = refs
    hg = pl.program_id(0)
    n = pl.program_id(1)
    c = RET_CHUNK

    @pl.when(n == 0)
    def _():
        s_ref[...] = jnp.zeros_like(s_ref)

    ri = lax.broadcasted_iota(jnp.int32, (c, c), 0)
    ci = lax.broadcasted_iota(jnp.int32, (c, c), 1)
    rel = (ci - ri) if reverse else (ri - ci)
    relf = jnp.maximum(rel, 0).astype(F32)
    idx = lax.broadcasted_iota(jnp.int32, (c, 1), 0).astype(F32)
    scale = HEAD_DIM ** -0.5
    for hd in range(heads):
        lg = lg_ref[hg * heads + hd]
        sl = slice(hd * HEAD_DIM, (hd + 1) * HEAD_DIM)
        q, k, v = q_ref[:, sl], k_ref[:, sl], v_ref[:, sl]
        decay = jnp.where(rel >= 0, jnp.exp(relf * lg), 0.0)
        k_w = jnp.exp((idx if reverse else (c - 1.0 - idx)) * lg) * scale
        q_w = jnp.exp(((c - idx) if reverse else (idx + 1.0)) * lg)
        chunk_decay = jnp.exp(jnp.full((1, 1), float(c), F32) * lg)
        qk = lax.dot_general(q, k, (((1,), (1,)), ((), ())), preferred_element_type=F32)
        inner = _dot((qk * scale * decay).astype(v.dtype), v)
        s_prev = s_ref[hd]
        cross = _dot((q.astype(F32) * q_w).astype(BF16), s_prev.astype(BF16))
        kv = lax.dot_general((k.astype(F32) * k_w).astype(BF16), v, (((0,), (0,)), ((), ())),
                             preferred_element_type=F32)
        s_ref[hd] = s_prev * chunk_decay + kv
        o = inner + cross
        if final:
            o = o + prev_ref[:, sl]
            o = o * lax.rsqrt(jnp.mean(o * o, axis=-1, keepdims=True) + EPS)
            g = g_ref[:, sl].astype(F32)
            o = g * jax.nn.sigmoid(g) * o
        o_ref[:, sl] = o.astype(o_ref.dtype)


def _retention_pass(z, seq, width, offs, reverse, prev=None):
    nc = seq // RET_CHUNK
    gw = math.gcd(width, *offs)
    assert gw % HEAD_DIM == 0
    heads = gw // HEAD_DIM
    q_off, k_off, v_off, g_off = offs
    hidx = jnp.arange(width // HEAD_DIM, dtype=F32)
    if reverse:
        log_gammas = jnp.log(1.0 - 2.0 ** (-5.5 - hidx))
        order = lambda n: nc - 1 - n
    else:
        log_gammas = jnp.log(1.0 - 2.0 ** (-5.0 - hidx))
        order = lambda n: n
    final = prev is not None
    zspec = lambda off: pl.BlockSpec((RET_CHUNK, gw), lambda g, n: (order(n), off // gw + g))
    own = pl.BlockSpec((RET_CHUNK, gw), lambda g, n: (order(n), g))
    in_specs = [pl.BlockSpec(memory_space=pltpu.SMEM), zspec(q_off), zspec(k_off), zspec(v_off)]
    args = [log_gammas, z, z, z]
    if final:
        in_specs += [own, zspec(g_off)]
        args += [prev, z]
    return pl.pallas_call(
        functools.partial(_retention_kernel, heads=heads, reverse=reverse, final=final),
        grid=(width // gw, nc),
        in_specs=in_specs,
        out_specs=own,
        out_shape=jax.ShapeDtypeStruct((seq, width), BF16 if final else F32),
        scratch_shapes=[pltpu.VMEM((heads, HEAD_DIM, HEAD_DIM), F32)],
        compiler_params=_cparams("parallel", "arbitrary"),
        name="retention_fwd" if final else "retention_bwd",
    )(*args)


def _retention(z, seq, width, offs):
    o_bwd = _retention_pass(z, seq, width, offs, reverse=True)
    return _retention_pass(z, seq, width, offs, reverse=False, prev=o_bwd)


def _merge_kernel(ya_ref, yb_ref, yc_ref, wa_ref, wb_ref, wc_ref, ga_ref, gb_ref, gc_ref, o_ref):
    acc = None
    for y_ref, w_ref, g_ref in ((ya_ref, wa_ref, ga_ref), (yb_ref, wb_ref, gb_ref), (yc_ref, wc_ref, gc_ref)):
        t = jax.nn.sigmoid(g_ref[...].astype(F32)) * _dot(y_ref[...], w_ref[...])
        acc = t if acc is None else acc + t
    o_ref[...] = acc.astype(o_ref.dtype)


def _merge(ya, yb, yc, w_branch, z, gate_off, d_model):
    seq, width = ya.shape
    tm, tn = _tile(seq, 512), _tile(d_model, 512)
    assert gate_off % tn == 0
    ys = pl.BlockSpec((tm, width), lambda i, j: (i, 0))
    ws = lambda b: pl.BlockSpec((None, width, tn), lambda i, j: (b, 0, j))
    gs = lambda b: pl.BlockSpec((tm, tn), lambda i, j: (i, (gate_off + b * d_model) // tn + j))
    return pl.pallas_call(
        _merge_kernel,
        grid=(seq // tm, d_model // tn),
        in_specs=[ys, ys, ys, ws(0), ws(1), ws(2), gs(0), gs(1), gs(2)],
        out_specs=pl.BlockSpec((tm, tn), lambda i, j: (i, j)),
        out_shape=jax.ShapeDtypeStruct((seq, d_model), BF16),
        compiler_params=_cparams("parallel", "parallel"),
        name="branch_merge",
    )(ya, yb, yc, w_branch, w_branch, w_branch, z, z, z)


def _ffn_in_kernel(h_ref, hp_ref, hn_ref, wg_ref, wu_ref, cw_ref, cb_ref, o_ref, *, n_tiles):
    i = pl.program_id(0)
    wg = wg_ref[...]
    g = _dot(h_ref[...], wg)
    prev_row = _dot(hp_ref[...], wg)[HALO - 1:HALO] * (i > 0).astype(F32)
    next_row = _dot(hn_ref[...], wg)[0:1] * (i < n_tiles - 1).astype(F32)
    down, up = _shift_rows(g, prev_row, next_row)
    cw = cw_ref[...]
    c = down * cw[0:1] + g * cw[1:2] + up * cw[2:3] + cb_ref[...]
    gelu = 0.5 * c * (1.0 + lax.erf(c * (2.0 ** -0.5)))
    o_ref[...] = (gelu * _dot(h_ref[...], wu_ref[...])).astype(o_ref.dtype)


def _ffn_in(h, w_in, conv_w, conv_b, ffp):
    seq, d = h.shape
    tm, tn = _tile(seq, 1024), _tile(ffp, 512)
    n_tiles, n_halo, per = seq // tm, seq // HALO, tm // HALO
    nj = ffp // tn
    return pl.pallas_call(
        functools.partial(_ffn_in_kernel, n_tiles=n_tiles),
        grid=(n_tiles, nj),
        in_specs=[
            pl.BlockSpec((tm, d), lambda i, j: (i, 0)),
            pl.BlockSpec((HALO, d), lambda i, j: (jnp.maximum(i * per - 1, 0), 0)),
            pl.BlockSpec((HALO, d), lambda i, j: (jnp.minimum((i + 1) * per, n_halo - 1), 0)),
            pl.BlockSpec((d, tn), lambda i, j: (0, j)),
            pl.BlockSpec((d, tn), lambda i, j: (0, nj + j)),
            pl.BlockSpec((3, tn), lambda i, j: (0, j)),
            pl.BlockSpec((1, tn), lambda i, j: (0, j)),
        ],
        out_specs=pl.BlockSpec((tm, tn), lambda i, j: (i, j)),
        out_shape=jax.ShapeDtypeStruct((seq, ffp), BF16),
        compiler_params=_cparams("parallel", "parallel"),
        name="ffn_in",
    )(h, h, h, w_in, w_in, conv_w, conv_b)


def _prepare_layer(l, w):
    ff = w["w_ffn_out"].shape[1]
    ffp = -(-ff // FF_PAD) * FF_PAD
    pad = ffp - ff
    w_ffn_in = w["w_ffn_in"][l]
    gate = jnp.pad(w_ffn_in[:, :ff], ((0, 0), (0, pad)))
    up = jnp.pad(w_ffn_in[:, ff:], ((0, 0), (0, pad)))
    p = {name: arr[l] for name, arr in w.items()}
    p.update(
        w_in=w["w_in"][l].astype(BF16),
        w_branch=w["w_branch"][l].astype(BF16),
        w_out=w["w_out"][l].astype(BF16),
        w_ffn_in=jnp.concatenate([gate, up], axis=1).astype(BF16),
        w_ffn_out=jnp.pad(w["w_ffn_out"][l], ((0, pad), (0, 0))).astype(BF16),
        ffn_conv_w=jnp.pad(w["ffn_conv_w"][l], ((0, 0), (0, pad))).astype(F32),
        ffn_conv_b=jnp.pad(w["ffn_conv_b"][l], ((0, pad),)).reshape(1, -1).astype(F32),
        ffp=ffp,
    )
    return p


def _layer(x, p, tables):
    seq, d = x.shape
    width = d // 4
    kvw = KV_HEADS * HEAD_DIM
    sizes = [3 * width, width, kvw, kvw, width, width, width, width, 3 * d]
    offs = [0]
    for s in sizes[:-1]:
        offs.append(offs[-1] + s)
    h = _rmsnorm(x, p["norm_mix"], BF16)
    z = _matmul(h, p["w_in"], tm=1024, tn=512, out_dtype=BF16, name="in_proj")
    y_a = _hyena_mixer(z, p, tables, seq, width)
    y_b = _attention(z, p["attn_sink"], seq, width, offs[1], offs[2], offs[3])
    y_c = _retention(z, seq, width, offs[4:8])
    merged = _merge(y_a, y_b, y_c, p["w_branch"], z, offs[8], d)
    x = _matmul(merged, p["w_out"], tm=1024, tn=512, out_dtype=F32, residual=x, name="out_proj")
    h = _rmsnorm(x, p["norm_ffn"], BF16)
    act = _ffn_in(h, p["w_ffn_in"], p["ffn_conv_w"], p["ffn_conv_b"], p["ffp"])
    return _matmul(act, p["w_ffn_out"], tm=1024, tn=1024, tk=1024, out_dtype=F32, residual=x, name="ffn_out")


def _trunk(x, layers, norm_final):
    b, seq, d = x.shape
    assert b == 1
    tables = _dft_tables(seq)
    x = x.reshape(seq, d)
    for p in layers:
        x = _layer(x, p, tables)
    return _rmsnorm(x, norm_final, F32).reshape(b, seq, d)


def kernel(x_prompt, x_sample, norm_mix, w_in, hy_conv_w, hy_conv_b, hy_filt_w1, hy_filt_b1, hy_filt_w2, hy_filt_b2, hy_filt_w3, hy_filt_freq, hy_skip, attn_sink, w_branch, w_out, norm_ffn, w_ffn_in, ffn_conv_w, ffn_conv_b, w_ffn_out, norm_final):
    w = dict(norm_mix=norm_mix, w_in=w_in, hy_conv_w=hy_conv_w, hy_conv_b=hy_conv_b,
             hy_filt_w1=hy_filt_w1, hy_filt_b1=hy_filt_b1, hy_filt_w2=hy_filt_w2,
             hy_filt_b2=hy_filt_b2, hy_filt_w3=hy_filt_w3, hy_filt_freq=hy_filt_freq,
             hy_skip=hy_skip, attn_sink=attn_sink, w_branch=w_branch, w_out=w_out,
             norm_ffn=norm_ffn, w_ffn_in=w_ffn_in, ffn_conv_w=ffn_conv_w,
             ffn_conv_b=ffn_conv_b, w_ffn_out=w_ffn_out)
    layers = [_prepare_layer(l, w) for l in range(norm_mix.shape[0])]
    return (_trunk(x_prompt, layers, norm_final), _trunk(x_sample, layers, norm_final))
```

```python
import functools
import math

import jax
import jax.numpy as jnp
from jax import lax
from jax.experimental import pallas as pl
from jax.experimental.pallas import tpu as pltpu

F32 = jnp.float32
BF16 = jnp.bfloat16

EPS = 1e-6
NEG = -1e30
HEAD_DIM = 128
KV_HEADS = 2
WINDOW = 128
RET_CHUNK = 128
HY_ORDER = 2
HY_EMB = 33
HY_BANDS = (HY_EMB - 1) // 2
HY_MIN_DECAY = math.log(1e-2) / 1.5
HY_MAX_DECAY = math.log(1e-2) / 0.3
DFT_INNER = 128
FFT_CB = 16
FFT_PASSES = 1
FEAT_LANES = 128
HALO = 16
FF_PAD = 1024
VMEM_LIMIT = 56 * 1024 * 1024


def _cparams(*sem, vmem=VMEM_LIMIT):
    return pltpu.CompilerParams(dimension_semantics=sem, vmem_limit_bytes=vmem)


def _tile(n, pref):
    if n <= pref:
        return n
    t = (pref // 128) * 128
    while t >= 128:
        if n % t == 0:
            return t
        t -= 128
    return n


def _dot(a, b):
    return jnp.dot(a, b, preferred_element_type=F32)


def _split(x):
    hi = x.astype(BF16)
    lo = (x - hi.astype(F32)).astype(BF16)
    return hi, lo


def _dot3(a, b):
    a_hi, a_lo = _split(a)
    b_hi, b_lo = _split(b)
    return _dot(a_hi, b_hi) + (_dot(a_lo, b_hi) + _dot(a_hi, b_lo))


def _rmsnorm_kernel(x_ref, g_ref, o_ref):
    x = x_ref[...]
    ms = jnp.mean(x * x, axis=-1, keepdims=True)
    o_ref[...] = (x * lax.rsqrt(ms + EPS) * g_ref[...]).astype(o_ref.dtype)


def _rmsnorm(x, g, out_dtype):
    m, d = x.shape
    tm = _tile(m, 256)
    return pl.pallas_call(
        _rmsnorm_kernel,
        grid=(m // tm,),
        in_specs=[pl.BlockSpec((tm, d), lambda i: (i, 0)),
                  pl.BlockSpec((1, d), lambda i: (0, 0))],
        out_specs=pl.BlockSpec((tm, d), lambda i: (i, 0)),
        out_shape=jax.ShapeDtypeStruct((m, d), out_dtype),
        compiler_params=_cparams("parallel"),
        name="rmsnorm",
    )(x, g.reshape(1, d).astype(F32))


def _matmul_kernel(*refs, nk, has_res):
    if has_res:
        a_ref, b_ref, r_ref, o_ref = refs[:4]
    else:
        a_ref, b_ref, o_ref = refs[:3]
        r_ref = None
    p = _dot(a_ref[...], b_ref[...])

    def finish(acc):
        if has_res:
            acc = acc + r_ref[...]
        o_ref[...] = acc.astype(o_ref.dtype)

    if nk == 1:
        finish(p)
        return
    acc_ref = refs[-1]
    k = pl.program_id(2)

    @pl.when(k == 0)
    def _():
        acc_ref[...] = p

    @pl.when(k > 0)
    def _():
        acc_ref[...] += p

    @pl.when(k == nk - 1)
    def _():
        finish(acc_ref[...])


def _matmul(a, b, *, tm, tn, tk=None, out_dtype, residual=None, name):
    m, kdim = a.shape
    n = b.shape[1]
    tm, tn = _tile(m, tm), _tile(n, tn)
    tk = kdim if tk is None else _tile(kdim, tk)
    nk = kdim // tk
    in_specs = [pl.BlockSpec((tm, tk), lambda i, j, k: (i, k)),
                pl.BlockSpec((tk, tn), lambda i, j, k: (k, j))]
    args = [a, b]
    if residual is not None:
        in_specs.append(pl.BlockSpec((tm, tn), lambda i, j, k: (i, j)))
        args.append(residual)
    scratch = [pltpu.VMEM((tm, tn), F32)] if nk > 1 else []
    return pl.pallas_call(
        functools.partial(_matmul_kernel, nk=nk, has_res=residual is not None),
        grid=(m // tm, n // tn, nk),
        in_specs=in_specs,
        out_specs=pl.BlockSpec((tm, tn), lambda i, j, k: (i, j)),
        out_shape=jax.ShapeDtypeStruct((m, n), out_dtype),
        scratch_shapes=scratch,
        compiler_params=_cparams("parallel", "parallel", "arbitrary"),
        name=name,
    )(*args)


def _shift_rows(x, prev_row, next_row):
    tm = x.shape[0]
    row = lax.broadcasted_iota(jnp.int32, x.shape, 0)
    down = jnp.where(row == 0, prev_row, pltpu.roll(x, 1, axis=0))
    up = jnp.where(row == tm - 1, next_row, pltpu.roll(x, tm - 1, axis=0))
    return down, up


def _hyconv_kernel(x_ref, xp_ref, xn_ref, w_ref, b_ref, o_ref, *, n_tiles):
    i = pl.program_id(0)
    x = x_ref[...].astype(F32)
    prev_row = xp_ref[...].astype(F32)[HALO - 1:HALO] * (i > 0).astype(F32)
    next_row = xn_ref[...].astype(F32)[0:1] * (i < n_tiles - 1).astype(F32)
    down, up = _shift_rows(x, prev_row, next_row)
    w = w_ref[...]
    o_ref[...] = down * w[0:1] + x * w[1:2] + up * w[2:3] + b_ref[...]


def _hyena_conv(z, conv_w, conv_b, seq, width):
    tm = _tile(seq, 512)
    n_tiles = seq // tm
    n_halo = seq // HALO
    per = tm // HALO
    return pl.pallas_call(
        functools.partial(_hyconv_kernel, n_tiles=n_tiles),
        grid=(n_tiles, 3),
        in_specs=[
            pl.BlockSpec((tm, width), lambda i, p: (i, p)),
            pl.BlockSpec((HALO, width), lambda i, p: (jnp.maximum(i * per - 1, 0), p)),
            pl.BlockSpec((HALO, width), lambda i, p: (jnp.minimum((i + 1) * per, n_halo - 1), p)),
            pl.BlockSpec((3, width), lambda i, p: (0, p)),
            pl.BlockSpec((1, width), lambda i, p: (0, p)),
        ],
        out_specs=pl.BlockSpec((None, tm, width), lambda i, p: (p, i, 0)),
        out_shape=jax.ShapeDtypeStruct((3, seq, width), F32),
        compiler_params=_cparams("parallel", "parallel"),
        name="hyena_conv3",
    )(z, z, z, conv_w.astype(F32), conv_b.reshape(1, -1).astype(F32))


def _filter_kernel(bands_ref, w1_ref, b1_ref, w2_ref, b2_ref, freq_ref, w3_ref, delta_ref,
                   k_ref, sum_ref, *, seq, tr):
    i = pl.program_id(1)
    n = 2 * seq
    r = i * tr + lax.broadcasted_iota(jnp.int32, (tr, FEAT_LANES), 0)
    pos = jnp.where(r < seq, r, n - r).astype(F32)
    t = pos / float(seq - 1)
    lane = lax.broadcasted_iota(jnp.int32, (tr, FEAT_LANES), 1)
    ang = (2.0 * math.pi / seq) * pos * bands_ref[...]
    feats = jnp.where(lane < HY_BANDS, jnp.cos(ang),
                      jnp.where(lane < 2 * HY_BANDS, -jnp.sin(ang),
                                jnp.where(lane == 2 * HY_BANDS, t, 0.0)))
    freq = freq_ref[...]
    h = jnp.sin(freq * (_dot3(feats, w1_ref[...]) + b1_ref[...]))
    h = jnp.sin(freq * (_dot3(h, w2_ref[...]) + b2_ref[...]))
    h = _dot3(h, w3_ref[...])
    r1 = i * tr + lax.broadcasted_iota(jnp.int32, (tr, 1), 0)
    t1 = jnp.where(r1 < seq, r1, n - r1).astype(F32) / float(seq - 1)
    kf = h * jnp.exp(-t1 * delta_ref[...])
    kf = jnp.where(r1 == seq, 0.0, kf)
    k_ref[...] = kf
    part = jnp.sum(jnp.abs(kf), axis=0, keepdims=True)

    @pl.when(i == 0)
    def _():
        sum_ref[...] = part

    @pl.when(i > 0)
    def _():
        sum_ref[...] += part


def _hyena_filters(seq, width, w1, b1, w2, b2, w3, freq):
    n = 2 * seq
    tr = _tile(seq, 512)
    nb = n // tr
    fw = w1.shape[1]
    bands = jnp.linspace(1e-4, HY_BANDS - 1, HY_BANDS, dtype=F32)
    bands = jnp.concatenate([bands, bands, jnp.zeros((FEAT_LANES - 2 * HY_BANDS,), F32)]).reshape(1, -1)
    w1 = w1.astype(F32)
    w1p = jnp.concatenate([w1[1:], w1[:1], jnp.zeros((FEAT_LANES - HY_EMB, fw), F32)], axis=0)
    deltas = jnp.abs(jnp.linspace(HY_MIN_DECAY, HY_MAX_DECAY, width, dtype=F32)).reshape(1, -1)
    row = lambda a: a.reshape(1, -1).astype(F32)
    const = lambda shape: pl.BlockSpec(shape, lambda o, i: (0, 0))
    return pl.pallas_call(
        functools.partial(_filter_kernel, seq=seq, tr=tr),
        grid=(HY_ORDER, nb),
        in_specs=[
            const((1, FEAT_LANES)), const((FEAT_LANES, fw)), const((1, fw)), const((fw, fw)),
            const((1, fw)), const((1, fw)),
            pl.BlockSpec((fw, width), lambda o, i: (0, 2 * o + (i >= nb // 2).astype(jnp.int32))),
            const((1, width)),
        ],
        out_specs=[pl.BlockSpec((tr, width), lambda o, i: (i, o)),
                   pl.BlockSpec((1, width), lambda o, i: (0, o))],
        out_shape=[jax.ShapeDtypeStruct((n, HY_ORDER * width), F32),
                   jax.ShapeDtypeStruct((1, HY_ORDER * width), F32)],
        compiler_params=_cparams("parallel", "arbitrary"),
        name="hyena_filters",
    )(bands, w1p, row(b1), w2.astype(F32), row(b2), row(freq), w3.astype(F32), deltas)


def _fft_tables(seq):
    n = 2 * seq
    n2 = DFT_INNER
    n1 = n // n2
    r = n1 // 2
    k1 = jnp.arange(n1, dtype=jnp.int32)
    th = (2.0 * math.pi / n1) * ((k1[:, None] * k1[None, :]) % n1).astype(F32)
    c, s = jnp.cos(th), jnp.sin(th)
    outer = jnp.concatenate([c, -s], axis=0)
    j = jnp.arange(n2, dtype=jnp.int32)
    ph = (2.0 * math.pi / n) * ((k1[:, None] * j[None, :]) % n).astype(F32)
    th2 = (2.0 * math.pi / n2) * ((j[:, None] * j[None, :]) % n2).astype(F32)
    cr, ci = jnp.cos(th2), -jnp.sin(th2)
    inner = jnp.concatenate([jnp.concatenate([cr, ci], axis=1),
                             jnp.concatenate([-ci, cr], axis=1)], axis=0)
    inv_inner = jnp.concatenate([jnp.concatenate([cr, -ci], axis=1),
                                 jnp.concatenate([ci, cr], axis=1)], axis=0) / n2
    return dict(n1=n1, n2=n2, outer=_split(outer), outer_half=_split(outer[:, :r]),
                inv_outer_re=_split(c[:r] / n1), inv_outer_im=_split(-s[:r] / n1),
                tw_re=jnp.cos(ph), tw_im=-jnp.sin(ph),
                inner=_split(inner), inv_inner=_split(inv_inner))


def _mat_dot(m_hi, m_lo, x, passes):
    x_hi = x.astype(BF16)
    out = _dot(m_hi, x_hi)
    if passes >= 2:
        out = out + _dot(m_hi, (x - x_hi.astype(F32)).astype(BF16))
    if passes >= 3:
        out = out + _dot(m_lo, x_hi)
    return out


def _dot_mat(x, m_hi, m_lo, passes):
    x_hi = x.astype(BF16)
    out = _dot(x_hi, m_hi)
    if passes >= 2:
        out = out + _dot((x - x_hi.astype(F32)).astype(BF16), m_hi)
    if passes >= 3:
        out = out + _dot(x_hi, m_lo)
    return out


def _forward_to_scratch(x_ref, a_scr, f1h, f1l, tw_re, tw_im, n1, cb, passes):
    half = DFT_INNER
    for c in range(cb):
        p = _mat_dot(f1h, f1l, x_ref[c], passes)
        ar, ai = p[:n1], p[n1:]
        a_scr[c, :, :half] = ar * tw_re - ai * tw_im
        a_scr[c, :, half:] = ar * tw_im + ai * tw_re


def _filter_spectrum_kernel(k_ref, inv_ref, f1h_ref, f1l_ref, twr_ref, twi_ref, mh_ref, ml_ref,
                            kr_ref, ki_ref, a_scr, *, n1, cb):
    half = DFT_INNER
    _forward_to_scratch(k_ref, a_scr, f1h_ref[...], f1l_ref[...], twr_ref[...], twi_ref[...], n1, cb, 3)
    x = _dot_mat(a_scr[...].reshape(cb * n1, 2 * half), mh_ref[...], ml_ref[...], 3)
    x = x.reshape(cb, n1, 2 * half)
    inv = inv_ref[...]
    kr_ref[...] = x[:, :, :half] * inv
    ki_ref[...] = x[:, :, half:] * inv


def _const_spec(arr):
    return pl.BlockSpec(arr.shape, lambda j: (0,) * arr.ndim)


def _filter_spectrum(kf_t, inv_sum, tables):
    ch, n1, n2 = kf_t.shape
    cb = FFT_CB
    f1h, f1l = tables["outer"]
    mh, ml = tables["inner"]
    consts = [f1h, f1l, tables["tw_re"], tables["tw_im"], mh, ml]
    blk = pl.BlockSpec((cb, n1, n2), lambda j: (j, 0, 0))
    out = jax.ShapeDtypeStruct((ch, n1, n2), F32)
    return pl.pallas_call(
        functools.partial(_filter_spectrum_kernel, n1=n1, cb=cb),
        grid=(ch // cb,),
        in_specs=[blk, pl.BlockSpec((cb, 1, n2), lambda j: (j, 0, 0))] + [_const_spec(a) for a in consts],
        out_specs=[blk, blk],
        out_shape=[out, out],
        scratch_shapes=[pltpu.VMEM((cb, n1, 2 * n2), F32)],
        compiler_params=_cparams("parallel"),
        name="filter_spectrum",
    )(kf_t, inv_sum, *consts)


def _fftconv_kernel(u_ref, gate_ref, kr_ref, ki_ref, skip_ref, f1h_ref, f1l_ref, twr_ref, twi_ref,
                    mh_ref, ml_ref, nh_ref, nl_ref, grh_ref, grl_ref, gih_ref, gil_ref,
                    o_ref, a_scr, *, n1, cb, passes):
    half = DFT_INNER
    tw_re, tw_im = twr_ref[...], twi_ref[...]
    _forward_to_scratch(u_ref, a_scr, f1h_ref[...], f1l_ref[...], tw_re, tw_im, n1, cb, passes)
    x = _dot_mat(a_scr[...].reshape(cb * n1, 2 * half), mh_ref[...], ml_ref[...], passes)
    xr, xi = x[:, :half], x[:, half:]
    kr = kr_ref[...].reshape(cb * n1, half)
    ki = ki_ref[...].reshape(cb * n1, half)
    y = jnp.concatenate([xr * kr - xi * ki, xr * ki + xi * kr], axis=1)
    b = _dot_mat(y, nh_ref[...], nl_ref[...], passes)
    a_scr[...] = b.reshape(cb, n1, 2 * half)
    grh, grl, gih, gil = grh_ref[...], grl_ref[...], gih_ref[...], gil_ref[...]
    for c in range(cb):
        br, bi = a_scr[c, :, :half], a_scr[c, :, half:]
        conv = (_mat_dot(grh, grl, br * tw_re + bi * tw_im, passes)
                + _mat_dot(gih, gil, bi * tw_re - br * tw_im, passes))
        o_ref[c] = (gate_ref[c] * (conv + u_ref[c] * skip_ref[c])).astype(o_ref.dtype)


def _fftconv(u_t, u_part, gate_t, gate_part, kr, ki, order, skip_t, tables, width, out_dtype):
    n1, n2 = tables["n1"], tables["n2"]
    rows = n1 // 2
    cb = FFT_CB
    nb = width // cb
    consts = [*tables["outer_half"], tables["tw_re"], tables["tw_im"], *tables["inner"],
              *tables["inv_inner"], *tables["inv_outer_re"], *tables["inv_outer_im"]]
    data = lambda part: pl.BlockSpec((cb, rows, n2), lambda j: (part * nb + j, 0, 0))
    spec = pl.BlockSpec((cb, n1, n2), lambda j: (order * nb + j, 0, 0))
    return pl.pallas_call(
        functools.partial(_fftconv_kernel, n1=n1, cb=cb, passes=FFT_PASSES),
        grid=(nb,),
        in_specs=[data(u_part), data(gate_part), spec, spec,
                  pl.BlockSpec((cb, 1, n2), lambda j: (j, 0, 0))] + [_const_spec(a) for a in consts],
        out_specs=pl.BlockSpec((cb, rows, n2), lambda j: (j, 0, 0)),
        out_shape=jax.ShapeDtypeStruct((width, rows, n2), out_dtype),
        scratch_shapes=[pltpu.VMEM((cb, n1, 2 * n2), F32)],
        compiler_params=_cparams("parallel"),
        name="hyena_fftconv",
    )(u_t, gate_t, kr, ki, skip_t, *consts)


def _lane_rows(v):
    return jnp.broadcast_to(v.astype(F32).reshape(-1, 1, 1), (v.shape[0], 1, DFT_INNER))


def _hyena_mixer(z, p, tables, seq, width):
    n1, n2 = tables["n1"], tables["n2"]
    rows = n1 // 2
    zc = _hyena_conv(z, p["hy_conv_w"], p["hy_conv_b"], seq, width)
    zc_t = jnp.transpose(zc, (0, 2, 1)).reshape(3 * width, rows, n2)
    kf, colsum = _hyena_filters(seq, width, p["hy_filt_w1"], p["hy_filt_b1"], p["hy_filt_w2"],
                                p["hy_filt_b2"], p["hy_filt_w3"], p["hy_filt_freq"])
    kf_t = kf.T.reshape(HY_ORDER * width, n1, n2)
    kr, ki = _filter_spectrum(kf_t, _lane_rows(1.0 / colsum.reshape(-1)), tables)
    skip = p["hy_skip"]
    y1 = _fftconv(zc_t, 0, zc_t, 1, kr, ki, 0, _lane_rows(skip[0]), tables, width, F32)
    y = _fftconv(y1, 0, zc_t, 2, kr, ki, 1, _lane_rows(skip[1]), tables, width, BF16)
    return y.reshape(width, seq).T


def _attn_kernel(q_ref, kp_ref, kc_ref, kn_ref, vp_ref, vc_ref, vn_ref, sink_ref, o_ref, *, seq, group):
    i = pl.program_id(0)
    h = pl.program_id(1)
    blk = WINDOW
    q = q_ref[...]
    q4 = jnp.concatenate([q[:, g * HEAD_DIM:(g + 1) * HEAD_DIM] for g in range(group)], axis=0)
    k = jnp.concatenate([kp_ref[...], kc_ref[...], kn_ref[...]], axis=0)
    v = jnp.concatenate([vp_ref[...], vc_ref[...], vn_ref[...]], axis=0)
    s = lax.dot_general(q4, k, (((1,), (1,)), ((), ())), preferred_element_type=F32)
    s = s * (HEAD_DIM ** -0.5)
    q_pos = i * blk + lax.broadcasted_iota(jnp.int32, (blk, 3 * blk), 0)
    k_pos = i * blk - blk + lax.broadcasted_iota(jnp.int32, (blk, 3 * blk), 1)
    dist = jnp.abs(q_pos - k_pos)
    valid = (dist <= WINDOW) & (k_pos >= 0) & (k_pos < seq)
    distf = dist.astype(F32)
    outs = []
    for g in range(group):
        sg = s[g * blk:(g + 1) * blk] - sink_ref[KV_HEADS + h, g] * distf
        sg = jnp.where(valid, sg, NEG)
        sink = sink_ref[h, g]
        m = jnp.maximum(jnp.max(sg, axis=-1, keepdims=True), sink)
        e = jnp.exp(sg - m)
        denom = jnp.sum(e, axis=-1, keepdims=True) + jnp.exp(sink - m)
        pg = (e / denom).astype(v.dtype)
        outs.append(_dot(pg, v))
    o_ref[...] = jnp.concatenate(outs, axis=1).astype(o_ref.dtype)


def _attention(z, sink, seq, width, q_off, k_off, v_off):
    group = width // HEAD_DIM // KV_HEADS
    gw = group * HEAD_DIM
    nb = seq // WINDOW
    qb, kb, vb = q_off // gw, k_off // HEAD_DIM, v_off // HEAD_DIM
    assert q_off % gw == 0 and k_off % HEAD_DIM == 0 and v_off % HEAD_DIM == 0
    prev = lambda i: jnp.maximum(i - 1, 0)
    nxt = lambda i: jnp.minimum(i + 1, nb - 1)
    kv = lambda base, f: pl.BlockSpec((WINDOW, HEAD_DIM), lambda i, h: (f(i), base + h))
    same = lambda i: i
    n_q = KV_HEADS * group
    slopes = 2.0 ** (-8.0 * (jnp.arange(n_q, dtype=F32) + 1.0) / n_q)
    scalars = jnp.concatenate([sink.astype(F32), slopes]).reshape(2 * KV_HEADS, group)
    return pl.pallas_call(
        functools.partial(_attn_kernel, seq=seq, group=group),
        grid=(nb, KV_HEADS),
        in_specs=[pl.BlockSpec((WINDOW, gw), lambda i, h: (i, qb + h)),
                  kv(kb, prev), kv(kb, same), kv(kb, nxt),
                  kv(vb, prev), kv(vb, same), kv(vb, nxt),
                  pl.BlockSpec(memory_space=pltpu.SMEM)],
        out_specs=pl.BlockSpec((WINDOW, gw), lambda i, h: (i, h)),
        out_shape=jax.ShapeDtypeStruct((seq, width), BF16),
        compiler_params=_cparams("parallel", "parallel"),
        name="window_attention",
    )(z, z, z, z, z, z, z, scalars)


def _retention_kernel(*refs, heads, reverse, final):
    if final:
        lg_ref, q_ref, k_ref, v_ref, prev_ref, g_ref, o_ref, s_ref = refs
    else:
        lg_ref, q_ref, k_ref, v_ref, o_ref, s_ref = refs
    hg = pl.program_id(0)
    n = pl.program_id(1)
    c = RET_CHUNK

    @pl.when(n == 0)
    def _():
        s_ref[...] = jnp.zeros_like(s_ref)

    ri = lax.broadcasted_iota(jnp.int32, (c, c), 0)
    ci = lax.broadcasted_iota(jnp.int32, (c, c), 1)
    rel = (ci - ri) if reverse else (ri - ci)
    relf = jnp.maximum(rel, 0).astype(F32)
    idx = lax.broadcasted_iota(jnp.int32, (c, 1), 0).astype(F32)
    scale = HEAD_DIM ** -0.5
    for hd in range(heads):
        lg = lg_ref[hg * heads + hd]
        sl = slice(hd * HEAD_DIM, (hd + 1) * HEAD_DIM)
        q, k, v = q_ref[:, sl], k_ref[:, sl], v_ref[:, sl]
        decay = jnp.where(rel >= 0, jnp.exp(relf * lg), 0.0)
        k_w = jnp.exp((idx if reverse else (c - 1.0 - idx)) * lg) * scale
        q_w = jnp.exp(((c - idx) if reverse else (idx + 1.0)) * lg)
        chunk_decay = jnp.exp(jnp.full((1, 1), float(c), F32) * lg)
        qk = lax.dot_general(q, k, (((1,), (1,)), ((), ())), preferred_element_type=F32)
        inner = _dot((qk * scale * decay).astype(v.dtype), v)
        s_prev = s_ref[hd]
        cross = _dot((q.astype(F32) * q_w).astype(BF16), s_prev.astype(BF16))
        kv = lax.dot_general((k.astype(F32) * k_w).astype(BF16), v, (((0,), (0,)), ((), ())),
                             preferred_element_type=F32)
        s_ref[hd] = s_prev * chunk_decay + kv
        o = inner + cross
        if final:
            o = o + prev_ref[:, sl]
            o = o * lax.rsqrt(jnp.mean(o * o, axis=-1, keepdims=True) + EPS)
            g = g_ref[:, sl].astype(F32)
            o = g * jax.nn.sigmoid(g) * o
        o_ref[:, sl] = o.astype(o_ref.dtype)


def _retention_pass(z, seq, width, offs, reverse, prev=None):
    nc = seq // RET_CHUNK
    gw = math.gcd(width, *offs)
    assert gw % HEAD_DIM == 0
    heads = gw // HEAD_DIM
    q_off, k_off, v_off, g_off = offs
    hidx = jnp.arange(width // HEAD_DIM, dtype=F32)
    if reverse:
        log_gammas = jnp.log(1.0 - 2.0 ** (-5.5 - hidx))
        order = lambda n: nc - 1 - n
    else:
        log_gammas = jnp.log(1.0 - 2.0 ** (-5.0 - hidx))
        order = lambda n: n
    final = prev is not None
    zspec = lambda off: pl.BlockSpec((RET_CHUNK, gw), lambda g, n: (order(n), off // gw + g))
    own = pl.BlockSpec((RET_CHUNK, gw), lambda g, n: (order(n), g))
    in_specs = [pl.BlockSpec(memory_space=pltpu.SMEM), zspec(q_off), zspec(k_off), zspec(v_off)]
    args = [log_gammas, z, z, z]
    if final:
        in_specs += [own, zspec(g_off)]
        args += [prev, z]
    return pl.pallas_call(
        functools.partial(_retention_kernel, heads=heads, reverse=reverse, final=final),
        grid=(width // gw, nc),
        in_specs=in_specs,
        out_specs=own,
        out_shape=jax.ShapeDtypeStruct((seq, width), BF16 if final else F32),
        scratch_shapes=[pltpu.VMEM((heads, HEAD_DIM, HEAD_DIM), F32)],
        compiler_params=_cparams("parallel", "arbitrary"),
        name="retention_fwd" if final else "retention_bwd",
    )(*args)


def _retention(z, seq, width, offs):
    o_bwd = _retention_pass(z, seq, width, offs, reverse=True)
    return _retention_pass(z, seq, width, offs, reverse=False, prev=o_bwd)


def _merge_kernel(ya_ref, yb_ref, yc_ref, wa_ref, wb_ref, wc_ref, ga_ref, gb_ref, gc_ref, o_ref):
    acc = None
    for y_ref, w_ref, g_ref in ((ya_ref, wa_ref, ga_ref), (yb_ref, wb_ref, gb_ref), (yc_ref, wc_ref, gc_ref)):
        t = jax.nn.sigmoid(g_ref[...].astype(F32)) * _dot(y_ref[...], w_ref[...])
        acc = t if acc is None else acc + t
    o_ref[...] = acc.astype(o_ref.dtype)


def _merge(ya, yb, yc, w_branch, z, gate_off, d_model):
    seq, width = ya.shape
    tm, tn = _tile(seq, 1024), _tile(d_model, 512)
    assert gate_off % tn == 0
    ys = pl.BlockSpec((tm, width), lambda i, j: (i, 0))
    ws = lambda b: pl.BlockSpec((None, width, tn), lambda i, j: (b, 0, j))
    gs = lambda b: pl.BlockSpec((tm, tn), lambda i, j: (i, (gate_off + b * d_model) // tn + j))
    return pl.pallas_call(
        _merge_kernel,
        grid=(seq // tm, d_model // tn),
        in_specs=[ys, ys, ys, ws(0), ws(1), ws(2), gs(0), gs(1), gs(2)],
        out_specs=pl.BlockSpec((tm, tn), lambda i, j: (i, j)),
        out_shape=jax.ShapeDtypeStruct((seq, d_model), BF16),
        compiler_params=_cparams("parallel", "parallel"),
        name="branch_merge",
    )(ya, yb, yc, w_branch, w_branch, w_branch, z, z, z)


def _ffn_in_kernel(h_ref, hp_ref, hn_ref, wg_ref, wu_ref, cw_ref, cb_ref, o_ref, *, n_tiles):
    i = pl.program_id(0)
    wg = wg_ref[...]
    g = _dot(h_ref[...], wg)
    prev_row = _dot(hp_ref[...], wg)[HALO - 1:HALO] * (i > 0).astype(F32)
    next_row = _dot(hn_ref[...], wg)[0:1] * (i < n_tiles - 1).astype(F32)
    down, up = _shift_rows(g, prev_row, next_row)
    cw = cw_ref[...]
    c = down * cw[0:1] + g * cw[1:2] + up * cw[2:3] + cb_ref[...]
    gelu = 0.5 * c * (1.0 + lax.erf(c * (2.0 ** -0.5)))
    o_ref[...] = (gelu * _dot(h_ref[...], wu_ref[...])).astype(o_ref.dtype)


def _ffn_in(h, w_in, conv_w, conv_b, ffp):
    seq, d = h.shape
    tm, tn = _tile(seq, 1024), _tile(ffp, 512)
    n_tiles, n_halo, per = seq // tm, seq // HALO, tm // HALO
    nj = ffp // tn
    return pl.pallas_call(
        functools.partial(_ffn_in_kernel, n_tiles=n_tiles),
        grid=(n_tiles, nj),
        in_specs=[
            pl.BlockSpec((tm, d), lambda i, j: (i, 0)),
            pl.BlockSpec((HALO, d), lambda i, j: (jnp.maximum(i * per - 1, 0), 0)),
            pl.BlockSpec((HALO, d), lambda i, j: (jnp.minimum((i + 1) * per, n_halo - 1), 0)),
            pl.BlockSpec((d, tn), lambda i, j: (0, j)),
            pl.BlockSpec((d, tn), lambda i, j: (0, nj + j)),
            pl.BlockSpec((3, tn), lambda i, j: (0, j)),
            pl.BlockSpec((1, tn), lambda i, j: (0, j)),
        ],
        out_specs=pl.BlockSpec((tm, tn), lambda i, j: (i, j)),
        out_shape=jax.ShapeDtypeStruct((seq, ffp), BF16),
        compiler_params=_cparams("parallel", "parallel"),
        name="ffn_in",
    )(h, h, h, w_in, w_in, conv_w, conv_b)


def _prepare_layer(l, w):
    ff = w["w_ffn_out"].shape[1]
    ffp = -(-ff // FF_PAD) * FF_PAD
    pad = ffp - ff
    w_ffn_in = w["w_ffn_in"][l]
    gate = jnp.pad(w_ffn_in[:, :ff], ((0, 0), (0, pad)))
    up = jnp.pad(w_ffn_in[:, ff:], ((0, 0), (0, pad)))
    p = {name: arr[l] for name, arr in w.items()}
    p.update(
        w_in=w["w_in"][l].astype(BF16),
        w_branch=w["w_branch"][l].astype(BF16),
        w_out=w["w_out"][l].astype(BF16),
        w_ffn_in=jnp.concatenate([gate, up], axis=1).astype(BF16),
        w_ffn_out=jnp.pad(w["w_ffn_out"][l], ((0, pad), (0, 0))).astype(BF16),
        ffn_conv_w=jnp.pad(w["ffn_conv_w"][l], ((0, 0), (0, pad))).astype(F32),
        ffn_conv_b=jnp.pad(w["ffn_conv_b"][l], ((0, pad),)).reshape(1, -1).astype(F32),
        ffp=ffp,
    )
    return p


def _layer(x, p, tables):
    seq, d = x.shape
    width = d // 4
    kvw = KV_HEADS * HEAD_DIM
    sizes = [3 * width, width, kvw, kvw, width, width, width, width, 3 * d]
    offs = [0]
    for s in sizes[:-1]:
        offs.append(offs[-1] + s)
    h = _rmsnorm(x, p["norm_mix"], BF16)
    z = _matmul(h, p["w_in"], tm=1024, tn=512, out_dtype=BF16, name="in_proj")
    y_a = _hyena_mixer(z, p, tables, seq, width)
    y_b = _attention(z, p["attn_sink"], seq, width, offs[1], offs[2], offs[3])
    y_c = _retention(z, seq, width, offs[4:8])
    merged = _merge(y_a, y_b, y_c, p["w_branch"], z, offs[8], d)
    x = _matmul(merged, p["w_out"], tm=1024, tn=512, out_dtype=F32, residual=x, name="out_proj")
    h = _rmsnorm(x, p["norm_ffn"], BF16)
    act = _ffn_in(h, p["w_ffn_in"], p["ffn_conv_w"], p["ffn_conv_b"], p["ffp"])
    return _matmul(act, p["w_ffn_out"], tm=1024, tn=512, tk=p["ffp"] // 2, out_dtype=F32, residual=x, name="ffn_out")


def _trunk(x, layers, norm_final):
    b, seq, d = x.shape
    assert b == 1
    tables = _fft_tables(seq)
    x = x.reshape(seq, d)
    for p in layers:
        x = _layer(x, p, tables)
    return _rmsnorm(x, norm_final, F32).reshape(b, seq, d)


def kernel(x_prompt, x_sample, norm_mix, w_in, hy_conv_w, hy_conv_b, hy_filt_w1, hy_filt_b1, hy_filt_w2, hy_filt_b2, hy_filt_w3, hy_filt_freq, hy_skip, attn_sink, w_branch, w_out, norm_ffn, w_ffn_in, ffn_conv_w, ffn_conv_b, w_ffn_out, norm_final):
    w = dict(norm_mix=norm_mix, w_in=w_in, hy_conv_w=hy_conv_w, hy_conv_b=hy_conv_b,
             hy_filt_w1=hy_filt_w1, hy_filt_b1=hy_filt_b1, hy_filt_w2=hy_filt_w2,
             hy_filt_b2=hy_filt_b2, hy_filt_w3=hy_filt_w3, hy_filt_freq=hy_filt_freq,
             hy_skip=hy_skip, attn_sink=attn_sink, w_branch=w_branch, w_out=w_out,
             norm_ffn=norm_ffn, w_ffn_in=w_ffn_in, ffn_conv_w=ffn_conv_w,
             ffn_conv_b=ffn_conv_b, w_ffn_out=w_ffn_out)
    layers = [_prepare_layer(l, w) for l in range(norm_mix.shape[0])]
    return (_trunk(x_prompt, layers, norm_final), _trunk(x_sample, layers, norm_final))
```

```python
import functools
import math

import jax
import jax.numpy as jnp
from jax import lax
from jax.experimental import pallas as pl
from jax.experimental.pallas import tpu as pltpu

F32 = jnp.float32
BF16 = jnp.bfloat16

EPS = 1e-6
NEG = -1e30
HEAD_DIM = 128
KV_HEADS = 2
WINDOW = 128
RET_CHUNK = 128
HY_ORDER = 2
HY_EMB = 33
HY_BANDS = (HY_EMB - 1) // 2
HY_MIN_DECAY = math.log(1e-2) / 1.5
HY_MAX_DECAY = math.log(1e-2) / 0.3
DFT_INNER = 128
FFT_CB = 16
FFT_PASSES = 1
FEAT_LANES = 128
HALO = 16
FF_PAD = 1024
VMEM_LIMIT = 56 * 1024 * 1024


def _cparams(*sem, vmem=VMEM_LIMIT):
    return pltpu.CompilerParams(dimension_semantics=sem, vmem_limit_bytes=vmem)


def _tile(n, pref):
    if n <= pref:
        return n
    t = (pref // 128) * 128
    while t >= 128:
        if n % t == 0:
            return t
        t -= 128
    return n


def _dot(a, b):
    return jnp.dot(a, b, preferred_element_type=F32)


def _split(x):
    hi = x.astype(BF16)
    lo = (x - hi.astype(F32)).astype(BF16)
    return hi, lo


def _dot3(a, b):
    a_hi, a_lo = _split(a)
    b_hi, b_lo = _split(b)
    return _dot(a_hi, b_hi) + (_dot(a_lo, b_hi) + _dot(a_hi, b_lo))


def _rmsnorm_kernel(x_ref, g_ref, o_ref):
    x = x_ref[...]
    ms = jnp.mean(x * x, axis=-1, keepdims=True)
    o_ref[...] = (x * lax.rsqrt(ms + EPS) * g_ref[...]).astype(o_ref.dtype)


def _rmsnorm(x, g, out_dtype):
    m, d = x.shape
    tm = _tile(m, 256)
    return pl.pallas_call(
        _rmsnorm_kernel,
        grid=(m // tm,),
        in_specs=[pl.BlockSpec((tm, d), lambda i: (i, 0)),
                  pl.BlockSpec((1, d), lambda i: (0, 0))],
        out_specs=pl.BlockSpec((tm, d), lambda i: (i, 0)),
        out_shape=jax.ShapeDtypeStruct((m, d), out_dtype),
        compiler_params=_cparams("parallel"),
        name="rmsnorm",
    )(x, g.reshape(1, d).astype(F32))


def _matmul_kernel(*refs, nk, has_res):
    if has_res:
        a_ref, b_ref, r_ref, o_ref = refs[:4]
    else:
        a_ref, b_ref, o_ref = refs[:3]
        r_ref = None
    p = _dot(a_ref[...], b_ref[...])

    def finish(acc):
        if has_res:
            acc = acc + r_ref[...]
        o_ref[...] = acc.astype(o_ref.dtype)

    if nk == 1:
        finish(p)
        return
    acc_ref = refs[-1]
    k = pl.program_id(2)

    @pl.when(k == 0)
    def _():
        acc_ref[...] = p

    @pl.when(k > 0)
    def _():
        acc_ref[...] += p

    @pl.when(k == nk - 1)
    def _():
        finish(acc_ref[...])


def _matmul(a, b, layer, *, tm, tn, tk=None, out_dtype, residual=None, name):
    m, kdim = a.shape
    n = b.shape[2]
    tm, tn = _tile(m, tm), _tile(n, tn)
    tk = kdim if tk is None else _tile(kdim, tk)
    nk = kdim // tk
    in_specs = [pl.BlockSpec((tm, tk), lambda i, j, k: (i, k)),
                pl.BlockSpec((None, tk, tn), lambda i, j, k: (layer, k, j))]
    args = [a, b]
    if residual is not None:
        in_specs.append(pl.BlockSpec((tm, tn), lambda i, j, k: (i, j)))
        args.append(residual)
    scratch = [pltpu.VMEM((tm, tn), F32)] if nk > 1 else []
    return pl.pallas_call(
        functools.partial(_matmul_kernel, nk=nk, has_res=residual is not None),
        grid=(m // tm, n // tn, nk),
        in_specs=in_specs,
        out_specs=pl.BlockSpec((tm, tn), lambda i, j, k: (i, j)),
        out_shape=jax.ShapeDtypeStruct((m, n), out_dtype),
        scratch_shapes=scratch,
        compiler_params=_cparams("parallel", "parallel", "arbitrary"),
        name=name,
    )(*args)


def _shift_rows(x, prev_row, next_row):
    tm = x.shape[0]
    row = lax.broadcasted_iota(jnp.int32, x.shape, 0)
    down = jnp.where(row == 0, prev_row, pltpu.roll(x, 1, axis=0))
    up = jnp.where(row == tm - 1, next_row, pltpu.roll(x, tm - 1, axis=0))
    return down, up


def _hyconv_kernel(x_ref, xp_ref, xn_ref, w_ref, b_ref, o_ref, *, n_tiles):
    i = pl.program_id(0)
    x = x_ref[...].astype(F32)
    prev_row = xp_ref[...].astype(F32)[HALO - 1:HALO] * (i > 0).astype(F32)
    next_row = xn_ref[...].astype(F32)[0:1] * (i < n_tiles - 1).astype(F32)
    down, up = _shift_rows(x, prev_row, next_row)
    w = w_ref[...]
    o_ref[...] = down * w[0:1] + x * w[1:2] + up * w[2:3] + b_ref[...]


def _hyena_conv(z, conv_w, conv_b, seq, width):
    tm = _tile(seq, 512)
    n_tiles = seq // tm
    n_halo = seq // HALO
    per = tm // HALO
    return pl.pallas_call(
        functools.partial(_hyconv_kernel, n_tiles=n_tiles),
        grid=(n_tiles, 3),
        in_specs=[
            pl.BlockSpec((tm, width), lambda i, p: (i, p)),
            pl.BlockSpec((HALO, width), lambda i, p: (jnp.maximum(i * per - 1, 0), p)),
            pl.BlockSpec((HALO, width), lambda i, p: (jnp.minimum((i + 1) * per, n_halo - 1), p)),
            pl.BlockSpec((3, width), lambda i, p: (0, p)),
            pl.BlockSpec((1, width), lambda i, p: (0, p)),
        ],
        out_specs=pl.BlockSpec((None, tm, width), lambda i, p: (p, i, 0)),
        out_shape=jax.ShapeDtypeStruct((3, seq, width), F32),
        compiler_params=_cparams("parallel", "parallel"),
        name="hyena_conv3",
    )(z, z, z, conv_w.astype(F32), conv_b.reshape(1, -1).astype(F32))


def _filter_kernel(bands_ref, w1_ref, b1_ref, w2_ref, b2_ref, freq_ref, w3_ref, delta_ref,
                   w3b_ref, k_ref, sum_ref, *, seq, tr, width):
    i = pl.program_id(0)
    n = 2 * seq
    r = i * tr + lax.broadcasted_iota(jnp.int32, (tr, FEAT_LANES), 0)
    pos = jnp.where(r < seq, r, n - r).astype(F32)
    t = pos / float(seq - 1)
    lane = lax.broadcasted_iota(jnp.int32, (tr, FEAT_LANES), 1)
    ang = (2.0 * math.pi / seq) * pos * bands_ref[...]
    quarter = jnp.where((lane >= HY_BANDS) & (lane < 2 * HY_BANDS), 0.5 * math.pi, 0.0)
    feats = jnp.where(lane < 2 * HY_BANDS, jnp.cos(ang + quarter),
                      jnp.where(lane == 2 * HY_BANDS, t, 0.0))
    freq = freq_ref[...]
    h = jnp.sin(freq * (_dot3(feats, w1_ref[...]) + b1_ref[...]))
    h = jnp.sin(freq * (_dot3(h, w2_ref[...]) + b2_ref[...]))
    h_hi, h_lo = _split(h)
    r1 = i * tr + lax.broadcasted_iota(jnp.int32, (tr, 1), 0)
    t1 = jnp.where(r1 < seq, r1, n - r1).astype(F32) / float(seq - 1)
    window = jnp.where(r1 == seq, 0.0, jnp.exp(-t1 * delta_ref[...]))
    parts = []
    for o, w3 in enumerate((w3_ref[...], w3b_ref[...])):
        w_hi, w_lo = _split(w3)
        kf = (_dot(h_hi, w_hi) + (_dot(h_lo, w_hi) + _dot(h_hi, w_lo))) * window
        k_ref[:, o * width:(o + 1) * width] = kf
        parts.append(jnp.sum(jnp.abs(kf), axis=0, keepdims=True))
    part = jnp.concatenate(parts, axis=1)

    @pl.when(i == 0)
    def _():
        sum_ref[...] = part

    @pl.when(i > 0)
    def _():
        sum_ref[...] += part


def _hyena_filters(seq, width, w1, b1, w2, b2, w3, freq):
    n = 2 * seq
    tr = _tile(seq, 512)
    nb = n // tr
    fw = w1.shape[1]
    bands = jnp.linspace(1e-4, HY_BANDS - 1, HY_BANDS, dtype=F32)
    bands = jnp.concatenate([bands, bands, jnp.zeros((FEAT_LANES - 2 * HY_BANDS,), F32)]).reshape(1, -1)
    w1 = w1.astype(F32)
    w1p = jnp.concatenate([w1[1:], w1[:1], jnp.zeros((FEAT_LANES - HY_EMB, fw), F32)], axis=0)
    deltas = jnp.abs(jnp.linspace(HY_MIN_DECAY, HY_MAX_DECAY, width, dtype=F32)).reshape(1, -1)
    row = lambda a: a.reshape(1, -1).astype(F32)
    const = lambda shape: pl.BlockSpec(shape, lambda i: (0, 0))
    w3_spec = lambda o: pl.BlockSpec((fw, width), lambda i: (0, 2 * o + (i >= nb // 2).astype(jnp.int32)))
    assert HY_ORDER == 2
    w3 = w3.astype(F32)
    return pl.pallas_call(
        functools.partial(_filter_kernel, seq=seq, tr=tr, width=width),
        grid=(nb,),
        in_specs=[
            const((1, FEAT_LANES)), const((FEAT_LANES, fw)), const((1, fw)), const((fw, fw)),
            const((1, fw)), const((1, fw)), w3_spec(0), const((1, width)), w3_spec(1),
        ],
        out_specs=[pl.BlockSpec((tr, HY_ORDER * width), lambda i: (i, 0)),
                   pl.BlockSpec((1, HY_ORDER * width), lambda i: (0, 0))],
        out_shape=[jax.ShapeDtypeStruct((n, HY_ORDER * width), F32),
                   jax.ShapeDtypeStruct((1, HY_ORDER * width), F32)],
        compiler_params=_cparams("arbitrary"),
        name="hyena_filters",
    )(bands, w1p, row(b1), w2.astype(F32), row(b2), row(freq), w3, deltas, w3)


def _fft_tables(seq):
    n = 2 * seq
    n2 = DFT_INNER
    n1 = n // n2
    r = n1 // 2
    k1 = jnp.arange(n1, dtype=jnp.int32)
    th = (2.0 * math.pi / n1) * ((k1[:, None] * k1[None, :]) % n1).astype(F32)
    c, s = jnp.cos(th), jnp.sin(th)
    outer = jnp.concatenate([c, -s], axis=0)
    j = jnp.arange(n2, dtype=jnp.int32)
    ph = (2.0 * math.pi / n) * ((k1[:, None] * j[None, :]) % n).astype(F32)
    th2 = (2.0 * math.pi / n2) * ((j[:, None] * j[None, :]) % n2).astype(F32)
    cr, ci = jnp.cos(th2), -jnp.sin(th2)
    inner = jnp.concatenate([jnp.concatenate([cr, ci], axis=1),
                             jnp.concatenate([-ci, cr], axis=1)], axis=0)
    inv_inner = jnp.concatenate([jnp.concatenate([cr, -ci], axis=1),
                                 jnp.concatenate([ci, cr], axis=1)], axis=0) / n2
    return dict(n1=n1, n2=n2, outer=_split(outer), outer_half=_split(outer[:, :r]),
                inv_outer_re=_split(c[:r] / n1), inv_outer_im=_split(-s[:r] / n1),
                tw_re=jnp.cos(ph), tw_im=-jnp.sin(ph),
                inner=_split(inner), inv_inner=_split(inv_inner))


def _mat_dot(m_hi, m_lo, x, passes):
    x_hi = x.astype(BF16)
    out = _dot(m_hi, x_hi)
    if passes >= 2:
        out = out + _dot(m_hi, (x - x_hi.astype(F32)).astype(BF16))
    if passes >= 3:
        out = out + _dot(m_lo, x_hi)
    return out


def _dot_mat(x, m_hi, m_lo, passes):
    x_hi = x.astype(BF16)
    out = _dot(x_hi, m_hi)
    if passes >= 2:
        out = out + _dot((x - x_hi.astype(F32)).astype(BF16), m_hi)
    if passes >= 3:
        out = out + _dot(x_hi, m_lo)
    return out


def _forward_to_scratch(x_ref, a_scr, f1h, f1l, tw_re, tw_im, n1, cb, passes):
    half = DFT_INNER
    for c in range(cb):
        p = _mat_dot(f1h, f1l, x_ref[c], passes)
        ar, ai = p[:n1], p[n1:]
        a_scr[c, :, :half] = ar * tw_re - ai * tw_im
        a_scr[c, :, half:] = ar * tw_im + ai * tw_re


def _filter_spectrum_kernel(k_ref, inv_ref, f1h_ref, f1l_ref, twr_ref, twi_ref, mh_ref, ml_ref,
                            kr_ref, ki_ref, a_scr, *, n1, cb):
    half = DFT_INNER
    _forward_to_scratch(k_ref, a_scr, f1h_ref[...], f1l_ref[...], twr_ref[...], twi_ref[...], n1, cb, FFT_PASSES)
    x = _dot_mat(a_scr[...].reshape(cb * n1, 2 * half), mh_ref[...], ml_ref[...], FFT_PASSES)
    x = x.reshape(cb, n1, 2 * half)
    inv = inv_ref[...]
    kr_ref[...] = x[:, :, :half] * inv
    ki_ref[...] = x[:, :, half:] * inv


def _const_spec(arr):
    return pl.BlockSpec(arr.shape, lambda j: (0,) * arr.ndim)


def _filter_spectrum(kf_t, inv_sum, tables):
    ch, n1, n2 = kf_t.shape
    cb = FFT_CB
    f1h, f1l = tables["outer"]
    mh, ml = tables["inner"]
    consts = [f1h, f1l, tables["tw_re"], tables["tw_im"], mh, ml]
    blk = pl.BlockSpec((cb, n1, n2), lambda j: (j, 0, 0))
    out = jax.ShapeDtypeStruct((ch, n1, n2), F32)
    return pl.pallas_call(
        functools.partial(_filter_spectrum_kernel, n1=n1, cb=cb),
        grid=(ch // cb,),
        in_specs=[blk, pl.BlockSpec((cb, 1, n2), lambda j: (j, 0, 0))] + [_const_spec(a) for a in consts],
        out_specs=[blk, blk],
        out_shape=[out, out],
        scratch_shapes=[pltpu.VMEM((cb, n1, 2 * n2), F32)],
        compiler_params=_cparams("parallel"),
        name="filter_spectrum",
    )(kf_t, inv_sum, *consts)


def _fftconv_kernel(u_ref, gate_ref, kr_ref, ki_ref, skip_ref, f1h_ref, f1l_ref, twr_ref, twi_ref,
                    mh_ref, ml_ref, nh_ref, nl_ref, grh_ref, grl_ref, gih_ref, gil_ref,
                    o_ref, a_scr, *, n1, cb, passes):
    half = DFT_INNER
    tw_re, tw_im = twr_ref[...], twi_ref[...]
    _forward_to_scratch(u_ref, a_scr, f1h_ref[...], f1l_ref[...], tw_re, tw_im, n1, cb, passes)
    x = _dot_mat(a_scr[...].reshape(cb * n1, 2 * half), mh_ref[...], ml_ref[...], passes)
    xr, xi = x[:, :half], x[:, half:]
    kr = kr_ref[...].reshape(cb * n1, half)
    ki = ki_ref[...].reshape(cb * n1, half)
    y = jnp.concatenate([xr * kr - xi * ki, xr * ki + xi * kr], axis=1)
    b = _dot_mat(y, nh_ref[...], nl_ref[...], passes)
    a_scr[...] = b.reshape(cb, n1, 2 * half)
    grh, grl, gih, gil = grh_ref[...], grl_ref[...], gih_ref[...], gil_ref[...]
    for c in range(cb):
        br, bi = a_scr[c, :, :half], a_scr[c, :, half:]
        conv = (_mat_dot(grh, grl, br * tw_re + bi * tw_im, passes)
                + _mat_dot(gih, gil, bi * tw_re - br * tw_im, passes))
        o_ref[c] = (gate_ref[c] * (conv + u_ref[c] * skip_ref[c])).astype(o_ref.dtype)


def _fftconv(u_t, u_part, gate_t, gate_part, kr, ki, order, skip_t, tables, width, out_dtype):
    n1, n2 = tables["n1"], tables["n2"]
    rows = n1 // 2
    cb = FFT_CB
    nb = width // cb
    consts = [*tables["outer_half"], tables["tw_re"], tables["tw_im"], *tables["inner"],
              *tables["inv_inner"], *tables["inv_outer_re"], *tables["inv_outer_im"]]
    data = lambda part: pl.BlockSpec((cb, rows, n2), lambda j: (part * nb + j, 0, 0))
    spec = pl.BlockSpec((cb, n1, n2), lambda j: (order * nb + j, 0, 0))
    return pl.pallas_call(
        functools.partial(_fftconv_kernel, n1=n1, cb=cb, passes=FFT_PASSES),
        grid=(nb,),
        in_specs=[data(u_part), data(gate_part), spec, spec,
                  pl.BlockSpec((cb, 1, n2), lambda j: (j, 0, 0))] + [_const_spec(a) for a in consts],
        out_specs=pl.BlockSpec((cb, rows, n2), lambda j: (j, 0, 0)),
        out_shape=jax.ShapeDtypeStruct((width, rows, n2), out_dtype),
        scratch_shapes=[pltpu.VMEM((cb, n1, 2 * n2), F32)],
        compiler_params=_cparams("parallel"),
        name="hyena_fftconv",
    )(u_t, gate_t, kr, ki, skip_t, *consts)


def _lane_rows(v):
    return jnp.broadcast_to(v.astype(F32).reshape(-1, 1, 1), (v.shape[0], 1, DFT_INNER))


def _hyena_mixer(z, p, tables, seq, width):
    n1, n2 = tables["n1"], tables["n2"]
    rows = n1 // 2
    zc = _hyena_conv(z, p["hy_conv_w"], p["hy_conv_b"], seq, width)
    zc_t = jnp.transpose(zc, (0, 2, 1)).reshape(3 * width, rows, n2)
    kf, colsum = _hyena_filters(seq, width, p["hy_filt_w1"], p["hy_filt_b1"], p["hy_filt_w2"],
                                p["hy_filt_b2"], p["hy_filt_w3"], p["hy_filt_freq"])
    kf_t = kf.T.reshape(HY_ORDER * width, n1, n2)
    kr, ki = _filter_spectrum(kf_t, _lane_rows(1.0 / colsum.reshape(-1)), tables)
    skip = p["hy_skip"]
    y1 = _fftconv(zc_t, 0, zc_t, 1, kr, ki, 0, _lane_rows(skip[0]), tables, width, F32)
    y = _fftconv(y1, 0, zc_t, 2, kr, ki, 1, _lane_rows(skip[1]), tables, width, BF16)
    return y.reshape(width, seq).T


def _attn_kernel(q_ref, kp_ref, kc_ref, kn_ref, vp_ref, vc_ref, vn_ref, sink_ref, o_ref, *, seq, group):
    i = pl.program_id(0)
    h = pl.program_id(1)
    blk = WINDOW
    q = q_ref[...]
    q4 = jnp.concatenate([q[:, g * HEAD_DIM:(g + 1) * HEAD_DIM] for g in range(group)], axis=0)
    k = jnp.concatenate([kp_ref[...], kc_ref[...], kn_ref[...]], axis=0)
    v = jnp.concatenate([vp_ref[...], vc_ref[...], vn_ref[...]], axis=0)
    s = lax.dot_general(q4, k, (((1,), (1,)), ((), ())), preferred_element_type=F32)
    s = s * (HEAD_DIM ** -0.5)
    q_pos = i * blk + lax.broadcasted_iota(jnp.int32, (blk, 3 * blk), 0)
    k_pos = i * blk - blk + lax.broadcasted_iota(jnp.int32, (blk, 3 * blk), 1)
    dist = jnp.abs(q_pos - k_pos)
    valid = (dist <= WINDOW) & (k_pos >= 0) & (k_pos < seq)
    distf = dist.astype(F32)
    outs = []
    for g in range(group):
        sg = s[g * blk:(g + 1) * blk] - sink_ref[KV_HEADS + h, g] * distf
        sg = jnp.where(valid, sg, NEG)
        sink = sink_ref[h, g]
        m = jnp.maximum(jnp.max(sg, axis=-1, keepdims=True), sink)
        e = jnp.exp(sg - m)
        denom = jnp.sum(e, axis=-1, keepdims=True) + jnp.exp(sink - m)
        pg = (e / denom).astype(v.dtype)
        outs.append(_dot(pg, v))
    o_ref[...] = jnp.concatenate(outs, axis=1).astype(o_ref.dtype)


def _attention(z, sink, seq, width, q_off, k_off, v_off):
    group = width // HEAD_DIM // KV_HEADS
    gw = group * HEAD_DIM
    nb = seq // WINDOW
    qb, kb, vb = q_off // gw, k_off // HEAD_DIM, v_off // HEAD_DIM
    assert q_off % gw == 0 and k_off % HEAD_DIM == 0 and v_off % HEAD_DIM == 0
    prev = lambda i: jnp.maximum(i - 1, 0)
    nxt = lambda i: jnp.minimum(i + 1, nb - 1)
    kv = lambda base, f: pl.BlockSpec((WINDOW, HEAD_DIM), lambda i, h: (f(i), base + h))
    same = lambda i: i
    n_q = KV_HEADS * group
    slopes = 2.0 ** (-8.0 * (jnp.arange(n_q, dtype=F32) + 1.0) / n_q)
    scalars = jnp.concatenate([sink.astype(F32), slopes]).reshape(2 * KV_HEADS, group)
    return pl.pallas_call(
        functools.partial(_attn_kernel, seq=seq, group=group),
        grid=(nb, KV_HEADS),
        in_specs=[pl.BlockSpec((WINDOW, gw), lambda i, h: (i, qb + h)),
                  kv(kb, prev), kv(kb, same), kv(kb, nxt),
                  kv(vb, prev), kv(vb, same), kv(vb, nxt),
                  pl.BlockSpec(memory_space=pltpu.SMEM)],
        out_specs=pl.BlockSpec((WINDOW, gw), lambda i, h: (i, h)),
        out_shape=jax.ShapeDtypeStruct((seq, width), BF16),
        compiler_params=_cparams("parallel", "parallel"),
        name="window_attention",
    )(z, z, z, z, z, z, z, scalars)


def _retention_kernel(*refs, heads, reverse, final):
    if final:
        lg_ref, q_ref, k_ref, v_ref, prev_ref, g_ref, o_ref, s_ref = refs
    else:
        lg_ref, q_ref, k_ref, v_ref, o_ref, s_ref = refs
    hg = pl.program_id(0)
    n = pl.program_id(1)
    c = RET_CHUNK

    @pl.when(n == 0)
    def _():
        s_ref[...] = jnp.zeros_like(s_ref)

    ri = lax.broadcasted_iota(jnp.int32, (c, c), 0)
    ci = lax.broadcasted_iota(jnp.int32, (c, c), 1)
    rel = (ci - ri) if reverse else (ri - ci)
    relf = jnp.maximum(rel, 0).astype(F32)
    idx = lax.broadcasted_iota(jnp.int32, (c, 1), 0).astype(F32)
    scale = HEAD_DIM ** -0.5
    for hd in range(heads):
        lg = lg_ref[hg * heads + hd]
        sl = slice(hd * HEAD_DIM, (hd + 1) * HEAD_DIM)
        q, k, v = q_ref[:, sl], k_ref[:, sl], v_ref[:, sl]
        decay = jnp.where(rel >= 0, jnp.exp(relf * lg), 0.0)
        k_w = jnp.exp((idx if reverse else (c - 1.0 - idx)) * lg) * scale
        q_w = jnp.exp(((c - idx) if reverse else (idx + 1.0)) * lg)
        chunk_decay = jnp.exp(jnp.full((1, 1), float(c), F32) * lg)
        qk = lax.dot_general(q, k, (((1,), (1,)), ((), ())), preferred_element_type=F32)
        inner = _dot((qk * scale * decay).astype(v.dtype), v)
        s_prev = s_ref[hd]
        cross = _dot((q.astype(F32) * q_w).astype(BF16), s_prev.astype(BF16))
        kv = lax.dot_general((k.astype(F32) * k_w).astype(BF16), v, (((0,), (0,)), ((), ())),
                             preferred_element_type=F32)
        s_ref[hd] = s_prev * chunk_decay + kv
        o = inner + cross
        if final:
            o = o + prev_ref[:, sl]
            o = o * lax.rsqrt(jnp.mean(o * o, axis=-1, keepdims=True) + EPS)
            g = g_ref[:, sl].astype(F32)
            o = g * jax.nn.sigmoid(g) * o
        o_ref[:, sl] = o.astype(o_ref.dtype)


def _retention_pass(z, seq, width, offs, reverse, prev=None):
    nc = seq // RET_CHUNK
    gw = math.gcd(width, *offs)
    assert gw % HEAD_DIM == 0
    heads = gw // HEAD_DIM
    q_off, k_off, v_off, g_off = offs
    hidx = jnp.arange(width // HEAD_DIM, dtype=F32)
    if reverse:
        log_gammas = jnp.log(1.0 - 2.0 ** (-5.5 - hidx))
        order = lambda n: nc - 1 - n
    else:
        log_gammas = jnp.log(1.0 - 2.0 ** (-5.0 - hidx))
        order = lambda n: n
    final = prev is not None
    zspec = lambda off: pl.BlockSpec((RET_CHUNK, gw), lambda g, n: (order(n), off // gw + g))
    own = pl.BlockSpec((RET_CHUNK, gw), lambda g, n: (order(n), g))
    in_specs = [pl.BlockSpec(memory_space=pltpu.SMEM), zspec(q_off), zspec(k_off), zspec(v_off)]
    args = [log_gammas, z, z, z]
    if final:
        in_specs += [own, zspec(g_off)]
        args += [prev, z]
    return pl.pallas_call(
        functools.partial(_retention_kernel, heads=heads, reverse=reverse, final=final),
        grid=(width // gw, nc),
        in_specs=in_specs,
        out_specs=own,
        out_shape=jax.ShapeDtypeStruct((seq, width), BF16 if final else F32),
        scratch_shapes=[pltpu.VMEM((heads, HEAD_DIM, HEAD_DIM), F32)],
        compiler_params=_cparams("parallel", "arbitrary"),
        name="retention_fwd" if final else "retention_bwd",
    )(*args)


def _retention(z, seq, width, offs):
    o_bwd = _retention_pass(z, seq, width, offs, reverse=True)
    return _retention_pass(z, seq, width, offs, reverse=False, prev=o_bwd)


def _merge_kernel(ya_ref, yb_ref, yc_ref, wa_ref, wb_ref, wc_ref, ga_ref, gb_ref, gc_ref, o_ref):
    acc = None
    for y_ref, w_ref, g_ref in ((ya_ref, wa_ref, ga_ref), (yb_ref, wb_ref, gb_ref), (yc_ref, wc_ref, gc_ref)):
        t = jax.nn.sigmoid(g_ref[...].astype(F32)) * _dot(y_ref[...], w_ref[...])
        acc = t if acc is None else acc + t
    o_ref[...] = acc.astype(o_ref.dtype)


def _merge(ya, yb, yc, w_branch, layer, z, gate_off, d_model):
    seq, width = ya.shape
    tm, tn = _tile(seq, 1024), _tile(d_model, 512)
    assert gate_off % tn == 0
    ys = pl.BlockSpec((tm, width), lambda i, j: (i, 0))
    ws = lambda b: pl.BlockSpec((None, None, width, tn), lambda i, j: (layer, b, 0, j))
    gs = lambda b: pl.BlockSpec((tm, tn), lambda i, j: (i, (gate_off + b * d_model) // tn + j))
    return pl.pallas_call(
        _merge_kernel,
        grid=(seq // tm, d_model // tn),
        in_specs=[ys, ys, ys, ws(0), ws(1), ws(2), gs(0), gs(1), gs(2)],
        out_specs=pl.BlockSpec((tm, tn), lambda i, j: (i, j)),
        out_shape=jax.ShapeDtypeStruct((seq, d_model), BF16),
        compiler_params=_cparams("parallel", "parallel"),
        name="branch_merge",
    )(ya, yb, yc, w_branch, w_branch, w_branch, z, z, z)


def _ffn_in_kernel(h_ref, hp_ref, hn_ref, wg_ref, wu_ref, cw_ref, cb_ref, o_ref, *, n_tiles, n_split):
    i = pl.program_id(0)
    first = (i > 0).astype(F32)
    last = (i < n_tiles - 1).astype(F32)
    cols = o_ref.shape[1] // n_split
    for part in range(n_split):
        cs = slice(part * cols, (part + 1) * cols)
        wg = wg_ref[:, cs]
        g = _dot(h_ref[...], wg)
        prev_row = _dot(hp_ref[...], wg)[HALO - 1:HALO] * first
        next_row = _dot(hn_ref[...], wg)[0:1] * last
        down, up = _shift_rows(g, prev_row, next_row)
        cw = cw_ref[:, cs]
        c = down * cw[0:1] + g * cw[1:2] + up * cw[2:3] + cb_ref[:, cs]
        gelu = 0.5 * c * (1.0 + lax.erf(c * (2.0 ** -0.5)))
        o_ref[:, cs] = (gelu * _dot(h_ref[...], wu_ref[:, cs])).astype(o_ref.dtype)


def _ffn_in(h, w_in, conv_w, conv_b, layer, ffp):
    seq, d = h.shape
    tm, tn = _tile(seq, 1024), _tile(ffp, 512)
    n_tiles, n_halo, per = seq // tm, seq // HALO, tm // HALO
    nj = ffp // tn
    n_split = 1
    return pl.pallas_call(
        functools.partial(_ffn_in_kernel, n_tiles=n_tiles, n_split=n_split),
        grid=(n_tiles, nj),
        in_specs=[
            pl.BlockSpec((tm, d), lambda i, j: (i, 0)),
            pl.BlockSpec((HALO, d), lambda i, j: (jnp.maximum(i * per - 1, 0), 0)),
            pl.BlockSpec((HALO, d), lambda i, j: (jnp.minimum((i + 1) * per, n_halo - 1), 0)),
            pl.BlockSpec((None, d, tn), lambda i, j: (layer, 0, j)),
            pl.BlockSpec((None, d, tn), lambda i, j: (layer, 0, nj + j)),
            pl.BlockSpec((None, 3, tn), lambda i, j: (layer, 0, j)),
            pl.BlockSpec((None, 1, tn), lambda i, j: (layer, 0, j)),
        ],
        out_specs=pl.BlockSpec((tm, tn), lambda i, j: (i, j)),
        out_shape=jax.ShapeDtypeStruct((seq, ffp), BF16),
        compiler_params=_cparams("parallel", "parallel"),
        name="ffn_in",
    )(h, h, h, w_in, w_in, conv_w, conv_b)


MATMUL_WEIGHTS = ("w_in", "w_branch", "w_out", "w_ffn_in", "w_ffn_out", "ffn_conv_w", "ffn_conv_b")


def _prepare_weights(w):
    ff = w["w_ffn_out"].shape[1]
    ffp = -(-ff // FF_PAD) * FF_PAD
    pad = ffp - ff
    last = lambda a: jnp.pad(a, ((0, 0),) * (a.ndim - 1) + ((0, pad),))
    w_ffn_in = w["w_ffn_in"]
    return dict(
        w_in=w["w_in"].astype(BF16),
        w_branch=w["w_branch"].astype(BF16),
        w_out=w["w_out"].astype(BF16),
        w_ffn_in=jnp.concatenate([last(w_ffn_in[..., :ff].astype(BF16)),
                                  last(w_ffn_in[..., ff:].astype(BF16))], axis=-1),
        w_ffn_out=jnp.pad(w["w_ffn_out"].astype(BF16), ((0, 0), (0, pad), (0, 0))),
        ffn_conv_w=last(w["ffn_conv_w"].astype(F32)),
        ffn_conv_b=last(w["ffn_conv_b"].astype(F32))[:, None, :],
        ffp=ffp,
    )


def _layer(x, l, p, tables):
    seq, d = x.shape
    width = d // 4
    kvw = KV_HEADS * HEAD_DIM
    sizes = [3 * width, width, kvw, kvw, width, width, width, width, 3 * d]
    offs = [0]
    for s in sizes[:-1]:
        offs.append(offs[-1] + s)
    small = {name: arr[l] for name, arr in p.items() if name not in MATMUL_WEIGHTS and name != "ffp"}
    h = _rmsnorm(x, small["norm_mix"], BF16)
    z = _matmul(h, p["w_in"], l, tm=1024, tn=512, out_dtype=BF16, name="in_proj")
    y_a = _hyena_mixer(z, small, tables, seq, width)
    y_b = _attention(z, small["attn_sink"], seq, width, offs[1], offs[2], offs[3])
    y_c = _retention(z, seq, width, offs[4:8])
    merged = _merge(y_a, y_b, y_c, p["w_branch"], l, z, offs[8], d)
    x = _matmul(merged, p["w_out"], l, tm=1024, tn=512, out_dtype=F32, residual=x, name="out_proj")
    h = _rmsnorm(x, small["norm_ffn"], BF16)
    act = _ffn_in(h, p["w_ffn_in"], p["ffn_conv_w"], p["ffn_conv_b"], l, p["ffp"])
    return _matmul(act, p["w_ffn_out"], l, tm=1024, tn=512, tk=p["ffp"] // 2, out_dtype=F32, residual=x,
                   name="ffn_out")


def _trunk(x, p, norm_final):
    b, seq, d = x.shape
    assert b == 1
    tables = _fft_tables(seq)
    x = x.reshape(seq, d)
    for l in range(p["w_in"].shape[0]):
        x = _layer(x, l, p, tables)
    return _rmsnorm(x, norm_final, F32).reshape(b, seq, d)


def kernel(x_prompt, x_sample, norm_mix, w_in, hy_conv_w, hy_conv_b, hy_filt_w1, hy_filt_b1, hy_filt_w2, hy_filt_b2, hy_filt_w3, hy_filt_freq, hy_skip, attn_sink, w_branch, w_out, norm_ffn, w_ffn_in, ffn_conv_w, ffn_conv_b, w_ffn_out, norm_final):
    w = dict(norm_mix=norm_mix, w_in=w_in, hy_conv_w=hy_conv_w, hy_conv_b=hy_conv_b,
             hy_filt_w1=hy_filt_w1, hy_filt_b1=hy_filt_b1, hy_filt_w2=hy_filt_w2,
             hy_filt_b2=hy_filt_b2, hy_filt_w3=hy_filt_w3, hy_filt_freq=hy_filt_freq,
             hy_skip=hy_skip, attn_sink=attn_sink, w_branch=w_branch, w_out=w_out,
             norm_ffn=norm_ffn, w_ffn_in=w_ffn_in, ffn_conv_w=ffn_conv_w,
             ffn_conv_b=ffn_conv_b, w_ffn_out=w_ffn_out)
    p = dict(w)
    p.update(_prepare_weights(w))
    return (_trunk(x_prompt, p, norm_final), _trunk(x_sample, p, norm_final))
```

```python
import functools
import math

import jax
import jax.numpy as jnp
from jax import lax
from jax.experimental import pallas as pl
from jax.experimental.pallas import tpu as pltpu

F32 = jnp.float32
BF16 = jnp.bfloat16

EPS = 1e-6
NEG = -1e30
HEAD_DIM = 128
KV_HEADS = 2
WINDOW = 128
ATTN_BLOCKS = 4
RET_CHUNK = 128
RET_CHUNKS_PER_STEP = 2
HY_ORDER = 2
HY_EMB = 33
HY_BANDS = (HY_EMB - 1) // 2
HY_MIN_DECAY = math.log(1e-2) / 1.5
HY_MAX_DECAY = math.log(1e-2) / 0.3
DFT_INNER = 128
FFT_CB = 16
FFT_PASSES = 1
FEAT_LANES = 128
HALO = 16
FF_PAD = 1024
VMEM_LIMIT = 56 * 1024 * 1024


def _cparams(*sem, vmem=VMEM_LIMIT):
    return pltpu.CompilerParams(dimension_semantics=sem, vmem_limit_bytes=vmem)


def _tile(n, pref):
    if n <= pref:
        return n
    t = (pref // 128) * 128
    while t >= 128:
        if n % t == 0:
            return t
        t -= 128
    return n


def _dot(a, b):
    return jnp.dot(a, b, preferred_element_type=F32)


def _sigmoid(x):
    return 0.5 * jnp.tanh(0.5 * x) + 0.5


def _split(x):
    hi = x.astype(BF16)
    lo = (x - hi.astype(F32)).astype(BF16)
    return hi, lo


def _dot3(a, b):
    a_hi, a_lo = _split(a)
    b_hi, b_lo = _split(b)
    return _dot(a_hi, b_hi) + (_dot(a_lo, b_hi) + _dot(a_hi, b_lo))


def _rmsnorm_kernel(x_ref, g_ref, o_ref):
    x = x_ref[...]
    ms = jnp.mean(x * x, axis=-1, keepdims=True)
    o_ref[...] = (x * lax.rsqrt(ms + EPS) * g_ref[...]).astype(o_ref.dtype)


def _rmsnorm(x, g, out_dtype):
    m, d = x.shape
    tm = _tile(m, 256)
    return pl.pallas_call(
        _rmsnorm_kernel,
        grid=(m // tm,),
        in_specs=[pl.BlockSpec((tm, d), lambda i: (i, 0)),
                  pl.BlockSpec((1, d), lambda i: (0, 0))],
        out_specs=pl.BlockSpec((tm, d), lambda i: (i, 0)),
        out_shape=jax.ShapeDtypeStruct((m, d), out_dtype),
        compiler_params=_cparams("parallel"),
        name="rmsnorm",
    )(x, g.reshape(1, d).astype(F32))


def _matmul_kernel(*refs, nk, has_res):
    if has_res:
        a_ref, b_ref, r_ref, o_ref = refs[:4]
    else:
        a_ref, b_ref, o_ref = refs[:3]
        r_ref = None
    p = _dot(a_ref[...], b_ref[...])

    def finish(acc):
        if has_res:
            acc = acc + r_ref[...]
        o_ref[...] = acc.astype(o_ref.dtype)

    if nk == 1:
        finish(p)
        return
    acc_ref = refs[-1]
    k = pl.program_id(2)

    @pl.when(k == 0)
    def _():
        acc_ref[...] = p

    @pl.when(k > 0)
    def _():
        acc_ref[...] += p

    @pl.when(k == nk - 1)
    def _():
        finish(acc_ref[...])


def _matmul(a, b, layer, *, tm, tn, tk=None, out_dtype, residual=None, name):
    m, kdim = a.shape
    n = b.shape[2]
    tm, tn = _tile(m, tm), _tile(n, tn)
    tk = kdim if tk is None else _tile(kdim, tk)
    nk = kdim // tk
    in_specs = [pl.BlockSpec((tm, tk), lambda i, j, k: (i, k)),
                pl.BlockSpec((None, tk, tn), lambda i, j, k: (layer, k, j))]
    args = [a, b]
    if residual is not None:
        in_specs.append(pl.BlockSpec((tm, tn), lambda i, j, k: (i, j)))
        args.append(residual)
    scratch = [pltpu.VMEM((tm, tn), F32)] if nk > 1 else []
    return pl.pallas_call(
        functools.partial(_matmul_kernel, nk=nk, has_res=residual is not None),
        grid=(m // tm, n // tn, nk),
        in_specs=in_specs,
        out_specs=pl.BlockSpec((tm, tn), lambda i, j, k: (i, j)),
        out_shape=jax.ShapeDtypeStruct((m, n), out_dtype),
        scratch_shapes=scratch,
        compiler_params=_cparams("parallel", "parallel", "arbitrary"),
        name=name,
    )(*args)


def _shift_rows(x, prev_row, next_row):
    tm = x.shape[0]
    row = lax.broadcasted_iota(jnp.int32, x.shape, 0)
    down = jnp.where(row == 0, prev_row, pltpu.roll(x, 1, axis=0))
    up = jnp.where(row == tm - 1, next_row, pltpu.roll(x, tm - 1, axis=0))
    return down, up


def _hyconv_kernel(x_ref, xp_ref, xn_ref, w_ref, b_ref, o_ref, *, n_tiles):
    i = pl.program_id(0)
    x = x_ref[...].astype(F32)
    prev_row = xp_ref[...].astype(F32)[HALO - 1:HALO] * (i > 0).astype(F32)
    next_row = xn_ref[...].astype(F32)[0:1] * (i < n_tiles - 1).astype(F32)
    down, up = _shift_rows(x, prev_row, next_row)
    w = w_ref[...]
    o_ref[...] = (down * w[0:1] + x * w[1:2] + up * w[2:3] + b_ref[...]).astype(o_ref.dtype)


def _hyena_conv(z, conv_w, conv_b, seq, width):
    tm = _tile(seq, 512)
    n_tiles = seq // tm
    n_halo = seq // HALO
    per = tm // HALO
    return pl.pallas_call(
        functools.partial(_hyconv_kernel, n_tiles=n_tiles),
        grid=(n_tiles, 3),
        in_specs=[
            pl.BlockSpec((tm, width), lambda i, p: (i, p)),
            pl.BlockSpec((HALO, width), lambda i, p: (jnp.maximum(i * per - 1, 0), p)),
            pl.BlockSpec((HALO, width), lambda i, p: (jnp.minimum((i + 1) * per, n_halo - 1), p)),
            pl.BlockSpec((3, width), lambda i, p: (0, p)),
            pl.BlockSpec((1, width), lambda i, p: (0, p)),
        ],
        out_specs=pl.BlockSpec((None, tm, width), lambda i, p: (p, i, 0)),
        out_shape=jax.ShapeDtypeStruct((3, seq, width), BF16),
        compiler_params=_cparams("parallel", "parallel"),
        name="hyena_conv3",
    )(z, z, z, conv_w.astype(F32), conv_b.reshape(1, -1).astype(F32))


def _filter_kernel(bands_ref, w1_ref, b1_ref, w2_ref, b2_ref, freq_ref, w3_ref, delta_ref,
                   w3b_ref, k_ref, sum_ref, *, seq, tr, width):
    i = pl.program_id(0)
    n = 2 * seq
    r = i * tr + lax.broadcasted_iota(jnp.int32, (tr, FEAT_LANES), 0)
    pos = jnp.where(r < seq, r, n - r).astype(F32)
    t = pos / float(seq - 1)
    lane = lax.broadcasted_iota(jnp.int32, (tr, FEAT_LANES), 1)
    ang = (2.0 * math.pi / seq) * pos * bands_ref[...]
    quarter = jnp.where((lane >= HY_BANDS) & (lane < 2 * HY_BANDS), 0.5 * math.pi, 0.0)
    feats = jnp.where(lane < 2 * HY_BANDS, jnp.cos(ang + quarter),
                      jnp.where(lane == 2 * HY_BANDS, t, 0.0))
    freq = freq_ref[...]
    h = jnp.sin(freq * (_dot3(feats, w1_ref[...]) + b1_ref[...]))
    h = jnp.sin(freq * (_dot3(h, w2_ref[...]) + b2_ref[...]))
    h_hi, h_lo = _split(h)
    r1 = i * tr + lax.broadcasted_iota(jnp.int32, (tr, 1), 0)
    t1 = jnp.where(r1 < seq, r1, n - r1).astype(F32) / float(seq - 1)
    window = jnp.where(r1 == seq, 0.0, jnp.exp(-t1 * delta_ref[...]))
    parts = []
    for o, w3 in enumerate((w3_ref[...], w3b_ref[...])):
        w_hi, w_lo = _split(w3)
        kf = (_dot(h_hi, w_hi) + (_dot(h_lo, w_hi) + _dot(h_hi, w_lo))) * window
        k_ref[:, o * width:(o + 1) * width] = kf
        parts.append(jnp.sum(jnp.abs(kf), axis=0, keepdims=True))
    part = jnp.concatenate(parts, axis=1)

    @pl.when(i == 0)
    def _():
        sum_ref[...] = part

    @pl.when(i > 0)
    def _():
        sum_ref[...] += part


def _hyena_filters(seq, width, w1, b1, w2, b2, w3, freq):
    n = 2 * seq
    tr = _tile(seq, 512)
    nb = n // tr
    fw = w1.shape[1]
    bands = jnp.linspace(1e-4, HY_BANDS - 1, HY_BANDS, dtype=F32)
    bands = jnp.concatenate([bands, bands, jnp.zeros((FEAT_LANES - 2 * HY_BANDS,), F32)]).reshape(1, -1)
    w1 = w1.astype(F32)
    w1p = jnp.concatenate([w1[1:], w1[:1], jnp.zeros((FEAT_LANES - HY_EMB, fw), F32)], axis=0)
    deltas = jnp.abs(jnp.linspace(HY_MIN_DECAY, HY_MAX_DECAY, width, dtype=F32)).reshape(1, -1)
    row = lambda a: a.reshape(1, -1).astype(F32)
    const = lambda shape: pl.BlockSpec(shape, lambda i: (0, 0))
    w3_spec = lambda o: pl.BlockSpec((fw, width), lambda i: (0, 2 * o + (i >= nb // 2).astype(jnp.int32)))
    assert HY_ORDER == 2
    w3 = w3.astype(F32)
    return pl.pallas_call(
        functools.partial(_filter_kernel, seq=seq, tr=tr, width=width),
        grid=(nb,),
        in_specs=[
            const((1, FEAT_LANES)), const((FEAT_LANES, fw)), const((1, fw)), const((fw, fw)),
            const((1, fw)), const((1, fw)), w3_spec(0), const((1, width)), w3_spec(1),
        ],
        out_specs=[pl.BlockSpec((tr, HY_ORDER * width), lambda i: (i, 0)),
                   pl.BlockSpec((1, HY_ORDER * width), lambda i: (0, 0))],
        out_shape=[jax.ShapeDtypeStruct((n, HY_ORDER * width), F32),
                   jax.ShapeDtypeStruct((1, HY_ORDER * width), F32)],
        compiler_params=_cparams("arbitrary"),
        name="hyena_filters",
    )(bands, w1p, row(b1), w2.astype(F32), row(b2), row(freq), w3, deltas, w3)


def _fft_tables(seq):
    n = 2 * seq
    n2 = DFT_INNER
    n1 = n // n2
    r = n1 // 2
    k1 = jnp.arange(n1, dtype=jnp.int32)
    th = (2.0 * math.pi / n1) * ((k1[:, None] * k1[None, :]) % n1).astype(F32)
    c, s = jnp.cos(th), jnp.sin(th)
    outer = jnp.concatenate([c, -s], axis=0)
    j = jnp.arange(n2, dtype=jnp.int32)
    ph = (2.0 * math.pi / n) * ((k1[:, None] * j[None, :]) % n).astype(F32)
    th2 = (2.0 * math.pi / n2) * ((j[:, None] * j[None, :]) % n2).astype(F32)
    cr, ci = jnp.cos(th2), -jnp.sin(th2)
    inner = jnp.concatenate([jnp.concatenate([cr, ci], axis=1),
                             jnp.concatenate([-ci, cr], axis=1)], axis=0)
    inv_inner = jnp.concatenate([jnp.concatenate([cr, -ci], axis=1),
                                 jnp.concatenate([ci, cr], axis=1)], axis=0) / n2
    return dict(n1=n1, n2=n2, outer=_split(outer), outer_half=_split(outer[:, :r]),
                inv_outer_re=_split(c[:r] / n1), inv_outer_im=_split(-s[:r] / n1),
                tw_re=jnp.cos(ph), tw_im=-jnp.sin(ph),
                inner=_split(inner), inv_inner=_split(inv_inner))


def _mat_dot(m_hi, m_lo, x, passes):
    x_hi = x.astype(BF16)
    out = _dot(m_hi, x_hi)
    if passes >= 2:
        out = out + _dot(m_hi, (x - x_hi.astype(F32)).astype(BF16))
    if passes >= 3:
        out = out + _dot(m_lo, x_hi)
    return out


def _dot_mat(x, m_hi, m_lo, passes):
    x_hi = x.astype(BF16)
    out = _dot(x_hi, m_hi)
    if passes >= 2:
        out = out + _dot((x - x_hi.astype(F32)).astype(BF16), m_hi)
    if passes >= 3:
        out = out + _dot(x_hi, m_lo)
    return out


def _forward_to_scratch(x_ref, a_scr, f1h, f1l, tw_re, tw_im, n1, cb, passes):
    half = DFT_INNER
    for c in range(cb):
        p = _mat_dot(f1h, f1l, x_ref[c], passes)
        ar, ai = p[:n1], p[n1:]
        a_scr[c, :, :half] = ar * tw_re - ai * tw_im
        a_scr[c, :, half:] = ar * tw_im + ai * tw_re


def _filter_spectrum_kernel(k_ref, inv_ref, f1h_ref, f1l_ref, twr_ref, twi_ref, mh_ref, ml_ref,
                            kr_ref, ki_ref, a_scr, *, n1, cb):
    half = DFT_INNER
    _forward_to_scratch(k_ref, a_scr, f1h_ref[...], f1l_ref[...], twr_ref[...], twi_ref[...], n1, cb, FFT_PASSES)
    x = _dot_mat(a_scr[...].reshape(cb * n1, 2 * half), mh_ref[...], ml_ref[...], FFT_PASSES)
    x = x.reshape(cb, n1, 2 * half)
    inv = inv_ref[...]
    kr_ref[...] = x[:, :, :half] * inv
    ki_ref[...] = x[:, :, half:] * inv


def _const_spec(arr):
    return pl.BlockSpec(arr.shape, lambda j: (0,) * arr.ndim)


def _filter_spectrum(kf_t, inv_sum, tables):
    ch, n1, n2 = kf_t.shape
    cb = FFT_CB
    f1h, f1l = tables["outer"]
    mh, ml = tables["inner"]
    consts = [f1h, f1l, tables["tw_re"], tables["tw_im"], mh, ml]
    blk = pl.BlockSpec((cb, n1, n2), lambda j: (j, 0, 0))
    out = jax.ShapeDtypeStruct((ch, n1, n2), F32)
    return pl.pallas_call(
        functools.partial(_filter_spectrum_kernel, n1=n1, cb=cb),
        grid=(ch // cb,),
        in_specs=[blk, pl.BlockSpec((cb, 1, n2), lambda j: (j, 0, 0))] + [_const_spec(a) for a in consts],
        out_specs=[blk, blk],
        out_shape=[out, out],
        scratch_shapes=[pltpu.VMEM((cb, n1, 2 * n2), F32)],
        compiler_params=_cparams("parallel"),
        name="filter_spectrum",
    )(kf_t, inv_sum, *consts)


def _fftconv_kernel(u_ref, gate_ref, kr_ref, ki_ref, skip_ref, f1h_ref, f1l_ref, twr_ref, twi_ref,
                    mh_ref, ml_ref, nh_ref, nl_ref, grh_ref, grl_ref, gih_ref, gil_ref,
                    o_ref, a_scr, *, n1, cb, passes):
    half = DFT_INNER
    tw_re, tw_im = twr_ref[...], twi_ref[...]
    _forward_to_scratch(u_ref, a_scr, f1h_ref[...], f1l_ref[...], tw_re, tw_im, n1, cb, passes)
    x = _dot_mat(a_scr[...].reshape(cb * n1, 2 * half), mh_ref[...], ml_ref[...], passes)
    xr, xi = x[:, :half], x[:, half:]
    kr = kr_ref[...].reshape(cb * n1, half)
    ki = ki_ref[...].reshape(cb * n1, half)
    y = jnp.concatenate([xr * kr - xi * ki, xr * ki + xi * kr], axis=1)
    b = _dot_mat(y, nh_ref[...], nl_ref[...], passes)
    a_scr[...] = b.reshape(cb, n1, 2 * half)
    grh, grl, gih, gil = grh_ref[...], grl_ref[...], gih_ref[...], gil_ref[...]
    for c in range(cb):
        br, bi = a_scr[c, :, :half], a_scr[c, :, half:]
        conv = (_mat_dot(grh, grl, br * tw_re + bi * tw_im, passes)
                + _mat_dot(gih, gil, bi * tw_re - br * tw_im, passes))
        o_ref[c] = (gate_ref[c] * (conv + u_ref[c] * skip_ref[c])).astype(o_ref.dtype)


def _fftconv(u_t, u_part, gate_t, gate_part, kr, ki, order, skip_t, tables, width, out_dtype):
    n1, n2 = tables["n1"], tables["n2"]
    rows = n1 // 2
    cb = FFT_CB
    nb = width // cb
    consts = [*tables["outer_half"], tables["tw_re"], tables["tw_im"], *tables["inner"],
              *tables["inv_inner"], *tables["inv_outer_re"], *tables["inv_outer_im"]]
    data = lambda part: pl.BlockSpec((cb, rows, n2), lambda j: (part * nb + j, 0, 0))
    spec = pl.BlockSpec((cb, n1, n2), lambda j: (order * nb + j, 0, 0))
    return pl.pallas_call(
        functools.partial(_fftconv_kernel, n1=n1, cb=cb, passes=FFT_PASSES),
        grid=(nb,),
        in_specs=[data(u_part), data(gate_part), spec, spec,
                  pl.BlockSpec((cb, 1, n2), lambda j: (j, 0, 0))] + [_const_spec(a) for a in consts],
        out_specs=pl.BlockSpec((cb, rows, n2), lambda j: (j, 0, 0)),
        out_shape=jax.ShapeDtypeStruct((width, rows, n2), out_dtype),
        scratch_shapes=[pltpu.VMEM((cb, n1, 2 * n2), F32)],
        compiler_params=_cparams("parallel"),
        name="hyena_fftconv",
    )(u_t, gate_t, kr, ki, skip_t, *consts)


def _lane_rows(v):
    return jnp.broadcast_to(v.astype(F32).reshape(-1, 1, 1), (v.shape[0], 1, DFT_INNER))


def _hyena_mixer(z, p, tables, seq, width):
    n1, n2 = tables["n1"], tables["n2"]
    rows = n1 // 2
    zc = _hyena_conv(z, p["hy_conv_w"], p["hy_conv_b"], seq, width)
    zc_t = jnp.transpose(zc, (0, 2, 1)).reshape(3 * width, rows, n2)
    kf, colsum = _hyena_filters(seq, width, p["hy_filt_w1"], p["hy_filt_b1"], p["hy_filt_w2"],
                                p["hy_filt_b2"], p["hy_filt_w3"], p["hy_filt_freq"])
    kf_t = kf.T.reshape(HY_ORDER * width, n1, n2)
    kr, ki = _filter_spectrum(kf_t, _lane_rows(1.0 / colsum.reshape(-1)), tables)
    skip = p["hy_skip"]
    y1 = _fftconv(zc_t, 0, zc_t, 1, kr, ki, 0, _lane_rows(skip[0]), tables, width, F32)
    y = _fftconv(y1, 0, zc_t, 2, kr, ki, 1, _lane_rows(skip[1]), tables, width, BF16)
    return y.reshape(width, seq).T


def _attn_kernel(q_ref, kp_ref, kc_ref, kn_ref, vp_ref, vc_ref, vn_ref, sink_ref, o_ref, *, seq, group):
    i = pl.program_id(0)
    h = pl.program_id(1)
    blk = WINDOW
    k_all = jnp.concatenate([kp_ref[...], kc_ref[...], kn_ref[...]], axis=0)
    v_all = jnp.concatenate([vp_ref[...], vc_ref[...], vn_ref[...]], axis=0)
    row = lax.broadcasted_iota(jnp.int32, (blk, 3 * blk), 0)
    col = lax.broadcasted_iota(jnp.int32, (blk, 3 * blk), 1)
    dist = jnp.abs(row + blk - col)
    distf = dist.astype(F32)
    for qb in range(ATTN_BLOCKS):
        rows = slice(qb * blk, (qb + 1) * blk)
        q = q_ref[rows, :]
        q4 = jnp.concatenate([q[:, g * HEAD_DIM:(g + 1) * HEAD_DIM] for g in range(group)], axis=0)
        k = k_all[qb * blk:(qb + 3) * blk]
        v = v_all[qb * blk:(qb + 3) * blk]
        s = lax.dot_general(q4, k, (((1,), (1,)), ((), ())), preferred_element_type=F32)
        s = s * (HEAD_DIM ** -0.5)
        k_pos = (i * ATTN_BLOCKS + qb - 1) * blk + col
        valid = (dist <= WINDOW) & (k_pos >= 0) & (k_pos < seq)
        outs = []
        for g in range(group):
            sg = s[g * blk:(g + 1) * blk] - sink_ref[KV_HEADS + h, g] * distf
            sg = jnp.where(valid, sg, NEG)
            sink = sink_ref[h, g]
            m = jnp.maximum(jnp.max(sg, axis=-1, keepdims=True), sink)
            e = jnp.exp(sg - m)
            denom = jnp.sum(e, axis=-1, keepdims=True) + jnp.exp(sink - m)
            pg = (e / denom).astype(v.dtype)
            outs.append(_dot(pg, v))
        o_ref[rows, :] = jnp.concatenate(outs, axis=1).astype(o_ref.dtype)


def _attention(z, sink, seq, width, q_off, k_off, v_off):
    group = width // HEAD_DIM // KV_HEADS
    gw = group * HEAD_DIM
    ab = ATTN_BLOCKS
    rows = ab * WINDOW
    nb = seq // WINDOW
    qb, kb, vb = q_off // gw, k_off // HEAD_DIM, v_off // HEAD_DIM
    assert q_off % gw == 0 and k_off % HEAD_DIM == 0 and v_off % HEAD_DIM == 0 and seq % rows == 0
    halo = lambda base, f: pl.BlockSpec((WINDOW, HEAD_DIM), lambda i, h: (f(i), base + h))
    prev = lambda i: jnp.maximum(i * ab - 1, 0)
    nxt = lambda i: jnp.minimum((i + 1) * ab, nb - 1)
    main = lambda base: pl.BlockSpec((rows, HEAD_DIM), lambda i, h: (i, base + h))
    n_q = KV_HEADS * group
    slopes = 2.0 ** (-8.0 * (jnp.arange(n_q, dtype=F32) + 1.0) / n_q)
    scalars = jnp.concatenate([sink.astype(F32), slopes]).reshape(2 * KV_HEADS, group)
    return pl.pallas_call(
        functools.partial(_attn_kernel, seq=seq, group=group),
        grid=(seq // rows, KV_HEADS),
        in_specs=[pl.BlockSpec((rows, gw), lambda i, h: (i, qb + h)),
                  halo(kb, prev), main(kb), halo(kb, nxt),
                  halo(vb, prev), main(vb), halo(vb, nxt),
                  pl.BlockSpec(memory_space=pltpu.SMEM)],
        out_specs=pl.BlockSpec((rows, gw), lambda i, h: (i, h)),
        out_shape=jax.ShapeDtypeStruct((seq, width), BF16),
        compiler_params=_cparams("parallel", "parallel"),
        name="window_attention",
    )(z, z, z, z, z, z, z, scalars)


def _retention_kernel(*refs, heads, reverse, final, cps):
    if final:
        cd_ref, dec_ref, kw_ref, qw_ref, q_ref, k_ref, v_ref, prev_ref, g_ref, o_ref, s_ref = refs
    else:
        cd_ref, dec_ref, kw_ref, qw_ref, q_ref, k_ref, v_ref, o_ref, s_ref = refs
    hg = pl.program_id(0)
    n = pl.program_id(1)
    c = RET_CHUNK

    @pl.when(n == 0)
    def _():
        s_ref[...] = jnp.zeros_like(s_ref)

    chunk_order = range(cps - 1, -1, -1) if reverse else range(cps)
    for hd in range(heads):
        sl = slice(hd * HEAD_DIM, (hd + 1) * HEAD_DIM)
        decay, k_w, q_w = dec_ref[hd], kw_ref[hd], qw_ref[hd]
        chunk_decay = cd_ref[hg * heads + hd]
        state = s_ref[hd]
        for cc in chunk_order:
            rows = slice(cc * c, (cc + 1) * c)
            q, k, v = q_ref[rows, sl], k_ref[rows, sl], v_ref[rows, sl]
            qk = lax.dot_general(q, k, (((1,), (1,)), ((), ())), preferred_element_type=F32)
            inner = _dot((qk * decay).astype(v.dtype), v)
            cross = _dot((q.astype(F32) * q_w).astype(BF16), state.astype(BF16))
            kv = lax.dot_general((k.astype(F32) * k_w).astype(BF16), v, (((0,), (0,)), ((), ())),
                                 preferred_element_type=F32)
            state = state * chunk_decay + kv
            o = inner + cross
            if final:
                o = o + prev_ref[rows, sl]
                o = o * lax.rsqrt(jnp.mean(o * o, axis=-1, keepdims=True) + EPS)
                g = g_ref[rows, sl].astype(F32)
                o = g * _sigmoid(g) * o
            o_ref[rows, sl] = o.astype(o_ref.dtype)
        s_ref[hd] = state


def _retention_pass(z, seq, width, offs, reverse, prev=None):
    c = RET_CHUNK
    cps = RET_CHUNKS_PER_STEP
    rows = cps * c
    nb = seq // rows
    gw = math.gcd(width, *offs)
    assert gw % HEAD_DIM == 0 and seq % rows == 0
    heads = gw // HEAD_DIM
    q_off, k_off, v_off, g_off = offs
    hidx = jnp.arange(width // HEAD_DIM, dtype=F32)
    idx = jnp.arange(c, dtype=F32)
    rel = idx[:, None] - idx[None, :]
    scale = HEAD_DIM ** -0.5
    if reverse:
        lg = jnp.log(1.0 - 2.0 ** (-5.5 - hidx))[:, None]
        order = lambda n: nb - 1 - n
        rel, k_pow, q_pow = -rel, idx, c - idx
    else:
        lg = jnp.log(1.0 - 2.0 ** (-5.0 - hidx))[:, None]
        order = lambda n: n
        k_pow, q_pow = c - 1.0 - idx, idx + 1.0
    decay = jnp.where(rel >= 0, jnp.exp(jnp.maximum(rel, 0.0)[None] * lg[:, :, None]), 0.0) * scale
    lanes = lambda a: jnp.broadcast_to(a[:, :, None], a.shape + (HEAD_DIM,))
    k_w = lanes(jnp.exp(k_pow[None] * lg) * scale)
    q_w = lanes(jnp.exp(q_pow[None] * lg))
    chunk_decay = jnp.exp(c * lg[:, 0])
    final = prev is not None
    zspec = lambda off: pl.BlockSpec((rows, gw), lambda g, n: (order(n), off // gw + g))
    own = pl.BlockSpec((rows, gw), lambda g, n: (order(n), g))
    table = pl.BlockSpec((heads, c, HEAD_DIM), lambda g, n: (g, 0, 0))
    in_specs = [pl.BlockSpec(memory_space=pltpu.SMEM), table, table, table,
                zspec(q_off), zspec(k_off), zspec(v_off)]
    args = [chunk_decay, decay, k_w, q_w, z, z, z]
    if final:
        in_specs += [own, zspec(g_off)]
        args += [prev, z]
    return pl.pallas_call(
        functools.partial(_retention_kernel, heads=heads, reverse=reverse, final=final, cps=cps),
        grid=(width // gw, nb),
        in_specs=in_specs,
        out_specs=own,
        out_shape=jax.ShapeDtypeStruct((seq, width), BF16 if final else F32),
        scratch_shapes=[pltpu.VMEM((heads, HEAD_DIM, HEAD_DIM), F32)],
        compiler_params=_cparams("parallel", "arbitrary"),
        name="retention_fwd" if final else "retention_bwd",
    )(*args)


def _retention(z, seq, width, offs):
    o_bwd = _retention_pass(z, seq, width, offs, reverse=True)
    return _retention_pass(z, seq, width, offs, reverse=False, prev=o_bwd)


def _merge_kernel(ya_ref, yb_ref, yc_ref, wa_ref, wb_ref, wc_ref, ga_ref, gb_ref, gc_ref, o_ref):
    acc = None
    for y_ref, w_ref, g_ref in ((ya_ref, wa_ref, ga_ref), (yb_ref, wb_ref, gb_ref), (yc_ref, wc_ref, gc_ref)):
        t = _sigmoid(g_ref[...].astype(F32)) * _dot(y_ref[...], w_ref[...])
        acc = t if acc is None else acc + t
    o_ref[...] = acc.astype(o_ref.dtype)


def _merge(ya, yb, yc, w_branch, layer, z, gate_off, d_model):
    seq, width = ya.shape
    tm, tn = _tile(seq, 1024), _tile(d_model, 512)
    assert gate_off % tn == 0
    ys = pl.BlockSpec((tm, width), lambda i, j: (i, 0))
    ws = lambda b: pl.BlockSpec((None, None, width, tn), lambda i, j: (layer, b, 0, j))
    gs = lambda b: pl.BlockSpec((tm, tn), lambda i, j: (i, (gate_off + b * d_model) // tn + j))
    return pl.pallas_call(
        _merge_kernel,
        grid=(seq // tm, d_model // tn),
        in_specs=[ys, ys, ys, ws(0), ws(1), ws(2), gs(0), gs(1), gs(2)],
        out_specs=pl.BlockSpec((tm, tn), lambda i, j: (i, j)),
        out_shape=jax.ShapeDtypeStruct((seq, d_model), BF16),
        compiler_params=_cparams("parallel", "parallel"),
        name="branch_merge",
    )(ya, yb, yc, w_branch, w_branch, w_branch, z, z, z)


def _ffn_in_kernel(h_ref, hp_ref, hn_ref, wg_ref, wu_ref, cw_ref, cb_ref, o_ref, *, n_tiles, n_split):
    i = pl.program_id(0)
    first = (i > 0).astype(F32)
    last = (i < n_tiles - 1).astype(F32)
    cols = o_ref.shape[1] // n_split
    for part in range(n_split):
        cs = slice(part * cols, (part + 1) * cols)
        wg = wg_ref[:, cs]
        g = _dot(h_ref[...], wg)
        prev_row = _dot(hp_ref[...], wg)[HALO - 1:HALO] * first
        next_row = _dot(hn_ref[...], wg)[0:1] * last
        down, up = _shift_rows(g, prev_row, next_row)
        cw = cw_ref[:, cs]
        c = down * cw[0:1] + g * cw[1:2] + up * cw[2:3] + cb_ref[:, cs]
        gelu = 0.5 * c * (1.0 + lax.erf(c * (2.0 ** -0.5)))
        o_ref[:, cs] = (gelu * _dot(h_ref[...], wu_ref[:, cs])).astype(o_ref.dtype)


def _ffn_in(h, w_in, conv_w, conv_b, layer, ffp):
    seq, d = h.shape
    tm, tn = _tile(seq, 1024), _tile(ffp, 512)
    n_tiles, n_halo, per = seq // tm, seq // HALO, tm // HALO
    nj = ffp // tn
    n_split = 1
    return pl.pallas_call(
        functools.partial(_ffn_in_kernel, n_tiles=n_tiles, n_split=n_split),
        grid=(n_tiles, nj),
        in_specs=[
            pl.BlockSpec((tm, d), lambda i, j: (i, 0)),
            pl.BlockSpec((HALO, d), lambda i, j: (jnp.maximum(i * per - 1, 0), 0)),
            pl.BlockSpec((HALO, d), lambda i, j: (jnp.minimum((i + 1) * per, n_halo - 1), 0)),
            pl.BlockSpec((None, d, tn), lambda i, j: (layer, 0, j)),
            pl.BlockSpec((None, d, tn), lambda i, j: (layer, 0, nj + j)),
            pl.BlockSpec((None, 3, tn), lambda i, j: (layer, 0, j)),
            pl.BlockSpec((None, 1, tn), lambda i, j: (layer, 0, j)),
        ],
        out_specs=pl.BlockSpec((tm, tn), lambda i, j: (i, j)),
        out_shape=jax.ShapeDtypeStruct((seq, ffp), BF16),
        compiler_params=_cparams("parallel", "parallel"),
        name="ffn_in",
    )(h, h, h, w_in, w_in, conv_w, conv_b)


MATMUL_WEIGHTS = ("w_in", "w_branch", "w_out", "w_ffn_in", "w_ffn_out", "ffn_conv_w", "ffn_conv_b")


def _prepare_weights(w):
    ff = w["w_ffn_out"].shape[1]
    ffp = -(-ff // FF_PAD) * FF_PAD
    pad = ffp - ff
    last = lambda a: jnp.pad(a, ((0, 0),) * (a.ndim - 1) + ((0, pad),))
    w_ffn_in = w["w_ffn_in"]
    w_in = w["w_in"]
    width = w_in.shape[1] // 4
    kv = 2 * KV_HEADS * HEAD_DIM
    w_in = jnp.concatenate([w_in[..., :4 * width].astype(BF16),
                            w_in[..., 4 * width + kv:8 * width + kv].astype(BF16),
                            w_in[..., 4 * width:4 * width + kv].astype(BF16),
                            w_in[..., 8 * width + kv:].astype(BF16)], axis=-1)
    return dict(
        w_in=w_in,
        w_branch=w["w_branch"].astype(BF16),
        w_out=w["w_out"].astype(BF16),
        w_ffn_in=jnp.concatenate([last(w_ffn_in[..., :ff].astype(BF16)),
                                  last(w_ffn_in[..., ff:].astype(BF16))], axis=-1),
        w_ffn_out=jnp.pad(w["w_ffn_out"].astype(BF16), ((0, 0), (0, pad), (0, 0))),
        ffn_conv_w=last(w["ffn_conv_w"].astype(F32)),
        ffn_conv_b=last(w["ffn_conv_b"].astype(F32))[:, None, :],
        ffp=ffp,
    )


def _layer(x, l, p, tables):
    seq, d = x.shape
    width = d // 4
    kvw = KV_HEADS * HEAD_DIM
    aq, r_offs = 3 * width, [4 * width, 5 * width, 6 * width, 7 * width]
    ak, av, gates = 8 * width, 8 * width + kvw, 8 * width + 2 * kvw
    small = {name: arr[l] for name, arr in p.items() if name not in MATMUL_WEIGHTS and name != "ffp"}
    h = _rmsnorm(x, small["norm_mix"], BF16)
    z = _matmul(h, p["w_in"], l, tm=1024, tn=512, out_dtype=BF16, name="in_proj")
    y_a = _hyena_mixer(z, small, tables, seq, width)
    y_b = _attention(z, small["attn_sink"], seq, width, aq, ak, av)
    y_c = _retention(z, seq, width, r_offs)
    merged = _merge(y_a, y_b, y_c, p["w_branch"], l, z, gates, d)
    x = _matmul(merged, p["w_out"], l, tm=1024, tn=512, out_dtype=F32, residual=x, name="out_proj")
    h = _rmsnorm(x, small["norm_ffn"], BF16)
    act = _ffn_in(h, p["w_ffn_in"], p["ffn_conv_w"], p["ffn_conv_b"], l, p["ffp"])
    return _matmul(act, p["w_ffn_out"], l, tm=1024, tn=512, tk=p["ffp"] // 2, out_dtype=F32, residual=x,
                   name="ffn_out")


def _trunk(x, p, norm_final):
    b, seq, d = x.shape
    assert b == 1
    tables = _fft_tables(seq)
    x = x.reshape(seq, d)
    for l in range(p["w_in"].shape[0]):
        x = _layer(x, l, p, tables)
    return _rmsnorm(x, norm_final, F32).reshape(b, seq, d)


def kernel(x_prompt, x_sample, norm_mix, w_in, hy_conv_w, hy_conv_b, hy_filt_w1, hy_filt_b1, hy_filt_w2, hy_filt_b2, hy_filt_w3, hy_filt_freq, hy_skip, attn_sink, w_branch, w_out, norm_ffn, w_ffn_in, ffn_conv_w, ffn_conv_b, w_ffn_out, norm_final):
    w = dict(norm_mix=norm_mix, w_in=w_in, hy_conv_w=hy_conv_w, hy_conv_b=hy_conv_b,
             hy_filt_w1=hy_filt_w1, hy_filt_b1=hy_filt_b1, hy_filt_w2=hy_filt_w2,
             hy_filt_b2=hy_filt_b2, hy_filt_w3=hy_filt_w3, hy_filt_freq=hy_filt_freq,
             hy_skip=hy_skip, attn_sink=attn_sink, w_branch=w_branch, w_out=w_out,
             norm_ffn=norm_ffn, w_ffn_in=w_ffn_in, ffn_conv_w=ffn_conv_w,
             ffn_conv_b=ffn_conv_b, w_ffn_out=w_ffn_out)
    p = dict(w)
    p.update(_prepare_weights(w))
    return (_trunk(x_prompt, p, norm_final), _trunk(x_sample, p, norm_final))
```

```python
import functools
import math

import jax
import jax.numpy as jnp
from jax import lax
from jax.experimental import pallas as pl
from jax.experimental.pallas import tpu as pltpu

F32 = jnp.float32
BF16 = jnp.bfloat16

LANES = 128
SUBLANES = 8
EPS = 1e-6
NEG = -1e30
HEAD_DIM = 128
KV_HEADS = 2
WINDOW = 128
ATTN_BLOCKS = 4
RET_CHUNK = 128
RET_CHUNKS_PER_STEP = 2
HY_ORDER = 2
HY_EMB = 33
HY_BANDS = (HY_EMB - 1) // 2
HY_MIN_DECAY = math.log(1e-2) / 1.5
HY_MAX_DECAY = math.log(1e-2) / 0.3
DFT_INNER = 128
FFT_CB = 16
FFT_PASSES = 1
FEAT_LANES = 128
HALO = 16
FF_PAD = 1024
VMEM_LIMIT = 56 * 1024 * 1024


def _cparams(*sem, vmem=VMEM_LIMIT):
    return pltpu.CompilerParams(dimension_semantics=sem, vmem_limit_bytes=vmem)


def _tile(n, pref):
    if n <= pref:
        return n
    t = (pref // 128) * 128
    while t >= 128:
        if n % t == 0:
            return t
        t -= 128
    return n


def _dot(a, b):
    return jnp.dot(a, b, preferred_element_type=F32)


def _sigmoid(x):
    return 0.5 * jnp.tanh(0.5 * x) + 0.5


def _split(x):
    hi = x.astype(BF16)
    lo = (x - hi.astype(F32)).astype(BF16)
    return hi, lo


def _dot3(a, b):
    a_hi, a_lo = _split(a)
    b_hi, b_lo = _split(b)
    return _dot(a_hi, b_hi) + (_dot(a_lo, b_hi) + _dot(a_hi, b_lo))


def _rmsnorm_kernel(x_ref, g_ref, o_ref):
    x = x_ref[...]
    ms = jnp.mean(x * x, axis=-1, keepdims=True)
    o_ref[...] = (x * lax.rsqrt(ms + EPS) * g_ref[...]).astype(o_ref.dtype)


def _rmsnorm(x, g, out_dtype):
    m, d = x.shape
    tm = _tile(m, 256)
    return pl.pallas_call(
        _rmsnorm_kernel,
        grid=(m // tm,),
        in_specs=[pl.BlockSpec((tm, d), lambda i: (i, 0)),
                  pl.BlockSpec((1, d), lambda i: (0, 0))],
        out_specs=pl.BlockSpec((tm, d), lambda i: (i, 0)),
        out_shape=jax.ShapeDtypeStruct((m, d), out_dtype),
        compiler_params=_cparams("parallel"),
        name="rmsnorm",
    )(x, g.reshape(1, d).astype(F32))


def _row_scale(ssq, d):
    return lax.rsqrt(ssq[:, 0:1] * (1.0 / d) + EPS)


def _matmul_kernel(*refs, nk, has_res, in_scale_d, norm_out):
    refs = list(refs)
    a_ref, b_ref = refs[:2]
    pos = 2
    s_ref = r_ref = gn_ref = xg_ref = so_ref = None
    if in_scale_d:
        s_ref, pos = refs[pos], pos + 1
    if has_res:
        r_ref, pos = refs[pos], pos + 1
    if norm_out:
        gn_ref, pos = refs[pos], pos + 1
    o_ref, pos = refs[pos], pos + 1
    if norm_out:
        xg_ref, so_ref = refs[pos], refs[pos + 1]
    p = _dot(a_ref[...], b_ref[...])
    j = pl.program_id(1)

    def finish(acc):
        if in_scale_d:
            acc = acc * _row_scale(s_ref[...], in_scale_d)
        if has_res:
            acc = acc + r_ref[...]
        o_ref[...] = acc.astype(o_ref.dtype)
        if norm_out:
            xg_ref[...] = (acc * gn_ref[...]).astype(xg_ref.dtype)
            part = jnp.broadcast_to(jnp.sum(acc * acc, axis=1, keepdims=True), so_ref.shape)

            @pl.when(j == 0)
            def _():
                so_ref[...] = part

            @pl.when(j > 0)
            def _():
                so_ref[...] += part

    if nk == 1:
        finish(p)
        return
    acc_ref = refs[-1]
    k = pl.program_id(2)

    @pl.when(k == 0)
    def _():
        acc_ref[...] = p

    @pl.when(k > 0)
    def _():
        acc_ref[...] += p

    @pl.when(k == nk - 1)
    def _():
        finish(acc_ref[...])


def _matmul(a, b, layer, *, tm, tn, tk=None, out_dtype, ssq=None, residual=None, next_gain=None,
            out_col=lambda j: j, name):
    m, kdim = a.shape
    n = b.shape[2]
    tm, tn = _tile(m, tm), _tile(n, tn)
    tk = kdim if tk is None else _tile(kdim, tk)
    nk = kdim // tk
    in_specs = [pl.BlockSpec((tm, tk), lambda i, j, k: (i, k)),
                pl.BlockSpec((None, tk, tn), lambda i, j, k: (layer, k, j))]
    args = [a, b]
    row_stat = pl.BlockSpec((tm, LANES), lambda i, j, k: (i, 0))
    tile = pl.BlockSpec((tm, tn), lambda i, j, k: (i, j))
    if ssq is not None:
        in_specs.append(row_stat)
        args.append(ssq)
    if residual is not None:
        in_specs.append(tile)
        args.append(residual)
    out_specs = pl.BlockSpec((tm, tn), lambda i, j, k: (i, out_col(j)))
    out_shape = jax.ShapeDtypeStruct((m, n), out_dtype)
    if next_gain is not None:
        in_specs.append(pl.BlockSpec((1, tn), lambda i, j, k: (0, j)))
        args.append(next_gain.reshape(1, n).astype(F32))
        out_specs = [out_specs, tile, row_stat]
        out_shape = [out_shape, jax.ShapeDtypeStruct((m, n), BF16), jax.ShapeDtypeStruct((m, LANES), F32)]
    scratch = [pltpu.VMEM((tm, tn), F32)] if nk > 1 else []
    return pl.pallas_call(
        functools.partial(_matmul_kernel, nk=nk, has_res=residual is not None,
                          in_scale_d=kdim if ssq is not None else 0, norm_out=next_gain is not None),
        grid=(m // tm, n // tn, nk),
        in_specs=in_specs,
        out_specs=out_specs,
        out_shape=out_shape,
        scratch_shapes=scratch,
        compiler_params=_cparams("parallel", "arbitrary", "arbitrary"),
        name=name,
    )(*args)


def _shift_rows(x, prev_row, next_row):
    tm = x.shape[0]
    row = lax.broadcasted_iota(jnp.int32, x.shape, 0)
    down = jnp.where(row == 0, prev_row, pltpu.roll(x, 1, axis=0))
    up = jnp.where(row == tm - 1, next_row, pltpu.roll(x, tm - 1, axis=0))
    return down, up


def _hyconv_kernel(x_ref, xp_ref, xn_ref, w_ref, b_ref, o_ref, *, n_tiles):
    i = pl.program_id(0)
    x = x_ref[...].astype(F32)
    prev_row = xp_ref[...].astype(F32)[HALO - 1:HALO] * (i > 0).astype(F32)
    next_row = xn_ref[...].astype(F32)[0:1] * (i < n_tiles - 1).astype(F32)
    down, up = _shift_rows(x, prev_row, next_row)
    w = w_ref[...]
    o_ref[...] = (down * w[0:1] + x * w[1:2] + up * w[2:3] + b_ref[...]).astype(o_ref.dtype)


def _hyena_conv(z, conv_w, conv_b, seq, width):
    tm = _tile(seq, 512)
    n_tiles = seq // tm
    n_halo = seq // HALO
    per = tm // HALO
    return pl.pallas_call(
        functools.partial(_hyconv_kernel, n_tiles=n_tiles),
        grid=(n_tiles, 3),
        in_specs=[
            pl.BlockSpec((tm, width), lambda i, p: (i, p)),
            pl.BlockSpec((HALO, width), lambda i, p: (jnp.maximum(i * per - 1, 0), p)),
            pl.BlockSpec((HALO, width), lambda i, p: (jnp.minimum((i + 1) * per, n_halo - 1), p)),
            pl.BlockSpec((3, width), lambda i, p: (0, p)),
            pl.BlockSpec((1, width), lambda i, p: (0, p)),
        ],
        out_specs=pl.BlockSpec((None, tm, width), lambda i, p: (p, i, 0)),
        out_shape=jax.ShapeDtypeStruct((3, seq, width), BF16),
        compiler_params=_cparams("parallel", "parallel"),
        name="hyena_conv3",
    )(z, z, z, conv_w.astype(F32), conv_b.reshape(1, -1).astype(F32))


def _filter_kernel(bands_ref, w1_ref, b1_ref, w2_ref, b2_ref, freq_ref, w3_ref, delta_ref,
                   w3b_ref, k_ref, sum_ref, *, seq, tr, width):
    i = pl.program_id(0)
    n = 2 * seq
    r = i * tr + lax.broadcasted_iota(jnp.int32, (tr, FEAT_LANES), 0)
    pos = jnp.where(r < seq, r, n - r).astype(F32)
    t = pos / float(seq - 1)
    lane = lax.broadcasted_iota(jnp.int32, (tr, FEAT_LANES), 1)
    ang = (2.0 * math.pi / seq) * pos * bands_ref[...]
    quarter = jnp.where((lane >= HY_BANDS) & (lane < 2 * HY_BANDS), 0.5 * math.pi, 0.0)
    feats = jnp.where(lane < 2 * HY_BANDS, jnp.cos(ang + quarter),
                      jnp.where(lane == 2 * HY_BANDS, t, 0.0))
    freq = freq_ref[...]
    h = jnp.sin(freq * (_dot3(feats, w1_ref[...]) + b1_ref[...]))
    h = jnp.sin(freq * (_dot3(h, w2_ref[...]) + b2_ref[...]))
    h_hi, h_lo = _split(h)
    r1 = i * tr + lax.broadcasted_iota(jnp.int32, (tr, 1), 0)
    t1 = jnp.where(r1 < seq, r1, n - r1).astype(F32) / float(seq - 1)
    window = jnp.where(r1 == seq, 0.0, jnp.exp(-t1 * delta_ref[...]))
    parts = []
    for o, w3 in enumerate((w3_ref[...], w3b_ref[...])):
        w_hi, w_lo = _split(w3)
        kf = (_dot(h_hi, w_hi) + (_dot(h_lo, w_hi) + _dot(h_hi, w_lo))) * window
        k_ref[:, o * width:(o + 1) * width] = kf.astype(k_ref.dtype)
        parts.append(jnp.sum(jnp.abs(kf), axis=0, keepdims=True))
    part = jnp.concatenate(parts, axis=1)

    @pl.when(i == 0)
    def _():
        sum_ref[...] = part

    @pl.when(i > 0)
    def _():
        sum_ref[...] += part


def _hyena_filters(seq, width, w1, b1, w2, b2, w3, freq):
    n = 2 * seq
    tr = _tile(seq, 512)
    nb = n // tr
    fw = w1.shape[1]
    bands = jnp.linspace(1e-4, HY_BANDS - 1, HY_BANDS, dtype=F32)
    bands = jnp.concatenate([bands, bands, jnp.zeros((FEAT_LANES - 2 * HY_BANDS,), F32)]).reshape(1, -1)
    w1 = w1.astype(F32)
    w1p = jnp.concatenate([w1[1:], w1[:1], jnp.zeros((FEAT_LANES - HY_EMB, fw), F32)], axis=0)
    deltas = jnp.abs(jnp.linspace(HY_MIN_DECAY, HY_MAX_DECAY, width, dtype=F32)).reshape(1, -1)
    row = lambda a: a.reshape(1, -1).astype(F32)
    const = lambda shape: pl.BlockSpec(shape, lambda i: (0, 0))
    w3_spec = lambda o: pl.BlockSpec((fw, width), lambda i: (0, 2 * o + (i >= nb // 2).astype(jnp.int32)))
    assert HY_ORDER == 2
    w3 = w3.astype(F32)
    return pl.pallas_call(
        functools.partial(_filter_kernel, seq=seq, tr=tr, width=width),
        grid=(nb,),
        in_specs=[
            const((1, FEAT_LANES)), const((FEAT_LANES, fw)), const((1, fw)), const((fw, fw)),
            const((1, fw)), const((1, fw)), w3_spec(0), const((1, width)), w3_spec(1),
        ],
        out_specs=[pl.BlockSpec((tr, HY_ORDER * width), lambda i: (i, 0)),
                   pl.BlockSpec((1, HY_ORDER * width), lambda i: (0, 0))],
        out_shape=[jax.ShapeDtypeStruct((n, HY_ORDER * width), BF16),
                   jax.ShapeDtypeStruct((1, HY_ORDER * width), F32)],
        compiler_params=_cparams("arbitrary"),
        name="hyena_filters",
    )(bands, w1p, row(b1), w2.astype(F32), row(b2), row(freq), w3, deltas, w3)


def _fft_tables(seq):
    n = 2 * seq
    n2 = DFT_INNER
    n1 = n // n2
    r = n1 // 2
    k1 = jnp.arange(n1, dtype=jnp.int32)
    th = (2.0 * math.pi / n1) * ((k1[:, None] * k1[None, :]) % n1).astype(F32)
    c, s = jnp.cos(th), jnp.sin(th)
    outer = jnp.concatenate([c, -s], axis=0)
    j = jnp.arange(n2, dtype=jnp.int32)
    ph = (2.0 * math.pi / n) * ((k1[:, None] * j[None, :]) % n).astype(F32)
    th2 = (2.0 * math.pi / n2) * ((j[:, None] * j[None, :]) % n2).astype(F32)
    cr, ci = jnp.cos(th2), -jnp.sin(th2)
    inner = jnp.concatenate([jnp.concatenate([cr, ci], axis=1),
                             jnp.concatenate([-ci, cr], axis=1)], axis=0)
    inv_inner = jnp.concatenate([jnp.concatenate([cr, -ci], axis=1),
                                 jnp.concatenate([ci, cr], axis=1)], axis=0) / n2
    return dict(n1=n1, n2=n2, outer=_split(outer), outer_half=_split(outer[:, :r]),
                inv_outer_re=_split(c[:r] / n1), inv_outer_im=_split(-s[:r] / n1),
                tw_re=jnp.cos(ph), tw_im=-jnp.sin(ph),
                inner=_split(inner), inv_inner=_split(inv_inner))


def _mat_dot(m_hi, m_lo, x, passes):
    x_hi = x.astype(BF16)
    out = _dot(m_hi, x_hi)
    if passes >= 2:
        out = out + _dot(m_hi, (x - x_hi.astype(F32)).astype(BF16))
    if passes >= 3:
        out = out + _dot(m_lo, x_hi)
    return out


def _dot_mat(x, m_hi, m_lo, passes):
    x_hi = x.astype(BF16)
    out = _dot(x_hi, m_hi)
    if passes >= 2:
        out = out + _dot((x - x_hi.astype(F32)).astype(BF16), m_hi)
    if passes >= 3:
        out = out + _dot(x_hi, m_lo)
    return out


def _forward_to_scratch(x_ref, a_scr, f1h, f1l, tw_re, tw_im, n1, cb, passes):
    half = DFT_INNER
    for c in range(cb):
        p = _mat_dot(f1h, f1l, x_ref[c], passes)
        ar, ai = p[:n1], p[n1:]
        a_scr[c, :, :half] = ar * tw_re - ai * tw_im
        a_scr[c, :, half:] = ar * tw_im + ai * tw_re


def _filter_spectrum_kernel(k_ref, inv_ref, f1h_ref, f1l_ref, twr_ref, twi_ref, mh_ref, ml_ref,
                            kr_ref, ki_ref, a_scr, *, n1, cb):
    half = DFT_INNER
    _forward_to_scratch(k_ref, a_scr, f1h_ref[...], f1l_ref[...], twr_ref[...], twi_ref[...], n1, cb, FFT_PASSES)
    x = _dot_mat(a_scr[...].reshape(cb * n1, 2 * half), mh_ref[...], ml_ref[...], FFT_PASSES)
    x = x.reshape(cb, n1, 2 * half)
    inv = inv_ref[...]
    kr_ref[...] = x[:, :, :half] * inv
    ki_ref[...] = x[:, :, half:] * inv


def _const_spec(arr):
    return pl.BlockSpec(arr.shape, lambda j: (0,) * arr.ndim)


def _filter_spectrum(kf_t, inv_sum, tables):
    ch, n1, n2 = kf_t.shape
    cb = FFT_CB
    f1h, f1l = tables["outer"]
    mh, ml = tables["inner"]
    consts = [f1h, f1l, tables["tw_re"], tables["tw_im"], mh, ml]
    blk = pl.BlockSpec((cb, n1, n2), lambda j: (j, 0, 0))
    out = jax.ShapeDtypeStruct((ch, n1, n2), F32)
    return pl.pallas_call(
        functools.partial(_filter_spectrum_kernel, n1=n1, cb=cb),
        grid=(ch // cb,),
        in_specs=[blk, pl.BlockSpec((cb, 1, n2), lambda j: (j, 0, 0))] + [_const_spec(a) for a in consts],
        out_specs=[blk, blk],
        out_shape=[out, out],
        scratch_shapes=[pltpu.VMEM((cb, n1, 2 * n2), F32)],
        compiler_params=_cparams("parallel"),
        name="filter_spectrum",
    )(kf_t, inv_sum, *consts)


def _fftconv_kernel(u_ref, gate_ref, kr_ref, ki_ref, skip_ref, f1h_ref, f1l_ref, twr_ref, twi_ref,
                    mh_ref, ml_ref, nh_ref, nl_ref, grh_ref, grl_ref, gih_ref, gil_ref,
                    o_ref, a_scr, *, n1, cb, passes):
    half = DFT_INNER
    tw_re, tw_im = twr_ref[...], twi_ref[...]
    _forward_to_scratch(u_ref, a_scr, f1h_ref[...], f1l_ref[...], tw_re, tw_im, n1, cb, passes)
    x = _dot_mat(a_scr[...].reshape(cb * n1, 2 * half), mh_ref[...], ml_ref[...], passes)
    xr, xi = x[:, :half], x[:, half:]
    kr = kr_ref[...].reshape(cb * n1, half)
    ki = ki_ref[...].reshape(cb * n1, half)
    y = jnp.concatenate([xr * kr - xi * ki, xr * ki + xi * kr], axis=1)
    b = _dot_mat(y, nh_ref[...], nl_ref[...], passes)
    a_scr[...] = b.reshape(cb, n1, 2 * half)
    grh, grl, gih, gil = grh_ref[...], grl_ref[...], gih_ref[...], gil_ref[...]
    for c in range(cb):
        br, bi = a_scr[c, :, :half], a_scr[c, :, half:]
        conv = (_mat_dot(grh, grl, br * tw_re + bi * tw_im, passes)
                + _mat_dot(gih, gil, bi * tw_re - br * tw_im, passes))
        o_ref[c] = (gate_ref[c] * (conv + u_ref[c] * skip_ref[c])).astype(o_ref.dtype)


def _fftconv(u_t, u_part, gate_t, gate_part, kr, ki, order, skip_t, tables, width, out_dtype):
    n1, n2 = tables["n1"], tables["n2"]
    rows = n1 // 2
    cb = FFT_CB
    nb = width // cb
    consts = [*tables["outer_half"], tables["tw_re"], tables["tw_im"], *tables["inner"],
              *tables["inv_inner"], *tables["inv_outer_re"], *tables["inv_outer_im"]]
    data = lambda part: pl.BlockSpec((cb, rows, n2), lambda j: (part * nb + j, 0, 0))
    spec = pl.BlockSpec((cb, n1, n2), lambda j: (order * nb + j, 0, 0))
    return pl.pallas_call(
        functools.partial(_fftconv_kernel, n1=n1, cb=cb, passes=FFT_PASSES),
        grid=(nb,),
        in_specs=[data(u_part), data(gate_part), spec, spec,
                  pl.BlockSpec((cb, 1, n2), lambda j: (j, 0, 0))] + [_const_spec(a) for a in consts],
        out_specs=pl.BlockSpec((cb, rows, n2), lambda j: (j, 0, 0)),
        out_shape=jax.ShapeDtypeStruct((width, rows, n2), out_dtype),
        scratch_shapes=[pltpu.VMEM((cb, n1, 2 * n2), F32)],
        compiler_params=_cparams("parallel"),
        name="hyena_fftconv",
    )(u_t, gate_t, kr, ki, skip_t, *consts)


def _lane_rows(v):
    return jnp.broadcast_to(v.astype(F32).reshape(-1, 1, 1), (v.shape[0], 1, DFT_INNER))


def _hyena_mixer(z, p, tables, seq, width):
    n1, n2 = tables["n1"], tables["n2"]
    rows = n1 // 2
    zc = _hyena_conv(z, p["hy_conv_w"], p["hy_conv_b"], seq, width)
    zc_t = jnp.transpose(zc, (0, 2, 1)).reshape(3 * width, rows, n2)
    kf, colsum = _hyena_filters(seq, width, p["hy_filt_w1"], p["hy_filt_b1"], p["hy_filt_w2"],
                                p["hy_filt_b2"], p["hy_filt_w3"], p["hy_filt_freq"])
    kf_t = kf.T.reshape(HY_ORDER * width, n1, n2)
    kr, ki = _filter_spectrum(kf_t, _lane_rows(1.0 / colsum.reshape(-1)), tables)
    skip = p["hy_skip"]
    y1 = _fftconv(zc_t, 0, zc_t, 1, kr, ki, 0, _lane_rows(skip[0]), tables, width, F32)
    y = _fftconv(y1, 0, zc_t, 2, kr, ki, 1, _lane_rows(skip[1]), tables, width, BF16)
    return y.reshape(width, seq).T


def _attn_kernel(q_ref, kp_ref, kc_ref, kn_ref, vp_ref, vc_ref, vn_ref, sink_ref, o_ref, *, seq, group):
    i = pl.program_id(0)
    h = pl.program_id(1)
    blk = WINDOW
    k_all = jnp.concatenate([kp_ref[...], kc_ref[...], kn_ref[...]], axis=0)
    v_all = jnp.concatenate([vp_ref[...], vc_ref[...], vn_ref[...]], axis=0)
    row = lax.broadcasted_iota(jnp.int32, (blk, 3 * blk), 0)
    col = lax.broadcasted_iota(jnp.int32, (blk, 3 * blk), 1)
    dist = jnp.abs(row + blk - col)
    distf = dist.astype(F32)
    for qb in range(ATTN_BLOCKS):
        rows = slice(qb * blk, (qb + 1) * blk)
        q = q_ref[rows, :]
        q4 = jnp.concatenate([q[:, g * HEAD_DIM:(g + 1) * HEAD_DIM] for g in range(group)], axis=0)
        k = k_all[qb * blk:(qb + 3) * blk]
        v = v_all[qb * blk:(qb + 3) * blk]
        s = lax.dot_general(q4, k, (((1,), (1,)), ((), ())), preferred_element_type=F32)
        s = s * (HEAD_DIM ** -0.5)
        k_pos = (i * ATTN_BLOCKS + qb - 1) * blk + col
        valid = (dist <= WINDOW) & (k_pos >= 0) & (k_pos < seq)
        outs = []
        for g in range(group):
            sg = s[g * blk:(g + 1) * blk] - sink_ref[KV_HEADS + h, g] * distf
            sg = jnp.where(valid, sg, NEG)
            sink = sink_ref[h, g]
            m = jnp.maximum(jnp.max(sg, axis=-1, keepdims=True), sink)
            e = jnp.exp(sg - m)
            denom = jnp.sum(e, axis=-1, keepdims=True) + jnp.exp(sink - m)
            pg = (e / denom).astype(v.dtype)
            outs.append(_dot(pg, v))
        o_ref[rows, :] = jnp.concatenate(outs, axis=1).astype(o_ref.dtype)


def _attention(z, sink, seq, width, q_off, k_off, v_off):
    group = width // HEAD_DIM // KV_HEADS
    gw = group * HEAD_DIM
    ab = ATTN_BLOCKS
    rows = ab * WINDOW
    nb = seq // WINDOW
    qb, kb, vb = q_off // gw, k_off // HEAD_DIM, v_off // HEAD_DIM
    assert q_off % gw == 0 and k_off % HEAD_DIM == 0 and v_off % HEAD_DIM == 0 and seq % rows == 0
    halo = lambda base, f: pl.BlockSpec((WINDOW, HEAD_DIM), lambda i, h: (f(i), base + h))
    prev = lambda i: jnp.maximum(i * ab - 1, 0)
    nxt = lambda i: jnp.minimum((i + 1) * ab, nb - 1)
    main = lambda base: pl.BlockSpec((rows, HEAD_DIM), lambda i, h: (i, base + h))
    n_q = KV_HEADS * group
    slopes = 2.0 ** (-8.0 * (jnp.arange(n_q, dtype=F32) + 1.0) / n_q)
    scalars = jnp.concatenate([sink.astype(F32), slopes]).reshape(2 * KV_HEADS, group)
    return pl.pallas_call(
        functools.partial(_attn_kernel, seq=seq, group=group),
        grid=(seq // rows, KV_HEADS),
        in_specs=[pl.BlockSpec((rows, gw), lambda i, h: (i, qb + h)),
                  halo(kb, prev), main(kb), halo(kb, nxt),
                  halo(vb, prev), main(vb), halo(vb, nxt),
                  pl.BlockSpec(memory_space=pltpu.SMEM)],
        out_specs=pl.BlockSpec((rows, gw), lambda i, h: (i, h)),
        out_shape=jax.ShapeDtypeStruct((seq, width), BF16),
        compiler_params=_cparams("parallel", "parallel"),
        name="window_attention",
    )(z, z, z, z, z, z, z, scalars)


def _retention_kernel(*refs, heads, reverse, final, cps):
    if final:
        cd_ref, dec_ref, kw_ref, qw_ref, q_ref, k_ref, v_ref, prev_ref, g_ref, o_ref, s_ref = refs
    else:
        cd_ref, dec_ref, kw_ref, qw_ref, q_ref, k_ref, v_ref, o_ref, s_ref = refs
    hg = pl.program_id(0)
    n = pl.program_id(1)
    c = RET_CHUNK

    @pl.when(n == 0)
    def _():
        s_ref[...] = jnp.zeros_like(s_ref)

    chunk_order = range(cps - 1, -1, -1) if reverse else range(cps)
    for hd in range(heads):
        sl = slice(hd * HEAD_DIM, (hd + 1) * HEAD_DIM)
        decay, k_w, q_w = dec_ref[hd], kw_ref[hd], qw_ref[hd]
        chunk_decay = cd_ref[hg * heads + hd]
        state = s_ref[hd]
        for cc in chunk_order:
            rows = slice(cc * c, (cc + 1) * c)
            q, k, v = q_ref[rows, sl], k_ref[rows, sl], v_ref[rows, sl]
            qk = lax.dot_general(q, k, (((1,), (1,)), ((), ())), preferred_element_type=F32)
            inner = _dot((qk * decay).astype(v.dtype), v)
            cross = _dot((q.astype(F32) * q_w).astype(BF16), state.astype(BF16))
            kv = lax.dot_general((k.astype(F32) * k_w).astype(BF16), v, (((0,), (0,)), ((), ())),
                                 preferred_element_type=F32)
            state = state * chunk_decay + kv
            o = inner + cross
            if final:
                o = o + prev_ref[rows, sl]
                o = o * lax.rsqrt(jnp.mean(o * o, axis=-1, keepdims=True) + EPS)
                g = g_ref[rows, sl].astype(F32)
                o = g * _sigmoid(g) * o
            o_ref[rows, sl] = o.astype(o_ref.dtype)
        s_ref[hd] = state


def _retention_pass(z, seq, width, offs, reverse, prev=None):
    c = RET_CHUNK
    cps = RET_CHUNKS_PER_STEP
    rows = cps * c
    nb = seq // rows
    gw = math.gcd(width, *offs)
    assert gw % HEAD_DIM == 0 and seq % rows == 0
    heads = gw // HEAD_DIM
    q_off, k_off, v_off, g_off = offs
    hidx = jnp.arange(width // HEAD_DIM, dtype=F32)
    idx = jnp.arange(c, dtype=F32)
    rel = idx[:, None] - idx[None, :]
    scale = HEAD_DIM ** -0.5
    if reverse:
        lg = jnp.log(1.0 - 2.0 ** (-5.5 - hidx))[:, None]
        order = lambda n: nb - 1 - n
        rel, k_pow, q_pow = -rel, idx, c - idx
    else:
        lg = jnp.log(1.0 - 2.0 ** (-5.0 - hidx))[:, None]
        order = lambda n: n
        k_pow, q_pow = c - 1.0 - idx, idx + 1.0
    decay = jnp.where(rel >= 0, jnp.exp(jnp.maximum(rel, 0.0)[None] * lg[:, :, None]), 0.0) * scale
    lanes = lambda a: jnp.broadcast_to(a[:, :, None], a.shape + (HEAD_DIM,))
    k_w = lanes(jnp.exp(k_pow[None] * lg) * scale)
    q_w = lanes(jnp.exp(q_pow[None] * lg))
    chunk_decay = jnp.exp(c * lg[:, 0])
    final = prev is not None
    zspec = lambda off: pl.BlockSpec((rows, gw), lambda g, n: (order(n), off // gw + g))
    own = pl.BlockSpec((rows, gw), lambda g, n: (order(n), g))
    table = pl.BlockSpec((heads, c, HEAD_DIM), lambda g, n: (g, 0, 0))
    in_specs = [pl.BlockSpec(memory_space=pltpu.SMEM), table, table, table,
                zspec(q_off), zspec(k_off), zspec(v_off)]
    args = [chunk_decay, decay, k_w, q_w, z, z, z]
    if final:
        in_specs += [own, zspec(g_off)]
        args += [prev, z]
    return pl.pallas_call(
        functools.partial(_retention_kernel, heads=heads, reverse=reverse, final=final, cps=cps),
        grid=(width // gw, nb),
        in_specs=in_specs,
        out_specs=own,
        out_shape=jax.ShapeDtypeStruct((seq, width), BF16 if final else F32),
        scratch_shapes=[pltpu.VMEM((heads, HEAD_DIM, HEAD_DIM), F32)],
        compiler_params=_cparams("parallel", "arbitrary"),
        name="retention_fwd" if final else "retention_bwd",
    )(*args)


def _retention(z, seq, width, offs):
    o_bwd = _retention_pass(z, seq, width, offs, reverse=True)
    return _retention_pass(z, seq, width, offs, reverse=False, prev=o_bwd)


def _merge_kernel(ya_ref, yb_ref, yc_ref, wa_ref, wb_ref, wc_ref, ga_ref, gb_ref, gc_ref, o_ref):
    acc = None
    for y_ref, w_ref, g_ref in ((ya_ref, wa_ref, ga_ref), (yb_ref, wb_ref, gb_ref), (yc_ref, wc_ref, gc_ref)):
        t = _sigmoid(g_ref[...].astype(F32)) * _dot(y_ref[...], w_ref[...])
        acc = t if acc is None else acc + t
    o_ref[...] = acc.astype(o_ref.dtype)


def _merge(ya, yb, yc, w_branch, layer, z, gate_off, d_model):
    seq, width = ya.shape
    tm, tn = _tile(seq, 1024), _tile(d_model, 512)
    assert gate_off % tn == 0
    ys = pl.BlockSpec((tm, width), lambda i, j: (i, 0))
    ws = lambda b: pl.BlockSpec((None, None, width, tn), lambda i, j: (layer, b, 0, j))
    gs = lambda b: pl.BlockSpec((tm, tn), lambda i, j: (i, (gate_off + b * d_model) // tn + j))
    return pl.pallas_call(
        _merge_kernel,
        grid=(seq // tm, d_model // tn),
        in_specs=[ys, ys, ys, ws(0), ws(1), ws(2), gs(0), gs(1), gs(2)],
        out_specs=pl.BlockSpec((tm, tn), lambda i, j: (i, j)),
        out_shape=jax.ShapeDtypeStruct((seq, d_model), BF16),
        compiler_params=_cparams("parallel", "parallel"),
        name="branch_merge",
    )(ya, yb, yc, w_branch, w_branch, w_branch, z, z, z)


def _ffn_in_kernel(h_ref, hp_ref, hn_ref, s_ref, sp_ref, sn_ref, wg_ref, wu_ref, cw_ref, cb_ref, o_ref,
                   *, n_tiles, d):
    i = pl.program_id(0)
    first = (i > 0).astype(F32)
    last = (i < n_tiles - 1).astype(F32)
    r = _row_scale(s_ref[...], d)
    wg = wg_ref[...]
    g = _dot(h_ref[...], wg) * r
    prev_row = _dot(hp_ref[...], wg)[HALO - 1:HALO] * (_row_scale(sp_ref[...], d)[SUBLANES - 1:SUBLANES] * first)
    next_row = _dot(hn_ref[...], wg)[0:1] * (_row_scale(sn_ref[...], d)[0:1] * last)
    down, up = _shift_rows(g, prev_row, next_row)
    cw = cw_ref[...]
    c = down * cw[0:1] + g * cw[1:2] + up * cw[2:3] + cb_ref[...]
    gelu = 0.5 * c * (1.0 + lax.erf(c * (2.0 ** -0.5)))
    o_ref[...] = (gelu * (_dot(h_ref[...], wu_ref[...]) * r)).astype(o_ref.dtype)


def _ffn_in(h, ssq, w_gate, w_up, conv_w, conv_b, layer, ffp):
    seq, d = h.shape
    tm, tn = _tile(seq, 1024), _tile(ffp, 512)
    n_tiles, n_halo, per = seq // tm, seq // HALO, tm // HALO
    n_sub, per_sub = seq // SUBLANES, tm // SUBLANES
    nj = ffp // tn
    return pl.pallas_call(
        functools.partial(_ffn_in_kernel, n_tiles=n_tiles, d=d),
        grid=(n_tiles, nj),
        in_specs=[
            pl.BlockSpec((tm, d), lambda i, j: (i, 0)),
            pl.BlockSpec((HALO, d), lambda i, j: (jnp.maximum(i * per - 1, 0), 0)),
            pl.BlockSpec((HALO, d), lambda i, j: (jnp.minimum((i + 1) * per, n_halo - 1), 0)),
            pl.BlockSpec((tm, LANES), lambda i, j: (i, 0)),
            pl.BlockSpec((SUBLANES, LANES), lambda i, j: (jnp.maximum(i * per_sub - 1, 0), 0)),
            pl.BlockSpec((SUBLANES, LANES), lambda i, j: (jnp.minimum((i + 1) * per_sub, n_sub - 1), 0)),
            pl.BlockSpec((None, d, tn), lambda i, j: (layer, 0, j)),
            pl.BlockSpec((None, d, tn), lambda i, j: (layer, 0, j)),
            pl.BlockSpec((None, 3, tn), lambda i, j: (layer, 0, j)),
            pl.BlockSpec((None, 1, tn), lambda i, j: (layer, 0, j)),
        ],
        out_specs=pl.BlockSpec((tm, tn), lambda i, j: (i, j)),
        out_shape=jax.ShapeDtypeStruct((seq, ffp), BF16),
        compiler_params=_cparams("parallel", "parallel"),
        name="ffn_in",
    )(h, h, h, ssq, ssq, ssq, w_gate, w_up, conv_w, conv_b)


MATMUL_WEIGHTS = ("w_in", "w_branch", "w_out", "w_ffn_in", "w_ffn_gate", "w_ffn_up", "w_ffn_out",
                  "ffn_conv_w", "ffn_conv_b")


def _prepare_weights(w):
    ff = w["w_ffn_out"].shape[1]
    ffp = -(-ff // FF_PAD) * FF_PAD
    pad = ffp - ff
    last = lambda a: jnp.pad(a, ((0, 0),) * (a.ndim - 1) + ((0, pad),))
    w_ffn_in = w["w_ffn_in"]
    return dict(
        w_in=w["w_in"].astype(BF16),
        w_branch=w["w_branch"].astype(BF16),
        w_out=w["w_out"].astype(BF16),
        w_ffn_gate=last(w_ffn_in[..., :ff].astype(BF16)),
        w_ffn_up=last(w_ffn_in[..., ff:].astype(BF16)),
        w_ffn_out=jnp.pad(w["w_ffn_out"].astype(BF16), ((0, 0), (0, pad), (0, 0))),
        ffn_conv_w=last(w["ffn_conv_w"].astype(F32)),
        ffn_conv_b=last(w["ffn_conv_b"].astype(F32))[:, None, :],
        ffp=ffp,
    )


def _layer(x, l, p, tables, normed, next_gain):
    seq, d = x.shape
    width = d // 4
    kvw = KV_HEADS * HEAD_DIM
    aq, r_offs = 3 * width, [4 * width, 5 * width, 6 * width, 7 * width]
    ak, av, gates = 8 * width, 8 * width + kvw, 8 * width + 2 * kvw
    tn = 2 * kvw
    assert (4 * width) % tn == 0
    kv_old, kv_new, r_end = 4 * width // tn, 8 * width // tn, (8 * width + 2 * kvw) // tn

    def z_tile(j):
        return jnp.where(j < kv_old, j, jnp.where(j == kv_old, kv_new, jnp.where(j < r_end, j - 1, j)))

    small = {name: arr[l] for name, arr in p.items() if name not in MATMUL_WEIGHTS and name != "ffp"}
    if normed is None:
        h, ssq = _rmsnorm(x, small["norm_mix"], BF16), None
    else:
        h, ssq = normed
    z = _matmul(h, p["w_in"], l, tm=1024, tn=tn, out_dtype=BF16, ssq=ssq, out_col=z_tile, name="in_proj")
    y_a = _hyena_mixer(z, small, tables, seq, width)
    y_b = _attention(z, small["attn_sink"], seq, width, aq, ak, av)
    y_c = _retention(z, seq, width, r_offs)
    merged = _merge(y_a, y_b, y_c, p["w_branch"], l, z, gates, d)
    x, h, ssq = _matmul(merged, p["w_out"], l, tm=1024, tn=512, out_dtype=F32, residual=x,
                        next_gain=small["norm_ffn"], name="out_proj")
    act = _ffn_in(h, ssq, p["w_ffn_gate"], p["w_ffn_up"], p["ffn_conv_w"], p["ffn_conv_b"], l, p["ffp"])
    ffn_out = functools.partial(_matmul, act, p["w_ffn_out"], l, tm=1024, tn=512, tk=p["ffp"] // 2,
                                out_dtype=F32, residual=x, name="ffn_out")
    if next_gain is None:
        return ffn_out(), None
    x, h, ssq = ffn_out(next_gain=next_gain)
    return x, (h, ssq)


def _trunk(x, p, norm_final):
    b, seq, d = x.shape
    assert b == 1
    tables = _fft_tables(seq)
    x = x.reshape(seq, d)
    depth = p["w_in"].shape[0]
    normed = None
    for l in range(depth):
        next_gain = p["norm_mix"][l + 1] if l + 1 < depth else None
        x, normed = _layer(x, l, p, tables, normed, next_gain)
    return _rmsnorm(x, norm_final, F32).reshape(b, seq, d)


def kernel(x_prompt, x_sample, norm_mix, w_in, hy_conv_w, hy_conv_b, hy_filt_w1, hy_filt_b1, hy_filt_w2, hy_filt_b2, hy_filt_w3, hy_filt_freq, hy_skip, attn_sink, w_branch, w_out, norm_ffn, w_ffn_in, ffn_conv_w, ffn_conv_b, w_ffn_out, norm_final):
    w = dict(norm_mix=norm_mix, w_in=w_in, hy_conv_w=hy_conv_w, hy_conv_b=hy_conv_b,
             hy_filt_w1=hy_filt_w1, hy_filt_b1=hy_filt_b1, hy_filt_w2=hy_filt_w2,
             hy_filt_b2=hy_filt_b2, hy_filt_w3=hy_filt_w3, hy_filt_freq=hy_filt_freq,
             hy_skip=hy_skip, attn_sink=attn_sink, w_branch=w_branch, w_out=w_out,
             norm_ffn=norm_ffn, w_ffn_in=w_ffn_in, ffn_conv_w=ffn_conv_w,
             ffn_conv_b=ffn_conv_b, w_ffn_out=w_ffn_out)
    p = dict(w)
    p.update(_prepare_weights(w))
    return (_trunk(x_prompt, p, norm_final), _trunk(x_sample, p, norm_final))
```

```python
import functools
import math

import jax
import jax.numpy as jnp
from jax import lax
from jax.experimental import pallas as pl
from jax.experimental.pallas import tpu as pltpu

F32 = jnp.float32
BF16 = jnp.bfloat16

LANES = 128
SUBLANES = 8
EPS = 1e-6
NEG = -1e30
HEAD_DIM = 128
KV_HEADS = 2
WINDOW = 128
ATTN_BLOCKS = 4
RET_CHUNK = 128
RET_CHUNKS_PER_STEP = 2
HY_ORDER = 2
HY_EMB = 33
HY_BANDS = (HY_EMB - 1) // 2
HY_MIN_DECAY = math.log(1e-2) / 1.5
HY_MAX_DECAY = math.log(1e-2) / 0.3
DFT_INNER = 128
FFT_CB = 16
FFT_PASSES = 1
FEAT_LANES = 128
HALO = 16
FFN_TILE = 512
VMEM_LIMIT = 56 * 1024 * 1024


def _cparams(*sem, vmem=VMEM_LIMIT):
    return pltpu.CompilerParams(dimension_semantics=sem, vmem_limit_bytes=vmem)


def _tile(n, pref):
    if n <= pref:
        return n
    t = (pref // 128) * 128
    while t >= 128:
        if n % t == 0:
            return t
        t -= 128
    return n


def _dot(a, b):
    return jnp.dot(a, b, preferred_element_type=F32)


def _sigmoid(x):
    return 0.5 * jnp.tanh(0.5 * x) + 0.5


def _split(x):
    hi = x.astype(BF16)
    lo = (x - hi.astype(F32)).astype(BF16)
    return hi, lo


def _dot3(a, b):
    a_hi, a_lo = _split(a)
    b_hi, b_lo = _split(b)
    return _dot(a_hi, b_hi) + (_dot(a_lo, b_hi) + _dot(a_hi, b_lo))


def _rmsnorm_kernel(x_ref, g_ref, o_ref):
    x = x_ref[...]
    ms = jnp.mean(x * x, axis=-1, keepdims=True)
    o_ref[...] = (x * lax.rsqrt(ms + EPS) * g_ref[...]).astype(o_ref.dtype)


def _rmsnorm(x, g, out_dtype):
    m, d = x.shape
    tm = _tile(m, 256)
    return pl.pallas_call(
        _rmsnorm_kernel,
        grid=(m // tm,),
        in_specs=[pl.BlockSpec((tm, d), lambda i: (i, 0)),
                  pl.BlockSpec((1, d), lambda i: (0, 0))],
        out_specs=pl.BlockSpec((tm, d), lambda i: (i, 0)),
        out_shape=jax.ShapeDtypeStruct((m, d), out_dtype),
        compiler_params=_cparams("parallel"),
        name="rmsnorm",
    )(x, g.reshape(1, d).astype(F32))


def _row_scale(ssq, d):
    return lax.rsqrt(ssq[:, 0:1] * (1.0 / d) + EPS)


def _matmul_kernel(*refs, nk, has_res, in_scale_d, norm_out):
    refs = list(refs)
    a_ref, b_ref = refs[:2]
    pos = 2
    s_ref = r_ref = gn_ref = xg_ref = so_ref = None
    if in_scale_d:
        s_ref, pos = refs[pos], pos + 1
    if has_res:
        r_ref, pos = refs[pos], pos + 1
    if norm_out:
        gn_ref, pos = refs[pos], pos + 1
    o_ref, pos = refs[pos], pos + 1
    if norm_out:
        xg_ref, so_ref = refs[pos], refs[pos + 1]
    p = _dot(a_ref[...], b_ref[...])
    j = pl.program_id(1)

    def finish(acc):
        if in_scale_d:
            acc = acc * _row_scale(s_ref[...], in_scale_d)
        if has_res:
            acc = acc + r_ref[...]
        o_ref[...] = acc.astype(o_ref.dtype)
        if norm_out:
            xg_ref[...] = (acc * gn_ref[...]).astype(xg_ref.dtype)
            part = jnp.broadcast_to(jnp.sum(acc * acc, axis=1, keepdims=True), so_ref.shape)

            @pl.when(j == 0)
            def _():
                so_ref[...] = part

            @pl.when(j > 0)
            def _():
                so_ref[...] += part

    if nk == 1:
        finish(p)
        return
    acc_ref = refs[-1]
    k = pl.program_id(2)

    @pl.when(k == 0)
    def _():
        acc_ref[...] = p

    @pl.when(k > 0)
    def _():
        acc_ref[...] += p

    @pl.when(k == nk - 1)
    def _():
        finish(acc_ref[...])


def _matmul(a, b, layer, *, tm, tn, tk=None, out_dtype, ssq=None, residual=None, next_gain=None,
            out_col=lambda j: j, name):
    m, kdim = a.shape
    n = b.shape[2]
    tm, tn = _tile(m, tm), _tile(n, tn)
    tk = kdim if tk is None else _tile(kdim, tk)
    nk = kdim // tk
    in_specs = [pl.BlockSpec((tm, tk), lambda i, j, k: (i, k)),
                pl.BlockSpec((None, tk, tn), lambda i, j, k: (layer, k, j))]
    args = [a, b]
    row_stat = pl.BlockSpec((tm, LANES), lambda i, j, k: (i, 0))
    tile = pl.BlockSpec((tm, tn), lambda i, j, k: (i, j))
    if ssq is not None:
        in_specs.append(row_stat)
        args.append(ssq)
    if residual is not None:
        in_specs.append(tile)
        args.append(residual)
    out_specs = pl.BlockSpec((tm, tn), lambda i, j, k: (i, out_col(j)))
    out_shape = jax.ShapeDtypeStruct((m, n), out_dtype)
    if next_gain is not None:
        in_specs.append(pl.BlockSpec((1, tn), lambda i, j, k: (0, j)))
        args.append(next_gain.reshape(1, n).astype(F32))
        out_specs = [out_specs, tile, row_stat]
        out_shape = [out_shape, jax.ShapeDtypeStruct((m, n), BF16), jax.ShapeDtypeStruct((m, LANES), F32)]
    scratch = [pltpu.VMEM((tm, tn), F32)] if nk > 1 else []
    return pl.pallas_call(
        functools.partial(_matmul_kernel, nk=nk, has_res=residual is not None,
                          in_scale_d=kdim if ssq is not None else 0, norm_out=next_gain is not None),
        grid=(m // tm, n // tn, nk),
        in_specs=in_specs,
        out_specs=out_specs,
        out_shape=out_shape,
        scratch_shapes=scratch,
        compiler_params=_cparams("parallel", "arbitrary", "arbitrary"),
        name=name,
    )(*args)


def _shift_rows(x, prev_row, next_row):
    tm = x.shape[0]
    row = lax.broadcasted_iota(jnp.int32, x.shape, 0)
    down = jnp.where(row == 0, prev_row, pltpu.roll(x, 1, axis=0))
    up = jnp.where(row == tm - 1, next_row, pltpu.roll(x, tm - 1, axis=0))
    return down, up


def _hyconv_kernel(x_ref, xp_ref, xn_ref, w_ref, b_ref, o_ref, *, n_tiles):
    i = pl.program_id(0)
    x = x_ref[...].astype(F32)
    prev_row = xp_ref[...].astype(F32)[HALO - 1:HALO] * (i > 0).astype(F32)
    next_row = xn_ref[...].astype(F32)[0:1] * (i < n_tiles - 1).astype(F32)
    down, up = _shift_rows(x, prev_row, next_row)
    w = w_ref[...]
    o_ref[...] = (down * w[0:1] + x * w[1:2] + up * w[2:3] + b_ref[...]).astype(o_ref.dtype)


def _hyena_conv(z, conv_w, conv_b, seq, width):
    tm = _tile(seq, 512)
    n_tiles = seq // tm
    n_halo = seq // HALO
    per = tm // HALO
    return pl.pallas_call(
        functools.partial(_hyconv_kernel, n_tiles=n_tiles),
        grid=(n_tiles, 3),
        in_specs=[
            pl.BlockSpec((tm, width), lambda i, p: (i, p)),
            pl.BlockSpec((HALO, width), lambda i, p: (jnp.maximum(i * per - 1, 0), p)),
            pl.BlockSpec((HALO, width), lambda i, p: (jnp.minimum((i + 1) * per, n_halo - 1), p)),
            pl.BlockSpec((3, width), lambda i, p: (0, p)),
            pl.BlockSpec((1, width), lambda i, p: (0, p)),
        ],
        out_specs=pl.BlockSpec((None, tm, width), lambda i, p: (p, i, 0)),
        out_shape=jax.ShapeDtypeStruct((3, seq, width), BF16),
        compiler_params=_cparams("parallel", "parallel"),
        name="hyena_conv3",
    )(z, z, z, conv_w.astype(F32), conv_b.reshape(1, -1).astype(F32))


def _filter_kernel(bands_ref, w1_ref, b1_ref, w2_ref, b2_ref, freq_ref, w3_ref, delta_ref,
                   w3b_ref, k_ref, sum_ref, *, seq, tr, width):
    i = pl.program_id(0)
    n = 2 * seq
    r = i * tr + lax.broadcasted_iota(jnp.int32, (tr, FEAT_LANES), 0)
    pos = jnp.where(r < seq, r, n - r).astype(F32)
    t = pos / float(seq - 1)
    lane = lax.broadcasted_iota(jnp.int32, (tr, FEAT_LANES), 1)
    ang = (2.0 * math.pi / seq) * pos * bands_ref[...]
    quarter = jnp.where((lane >= HY_BANDS) & (lane < 2 * HY_BANDS), 0.5 * math.pi, 0.0)
    feats = jnp.where(lane < 2 * HY_BANDS, jnp.cos(ang + quarter),
                      jnp.where(lane == 2 * HY_BANDS, t, 0.0))
    freq = freq_ref[...]
    h = jnp.sin(freq * (_dot3(feats, w1_ref[...]) + b1_ref[...]))
    h = jnp.sin(freq * (_dot3(h, w2_ref[...]) + b2_ref[...]))
    h_hi, h_lo = _split(h)
    r1 = i * tr + lax.broadcasted_iota(jnp.int32, (tr, 1), 0)
    t1 = jnp.where(r1 < seq, r1, n - r1).astype(F32) / float(seq - 1)
    window = jnp.where(r1 == seq, 0.0, jnp.exp(-t1 * delta_ref[...]))
    parts = []
    for o, w3 in enumerate((w3_ref[...], w3b_ref[...])):
        w_hi, w_lo = _split(w3)
        kf = (_dot(h_hi, w_hi) + (_dot(h_lo, w_hi) + _dot(h_hi, w_lo))) * window
        k_ref[:, o * width:(o + 1) * width] = kf.astype(k_ref.dtype)
        parts.append(jnp.sum(jnp.abs(kf), axis=0, keepdims=True))
    part = jnp.concatenate(parts, axis=1)

    @pl.when(i == 0)
    def _():
        sum_ref[...] = part

    @pl.when(i > 0)
    def _():
        sum_ref[...] += part


def _hyena_filters(seq, width, w1, b1, w2, b2, w3, freq):
    n = 2 * seq
    tr = _tile(seq, 512)
    nb = n // tr
    fw = w1.shape[1]
    bands = jnp.linspace(1e-4, HY_BANDS - 1, HY_BANDS, dtype=F32)
    bands = jnp.concatenate([bands, bands, jnp.zeros((FEAT_LANES - 2 * HY_BANDS,), F32)]).reshape(1, -1)
    w1 = w1.astype(F32)
    w1p = jnp.concatenate([w1[1:], w1[:1], jnp.zeros((FEAT_LANES - HY_EMB, fw), F32)], axis=0)
    deltas = jnp.abs(jnp.linspace(HY_MIN_DECAY, HY_MAX_DECAY, width, dtype=F32)).reshape(1, -1)
    row = lambda a: a.reshape(1, -1).astype(F32)
    const = lambda shape: pl.BlockSpec(shape, lambda i: (0, 0))
    w3_spec = lambda o: pl.BlockSpec((fw, width), lambda i: (0, 2 * o + (i >= nb // 2).astype(jnp.int32)))
    assert HY_ORDER == 2
    w3 = w3.astype(F32)
    return pl.pallas_call(
        functools.partial(_filter_kernel, seq=seq, tr=tr, width=width),
        grid=(nb,),
        in_specs=[
            const((1, FEAT_LANES)), const((FEAT_LANES, fw)), const((1, fw)), const((fw, fw)),
            const((1, fw)), const((1, fw)), w3_spec(0), const((1, width)), w3_spec(1),
        ],
        out_specs=[pl.BlockSpec((tr, HY_ORDER * width), lambda i: (i, 0)),
                   pl.BlockSpec((1, HY_ORDER * width), lambda i: (0, 0))],
        out_shape=[jax.ShapeDtypeStruct((n, HY_ORDER * width), BF16),
                   jax.ShapeDtypeStruct((1, HY_ORDER * width), F32)],
        compiler_params=_cparams("arbitrary"),
        name="hyena_filters",
    )(bands, w1p, row(b1), w2.astype(F32), row(b2), row(freq), w3, deltas, w3)


def _fft_tables(seq):
    n = 2 * seq
    n2 = DFT_INNER
    n1 = n // n2
    r = n1 // 2
    k1 = jnp.arange(n1, dtype=jnp.int32)
    th = (2.0 * math.pi / n1) * ((k1[:, None] * k1[None, :]) % n1).astype(F32)
    c, s = jnp.cos(th), jnp.sin(th)
    outer = jnp.concatenate([c, -s], axis=0)
    j = jnp.arange(n2, dtype=jnp.int32)
    ph = (2.0 * math.pi / n) * ((k1[:, None] * j[None, :]) % n).astype(F32)
    th2 = (2.0 * math.pi / n2) * ((j[:, None] * j[None, :]) % n2).astype(F32)
    cr, ci = jnp.cos(th2), -jnp.sin(th2)
    inner = jnp.concatenate([jnp.concatenate([cr, ci], axis=1),
                             jnp.concatenate([-ci, cr], axis=1)], axis=0)
    inv_inner = jnp.concatenate([jnp.concatenate([cr, -ci], axis=1),
                                 jnp.concatenate([ci, cr], axis=1)], axis=0) / n2
    return dict(n1=n1, n2=n2, outer=_split(outer), outer_half=_split(outer[:, :r]),
                inv_outer_re=_split(c[:r] / n1), inv_outer_im=_split(-s[:r] / n1),
                tw_re=jnp.cos(ph), tw_im=-jnp.sin(ph),
                inner=_split(inner), inv_inner=_split(inv_inner))


def _mat_dot(m_hi, m_lo, x, passes):
    x_hi = x.astype(BF16)
    out = _dot(m_hi, x_hi)
    if passes >= 2:
        out = out + _dot(m_hi, (x - x_hi.astype(F32)).astype(BF16))
    if passes >= 3:
        out = out + _dot(m_lo, x_hi)
    return out


def _dot_mat(x, m_hi, m_lo, passes):
    x_hi = x.astype(BF16)
    out = _dot(x_hi, m_hi)
    if passes >= 2:
        out = out + _dot((x - x_hi.astype(F32)).astype(BF16), m_hi)
    if passes >= 3:
        out = out + _dot(x_hi, m_lo)
    return out


def _forward_to_scratch(x_ref, a_scr, f1h, f1l, tw_re, tw_im, n1, cb, passes):
    half = DFT_INNER
    for c in range(cb):
        p = _mat_dot(f1h, f1l, x_ref[c], passes)
        ar, ai = p[:n1], p[n1:]
        a_scr[c, :, :half] = ar * tw_re - ai * tw_im
        a_scr[c, :, half:] = ar * tw_im + ai * tw_re


def _filter_spectrum_kernel(k_ref, inv_ref, f1h_ref, f1l_ref, twr_ref, twi_ref, mh_ref, ml_ref,
                            kr_ref, ki_ref, a_scr, *, n1, cb):
    half = DFT_INNER
    _forward_to_scratch(k_ref, a_scr, f1h_ref[...], f1l_ref[...], twr_ref[...], twi_ref[...], n1, cb, FFT_PASSES)
    x = _dot_mat(a_scr[...].reshape(cb * n1, 2 * half), mh_ref[...], ml_ref[...], FFT_PASSES)
    x = x.reshape(cb, n1, 2 * half)
    inv = inv_ref[...]
    kr_ref[...] = x[:, :, :half] * inv
    ki_ref[...] = x[:, :, half:] * inv


def _const_spec(arr):
    return pl.BlockSpec(arr.shape, lambda j: (0,) * arr.ndim)


def _filter_spectrum(kf_t, inv_sum, tables):
    ch, n1, n2 = kf_t.shape
    cb = FFT_CB
    f1h, f1l = tables["outer"]
    mh, ml = tables["inner"]
    consts = [f1h, f1l, tables["tw_re"], tables["tw_im"], mh, ml]
    blk = pl.BlockSpec((cb, n1, n2), lambda j: (j, 0, 0))
    out = jax.ShapeDtypeStruct((ch, n1, n2), F32)
    return pl.pallas_call(
        functools.partial(_filter_spectrum_kernel, n1=n1, cb=cb),
        grid=(ch // cb,),
        in_specs=[blk, pl.BlockSpec((cb, 1, n2), lambda j: (j, 0, 0))] + [_const_spec(a) for a in consts],
        out_specs=[blk, blk],
        out_shape=[out, out],
        scratch_shapes=[pltpu.VMEM((cb, n1, 2 * n2), F32)],
        compiler_params=_cparams("parallel"),
        name="filter_spectrum",
    )(kf_t, inv_sum, *consts)


def _fftconv_kernel(u_ref, gate_ref, kr_ref, ki_ref, skip_ref, f1h_ref, f1l_ref, twr_ref, twi_ref,
                    mh_ref, ml_ref, nh_ref, nl_ref, grh_ref, grl_ref, gih_ref, gil_ref,
                    o_ref, a_scr, *, n1, cb, passes):
    half = DFT_INNER
    tw_re, tw_im = twr_ref[...], twi_ref[...]
    _forward_to_scratch(u_ref, a_scr, f1h_ref[...], f1l_ref[...], tw_re, tw_im, n1, cb, passes)
    x = _dot_mat(a_scr[...].reshape(cb * n1, 2 * half), mh_ref[...], ml_ref[...], passes)
    xr, xi = x[:, :half], x[:, half:]
    kr = kr_ref[...].reshape(cb * n1, half)
    ki = ki_ref[...].reshape(cb * n1, half)
    y = jnp.concatenate([xr * kr - xi * ki, xr * ki + xi * kr], axis=1)
    b = _dot_mat(y, nh_ref[...], nl_ref[...], passes)
    a_scr[...] = b.reshape(cb, n1, 2 * half)
    grh, grl, gih, gil = grh_ref[...], grl_ref[...], gih_ref[...], gil_ref[...]
    for c in range(cb):
        br, bi = a_scr[c, :, :half], a_scr[c, :, half:]
        conv = (_mat_dot(grh, grl, br * tw_re + bi * tw_im, passes)
                + _mat_dot(gih, gil, bi * tw_re - br * tw_im, passes))
        o_ref[c] = (gate_ref[c] * (conv + u_ref[c] * skip_ref[c])).astype(o_ref.dtype)


def _fftconv(u_t, u_part, gate_t, gate_part, kr, ki, order, skip_t, tables, width, out_dtype):
    n1, n2 = tables["n1"], tables["n2"]
    rows = n1 // 2
    cb = FFT_CB
    nb = width // cb
    consts = [*tables["outer_half"], tables["tw_re"], tables["tw_im"], *tables["inner"],
              *tables["inv_inner"], *tables["inv_outer_re"], *tables["inv_outer_im"]]
    data = lambda part: pl.BlockSpec((cb, rows, n2), lambda j: (part * nb + j, 0, 0))
    spec = pl.BlockSpec((cb, n1, n2), lambda j: (order * nb + j, 0, 0))
    return pl.pallas_call(
        functools.partial(_fftconv_kernel, n1=n1, cb=cb, passes=FFT_PASSES),
        grid=(nb,),
        in_specs=[data(u_part), data(gate_part), spec, spec,
                  pl.BlockSpec((cb, 1, n2), lambda j: (j, 0, 0))] + [_const_spec(a) for a in consts],
        out_specs=pl.BlockSpec((cb, rows, n2), lambda j: (j, 0, 0)),
        out_shape=jax.ShapeDtypeStruct((width, rows, n2), out_dtype),
        scratch_shapes=[pltpu.VMEM((cb, n1, 2 * n2), F32)],
        compiler_params=_cparams("parallel"),
        name="hyena_fftconv",
    )(u_t, gate_t, kr, ki, skip_t, *consts)


def _lane_rows(v):
    return jnp.broadcast_to(v.astype(F32).reshape(-1, 1, 1), (v.shape[0], 1, DFT_INNER))


def _hyena_mixer(z, p, tables, seq, width):
    n1, n2 = tables["n1"], tables["n2"]
    rows = n1 // 2
    zc = _hyena_conv(z, p["hy_conv_w"], p["hy_conv_b"], seq, width)
    zc_t = jnp.transpose(zc, (0, 2, 1)).reshape(3 * width, rows, n2)
    kf, colsum = _hyena_filters(seq, width, p["hy_filt_w1"], p["hy_filt_b1"], p["hy_filt_w2"],
                                p["hy_filt_b2"], p["hy_filt_w3"], p["hy_filt_freq"])
    kf_t = kf.T.reshape(HY_ORDER * width, n1, n2)
    kr, ki = _filter_spectrum(kf_t, _lane_rows(1.0 / colsum.reshape(-1)), tables)
    skip = p["hy_skip"]
    y1 = _fftconv(zc_t, 0, zc_t, 1, kr, ki, 0, _lane_rows(skip[0]), tables, width, F32)
    y = _fftconv(y1, 0, zc_t, 2, kr, ki, 1, _lane_rows(skip[1]), tables, width, BF16)
    return y.reshape(width, seq).T


def _attn_kernel(q_ref, kp_ref, kc_ref, kn_ref, vp_ref, vc_ref, vn_ref, sink_ref, o_ref, *, seq, group):
    i = pl.program_id(0)
    h = pl.program_id(1)
    blk = WINDOW
    k_all = jnp.concatenate([kp_ref[...], kc_ref[...], kn_ref[...]], axis=0)
    v_all = jnp.concatenate([vp_ref[...], vc_ref[...], vn_ref[...]], axis=0)
    row = lax.broadcasted_iota(jnp.int32, (blk, 3 * blk), 0)
    col = lax.broadcasted_iota(jnp.int32, (blk, 3 * blk), 1)
    dist = jnp.abs(row + blk - col)
    distf = dist.astype(F32)
    for qb in range(ATTN_BLOCKS):
        rows = slice(qb * blk, (qb + 1) * blk)
        q = q_ref[rows, :]
        q4 = jnp.concatenate([q[:, g * HEAD_DIM:(g + 1) * HEAD_DIM] for g in range(group)], axis=0)
        k = k_all[qb * blk:(qb + 3) * blk]
        v = v_all[qb * blk:(qb + 3) * blk]
        s = lax.dot_general(q4, k, (((1,), (1,)), ((), ())), preferred_element_type=F32)
        s = s * (HEAD_DIM ** -0.5)
        k_pos = (i * ATTN_BLOCKS + qb - 1) * blk + col
        valid = (dist <= WINDOW) & (k_pos >= 0) & (k_pos < seq)
        outs = []
        for g in range(group):
            sg = s[g * blk:(g + 1) * blk] - sink_ref[KV_HEADS + h, g] * distf
            sg = jnp.where(valid, sg, NEG)
            sink = sink_ref[h, g]
            m = jnp.maximum(jnp.max(sg, axis=-1, keepdims=True), sink)
            e = jnp.exp(sg - m)
            denom = jnp.sum(e, axis=-1, keepdims=True) + jnp.exp(sink - m)
            pg = (e / denom).astype(v.dtype)
            outs.append(_dot(pg, v))
        o_ref[rows, :] = jnp.concatenate(outs, axis=1).astype(o_ref.dtype)


def _attention(z, sink, seq, width, q_off, k_off, v_off):
    group = width // HEAD_DIM // KV_HEADS
    gw = group * HEAD_DIM
    ab = ATTN_BLOCKS
    rows = ab * WINDOW
    nb = seq // WINDOW
    qb, kb, vb = q_off // gw, k_off // HEAD_DIM, v_off // HEAD_DIM
    assert q_off % gw == 0 and k_off % HEAD_DIM == 0 and v_off % HEAD_DIM == 0 and seq % rows == 0
    halo = lambda base, f: pl.BlockSpec((WINDOW, HEAD_DIM), lambda i, h: (f(i), base + h))
    prev = lambda i: jnp.maximum(i * ab - 1, 0)
    nxt = lambda i: jnp.minimum((i + 1) * ab, nb - 1)
    main = lambda base: pl.BlockSpec((rows, HEAD_DIM), lambda i, h: (i, base + h))
    n_q = KV_HEADS * group
    slopes = 2.0 ** (-8.0 * (jnp.arange(n_q, dtype=F32) + 1.0) / n_q)
    scalars = jnp.concatenate([sink.astype(F32), slopes]).reshape(2 * KV_HEADS, group)
    return pl.pallas_call(
        functools.partial(_attn_kernel, seq=seq, group=group),
        grid=(seq // rows, KV_HEADS),
        in_specs=[pl.BlockSpec((rows, gw), lambda i, h: (i, qb + h)),
                  halo(kb, prev), main(kb), halo(kb, nxt),
                  halo(vb, prev), main(vb), halo(vb, nxt),
                  pl.BlockSpec(memory_space=pltpu.SMEM)],
        out_specs=pl.BlockSpec((rows, gw), lambda i, h: (i, h)),
        out_shape=jax.ShapeDtypeStruct((seq, width), BF16),
        compiler_params=_cparams("parallel", "parallel"),
        name="window_attention",
    )(z, z, z, z, z, z, z, scalars)


def _retention_kernel(*refs, heads, reverse, final, cps):
    if final:
        cd_ref, dec_ref, kw_ref, qw_ref, q_ref, k_ref, v_ref, prev_ref, g_ref, o_ref, s_ref = refs
    else:
        cd_ref, dec_ref, kw_ref, qw_ref, q_ref, k_ref, v_ref, o_ref, s_ref = refs
    hg = pl.program_id(0)
    n = pl.program_id(1)
    c = RET_CHUNK

    @pl.when(n == 0)
    def _():
        s_ref[...] = jnp.zeros_like(s_ref)

    chunk_order = range(cps - 1, -1, -1) if reverse else range(cps)
    for hd in range(heads):
        sl = slice(hd * HEAD_DIM, (hd + 1) * HEAD_DIM)
        decay, k_w, q_w = dec_ref[hd], kw_ref[hd], qw_ref[hd]
        chunk_decay = cd_ref[hg * heads + hd]
        state = s_ref[hd]
        for cc in chunk_order:
            rows = slice(cc * c, (cc + 1) * c)
            q, k, v = q_ref[rows, sl], k_ref[rows, sl], v_ref[rows, sl]
            qk = lax.dot_general(q, k, (((1,), (1,)), ((), ())), preferred_element_type=F32)
            inner = _dot((qk * decay).astype(v.dtype), v)
            cross = _dot((q.astype(F32) * q_w).astype(BF16), state.astype(BF16))
            kv = lax.dot_general((k.astype(F32) * k_w).astype(BF16), v, (((0,), (0,)), ((), ())),
                                 preferred_element_type=F32)
            state = state * chunk_decay + kv
            o = inner + cross
            if final:
                o = o + prev_ref[rows, sl]
                o = o * lax.rsqrt(jnp.mean(o * o, axis=-1, keepdims=True) + EPS)
                g = g_ref[rows, sl].astype(F32)
                o = g * _sigmoid(g) * o
            o_ref[rows, sl] = o.astype(o_ref.dtype)
        s_ref[hd] = state


def _retention_pass(z, seq, width, offs, reverse, prev=None):
    c = RET_CHUNK
    cps = RET_CHUNKS_PER_STEP
    rows = cps * c
    nb = seq // rows
    gw = math.gcd(width, *offs)
    assert gw % HEAD_DIM == 0 and seq % rows == 0
    heads = gw // HEAD_DIM
    q_off, k_off, v_off, g_off = offs
    hidx = jnp.arange(width // HEAD_DIM, dtype=F32)
    idx = jnp.arange(c, dtype=F32)
    rel = idx[:, None] - idx[None, :]
    scale = HEAD_DIM ** -0.5
    if reverse:
        lg = jnp.log(1.0 - 2.0 ** (-5.5 - hidx))[:, None]
        order = lambda n: nb - 1 - n
        rel, k_pow, q_pow = -rel, idx, c - idx
    else:
        lg = jnp.log(1.0 - 2.0 ** (-5.0 - hidx))[:, None]
        order = lambda n: n
        k_pow, q_pow = c - 1.0 - idx, idx + 1.0
    decay = jnp.where(rel >= 0, jnp.exp(jnp.maximum(rel, 0.0)[None] * lg[:, :, None]), 0.0) * scale
    lanes = lambda a: jnp.broadcast_to(a[:, :, None], a.shape + (HEAD_DIM,))
    k_w = lanes(jnp.exp(k_pow[None] * lg) * scale)
    q_w = lanes(jnp.exp(q_pow[None] * lg))
    chunk_decay = jnp.exp(c * lg[:, 0])
    final = prev is not None
    zspec = lambda off: pl.BlockSpec((rows, gw), lambda g, n: (order(n), off // gw + g))
    own = pl.BlockSpec((rows, gw), lambda g, n: (order(n), g))
    table = pl.BlockSpec((heads, c, HEAD_DIM), lambda g, n: (g, 0, 0))
    in_specs = [pl.BlockSpec(memory_space=pltpu.SMEM), table, table, table,
                zspec(q_off), zspec(k_off), zspec(v_off)]
    args = [chunk_decay, decay, k_w, q_w, z, z, z]
    if final:
        in_specs += [own, zspec(g_off)]
        args += [prev, z]
    return pl.pallas_call(
        functools.partial(_retention_kernel, heads=heads, reverse=reverse, final=final, cps=cps),
        grid=(width // gw, nb),
        in_specs=in_specs,
        out_specs=own,
        out_shape=jax.ShapeDtypeStruct((seq, width), BF16 if final else F32),
        scratch_shapes=[pltpu.VMEM((heads, HEAD_DIM, HEAD_DIM), F32)],
        compiler_params=_cparams("parallel", "arbitrary"),
        name="retention_fwd" if final else "retention_bwd",
    )(*args)


def _retention(z, seq, width, offs):
    o_bwd = _retention_pass(z, seq, width, offs, reverse=True)
    return _retention_pass(z, seq, width, offs, reverse=False, prev=o_bwd)


def _merge_kernel(ya_ref, yb_ref, yc_ref, wa_ref, wb_ref, wc_ref, ga_ref, gb_ref, gc_ref, o_ref):
    acc = None
    for y_ref, w_ref, g_ref in ((ya_ref, wa_ref, ga_ref), (yb_ref, wb_ref, gb_ref), (yc_ref, wc_ref, gc_ref)):
        t = _sigmoid(g_ref[...].astype(F32)) * _dot(y_ref[...], w_ref[...])
        acc = t if acc is None else acc + t
    o_ref[...] = acc.astype(o_ref.dtype)


def _merge(ya, yb, yc, w_branch, layer, z, gate_off, d_model):
    seq, width = ya.shape
    tm, tn = _tile(seq, 1024), _tile(d_model, 512)
    assert gate_off % tn == 0
    ys = pl.BlockSpec((tm, width), lambda i, j: (i, 0))
    ws = lambda b: pl.BlockSpec((None, None, width, tn), lambda i, j: (layer, b, 0, j))
    gs = lambda b: pl.BlockSpec((tm, tn), lambda i, j: (i, (gate_off + b * d_model) // tn + j))
    return pl.pallas_call(
        _merge_kernel,
        grid=(seq // tm, d_model // tn),
        in_specs=[ys, ys, ys, ws(0), ws(1), ws(2), gs(0), gs(1), gs(2)],
        out_specs=pl.BlockSpec((tm, tn), lambda i, j: (i, j)),
        out_shape=jax.ShapeDtypeStruct((seq, d_model), BF16),
        compiler_params=_cparams("parallel", "parallel"),
        name="branch_merge",
    )(ya, yb, yc, w_branch, w_branch, w_branch, z, z, z)


def _ffn_in_kernel(h_ref, hp_ref, hn_ref, s_ref, sp_ref, sn_ref, wg_ref, wu_ref, cw_ref, cb_ref, *rest,
                   n_tiles, d):
    o_ref = rest[-1]
    i = pl.program_id(0)
    first = (i > 0).astype(F32)
    last = (i < n_tiles - 1).astype(F32)
    r = _row_scale(s_ref[...], d)
    wg = wg_ref[...]
    g = _dot(h_ref[...], wg) * r
    prev_row = _dot(hp_ref[...], wg)[HALO - 1:HALO] * (_row_scale(sp_ref[...], d)[SUBLANES - 1:SUBLANES] * first)
    next_row = _dot(hn_ref[...], wg)[0:1] * (_row_scale(sn_ref[...], d)[0:1] * last)
    down, up = _shift_rows(g, prev_row, next_row)
    cw = cw_ref[...]
    c = down * cw[0:1] + g * cw[1:2] + up * cw[2:3] + cb_ref[...]
    gelu = 0.5 * c * (1.0 + lax.erf(c * (2.0 ** -0.5)))
    o_ref[...] = (gelu * (_dot(h_ref[...], wu_ref[...]) * r)).astype(o_ref.dtype)


def _ffn_in_span(h, ssq, w_gate, w_up, conv_w, conv_b, layer, tn, col0, n_col_tiles, prev_out):
    seq, d = h.shape
    ff = w_gate.shape[2]
    tm = _tile(seq, 1024)
    n_tiles, n_halo, per = seq // tm, seq // HALO, tm // HALO
    n_sub, per_sub = seq // SUBLANES, tm // SUBLANES
    assert col0 % tn == 0
    j0 = col0 // tn
    in_specs = [
        pl.BlockSpec((tm, d), lambda i, j: (i, 0)),
        pl.BlockSpec((HALO, d), lambda i, j: (jnp.maximum(i * per - 1, 0), 0)),
        pl.BlockSpec((HALO, d), lambda i, j: (jnp.minimum((i + 1) * per, n_halo - 1), 0)),
        pl.BlockSpec((tm, LANES), lambda i, j: (i, 0)),
        pl.BlockSpec((SUBLANES, LANES), lambda i, j: (jnp.maximum(i * per_sub - 1, 0), 0)),
        pl.BlockSpec((SUBLANES, LANES), lambda i, j: (jnp.minimum((i + 1) * per_sub, n_sub - 1), 0)),
        pl.BlockSpec((None, d, tn), lambda i, j: (layer, 0, j0 + j)),
        pl.BlockSpec((None, d, tn), lambda i, j: (layer, 0, j0 + j)),
        pl.BlockSpec((None, 3, tn), lambda i, j: (layer, 0, j0 + j)),
        pl.BlockSpec((None, 1, tn), lambda i, j: (layer, 0, j0 + j)),
    ]
    args = [h, h, h, ssq, ssq, ssq, w_gate, w_up, conv_w, conv_b]
    aliases = {}
    if prev_out is not None:
        in_specs.append(pl.BlockSpec(memory_space=pl.ANY))
        args.append(prev_out)
        aliases = {len(args) - 1: 0}
    return pl.pallas_call(
        functools.partial(_ffn_in_kernel, n_tiles=n_tiles, d=d),
        grid=(n_tiles, n_col_tiles),
        in_specs=in_specs,
        out_specs=pl.BlockSpec((tm, tn), lambda i, j: (i, j0 + j)),
        out_shape=jax.ShapeDtypeStruct((seq, ff), BF16),
        input_output_aliases=aliases,
        compiler_params=_cparams("parallel", "parallel"),
        name="ffn_in",
    )(*args)


def _ffn_in(h, ssq, w_gate, w_up, conv_w, conv_b, layer):
    ff = w_gate.shape[2]
    tn = min(FFN_TILE, ff)
    main = (ff // tn) * tn
    out = _ffn_in_span(h, ssq, w_gate, w_up, conv_w, conv_b, layer, tn, 0, main // tn, None)
    if main < ff:
        tail = math.gcd(ff - main, main)
        assert tail % LANES == 0
        out = _ffn_in_span(h, ssq, w_gate, w_up, conv_w, conv_b, layer, tail, main, (ff - main) // tail, out)
    return out


MATMUL_WEIGHTS = ("w_in", "w_branch", "w_out", "w_ffn_in", "w_ffn_gate", "w_ffn_up", "w_ffn_out",
                  "ffn_conv_w", "ffn_conv_b")


def _prepare_weights(w):
    ff = w["w_ffn_out"].shape[1]
    w_ffn_in = w["w_ffn_in"]
    return dict(
        w_in=w["w_in"].astype(BF16),
        w_branch=w["w_branch"].astype(BF16),
        w_out=w["w_out"].astype(BF16),
        w_ffn_gate=w_ffn_in[..., :ff].astype(BF16),
        w_ffn_up=w_ffn_in[..., ff:].astype(BF16),
        w_ffn_out=w["w_ffn_out"].astype(BF16),
        ffn_conv_w=w["ffn_conv_w"].astype(F32),
        ffn_conv_b=w["ffn_conv_b"].astype(F32)[:, None, :],
    )


def _layer(x, l, p, tables, normed, next_gain):
    seq, d = x.shape
    width = d // 4
    kvw = KV_HEADS * HEAD_DIM
    aq, r_offs = 3 * width, [4 * width, 5 * width, 6 * width, 7 * width]
    ak, av, gates = 8 * width, 8 * width + kvw, 8 * width + 2 * kvw
    tn = 2 * kvw
    assert (4 * width) % tn == 0
    kv_old, kv_new, r_end = 4 * width // tn, 8 * width // tn, (8 * width + 2 * kvw) // tn

    def z_tile(j):
        return jnp.where(j < kv_old, j, jnp.where(j == kv_old, kv_new, jnp.where(j < r_end, j - 1, j)))

    small = {name: arr[l] for name, arr in p.items() if name not in MATMUL_WEIGHTS}
    if normed is None:
        h, ssq = _rmsnorm(x, small["norm_mix"], BF16), None
    else:
        h, ssq = normed
    z = _matmul(h, p["w_in"], l, tm=1024, tn=tn, out_dtype=BF16, ssq=ssq, out_col=z_tile, name="in_proj")
    y_a = _hyena_mixer(z, small, tables, seq, width)
    y_b = _attention(z, small["attn_sink"], seq, width, aq, ak, av)
    y_c = _retention(z, seq, width, r_offs)
    merged = _merge(y_a, y_b, y_c, p["w_branch"], l, z, gates, d)
    x, h, ssq = _matmul(merged, p["w_out"], l, tm=1024, tn=512, out_dtype=F32, residual=x,
                        next_gain=small["norm_ffn"], name="out_proj")
    act = _ffn_in(h, ssq, p["w_ffn_gate"], p["w_ffn_up"], p["ffn_conv_w"], p["ffn_conv_b"], l)
    ffn_out = functools.partial(_matmul, act, p["w_ffn_out"], l, tm=1024, tn=512, tk=act.shape[1] // 2,
                                out_dtype=F32, residual=x, name="ffn_out")
    if next_gain is None:
        return ffn_out(), None
    x, h, ssq = ffn_out(next_gain=next_gain)
    return x, (h, ssq)


def _trunk(x, p, norm_final):
    b, seq, d = x.shape
    assert b == 1
    tables = _fft_tables(seq)
    x = x.reshape(seq, d)
    depth = p["w_in"].shape[0]
    normed = None
    for l in range(depth):
        next_gain = p["norm_mix"][l + 1] if l + 1 < depth else None
        x, normed = _layer(x, l, p, tables, normed, next_gain)
    return _rmsnorm(x, norm_final, F32).reshape(b, seq, d)


def kernel(x_prompt, x_sample, norm_mix, w_in, hy_conv_w, hy_conv_b, hy_filt_w1, hy_filt_b1, hy_filt_w2, hy_filt_b2, hy_filt_w3, hy_filt_freq, hy_skip, attn_sink, w_branch, w_out, norm_ffn, w_ffn_in, ffn_conv_w, ffn_conv_b, w_ffn_out, norm_final):
    w = dict(norm_mix=norm_mix, w_in=w_in, hy_conv_w=hy_conv_w, hy_conv_b=hy_conv_b,
             hy_filt_w1=hy_filt_w1, hy_filt_b1=hy_filt_b1, hy_filt_w2=hy_filt_w2,
             hy_filt_b2=hy_filt_b2, hy_filt_w3=hy_filt_w3, hy_filt_freq=hy_filt_freq,
             hy_skip=hy_skip, attn_sink=attn_sink, w_branch=w_branch, w_out=w_out,
             norm_ffn=norm_ffn, w_ffn_in=w_ffn_in, ffn_conv_w=ffn_conv_w,
             ffn_conv_b=ffn_conv_b, w_ffn_out=w_ffn_out)
    p = dict(w)
    p.update(_prepare_weights(w))
    return (_trunk(x_prompt, p, norm_final), _trunk(x_sample, p, norm_final))
```

```python
import functools
import math

import jax
import jax.numpy as jnp
from jax import lax
from jax.experimental import pallas as pl
from jax.experimental.pallas import tpu as pltpu

F32 = jnp.float32
BF16 = jnp.bfloat16

LANES = 128
SUBLANES = 8
EPS = 1e-6
NEG = -1e30
HEAD_DIM = 128
KV_HEADS = 2
WINDOW = 128
ATTN_BLOCKS = 8
RET_CHUNK = 128
RET_CHUNKS_PER_STEP = 4
HY_ORDER = 2
HY_EMB = 33
HY_BANDS = (HY_EMB - 1) // 2
HY_MIN_DECAY = math.log(1e-2) / 1.5
HY_MAX_DECAY = math.log(1e-2) / 0.3
DFT_INNER = 128
FFT_CB = 16
FFT_PASSES = 1
FEAT_LANES = 128
HALO = 16
FFN_TILE = 512
VMEM_LIMIT = 56 * 1024 * 1024


def _cparams(*sem, vmem=VMEM_LIMIT):
    return pltpu.CompilerParams(dimension_semantics=sem, vmem_limit_bytes=vmem)


def _tile(n, pref):
    if n <= pref:
        return n
    t = (pref // 128) * 128
    while t >= 128:
        if n % t == 0:
            return t
        t -= 128
    return n


def _dot(a, b):
    return jnp.dot(a, b, preferred_element_type=F32)


def _sigmoid(x):
    return 0.5 * jnp.tanh(0.5 * x) + 0.5


def _split(x):
    hi = x.astype(BF16)
    lo = (x - hi.astype(F32)).astype(BF16)
    return hi, lo


def _dot3(a, b):
    a_hi, a_lo = _split(a)
    b_hi, b_lo = _split(b)
    return _dot(a_hi, b_hi) + (_dot(a_lo, b_hi) + _dot(a_hi, b_lo))


def _rmsnorm_kernel(x_ref, g_ref, o_ref):
    x = x_ref[...]
    ms = jnp.mean(x * x, axis=-1, keepdims=True)
    o_ref[...] = (x * lax.rsqrt(ms + EPS) * g_ref[...]).astype(o_ref.dtype)


def _rmsnorm(x, g, out_dtype):
    m, d = x.shape
    tm = _tile(m, 256)
    return pl.pallas_call(
        _rmsnorm_kernel,
        grid=(m // tm,),
        in_specs=[pl.BlockSpec((tm, d), lambda i: (i, 0)),
                  pl.BlockSpec((1, d), lambda i: (0, 0))],
        out_specs=pl.BlockSpec((tm, d), lambda i: (i, 0)),
        out_shape=jax.ShapeDtypeStruct((m, d), out_dtype),
        compiler_params=_cparams("parallel"),
        name="rmsnorm",
    )(x, g.reshape(1, d).astype(F32))


def _row_scale(ssq, d):
    return lax.rsqrt(ssq[:, 0:1] * (1.0 / d) + EPS)


def _matmul_kernel(*refs, nk, has_res, in_scale_d, norm_out):
    refs = list(refs)
    a_ref, b_ref = refs[:2]
    pos = 2
    s_ref = r_ref = gn_ref = xg_ref = so_ref = None
    if in_scale_d:
        s_ref, pos = refs[pos], pos + 1
    if has_res:
        r_ref, pos = refs[pos], pos + 1
    if norm_out:
        gn_ref, pos = refs[pos], pos + 1
    o_ref, pos = refs[pos], pos + 1
    if norm_out:
        xg_ref, so_ref = refs[pos], refs[pos + 1]
    p = _dot(a_ref[...], b_ref[...])
    j = pl.program_id(1)

    def finish(acc):
        if in_scale_d:
            acc = acc * _row_scale(s_ref[...], in_scale_d)
        if has_res:
            acc = acc + r_ref[...]
        o_ref[...] = acc.astype(o_ref.dtype)
        if norm_out:
            xg_ref[...] = (acc * gn_ref[...]).astype(xg_ref.dtype)
            part = jnp.broadcast_to(jnp.sum(acc * acc, axis=1, keepdims=True), so_ref.shape)

            @pl.when(j == 0)
            def _():
                so_ref[...] = part

            @pl.when(j > 0)
            def _():
                so_ref[...] += part

    if nk == 1:
        finish(p)
        return
    acc_ref = refs[-1]
    k = pl.program_id(2)

    @pl.when(k == 0)
    def _():
        acc_ref[...] = p

    @pl.when(k > 0)
    def _():
        acc_ref[...] += p

    @pl.when(k == nk - 1)
    def _():
        finish(acc_ref[...])


def _matmul(a, b, layer, *, tm, tn, tk=None, out_dtype, ssq=None, residual=None, next_gain=None,
            out_col=lambda j: j, name):
    m, kdim = a.shape
    n = b.shape[2]
    tm, tn = _tile(m, tm), _tile(n, tn)
    tk = kdim if tk is None else _tile(kdim, tk)
    nk = kdim // tk
    in_specs = [pl.BlockSpec((tm, tk), lambda i, j, k: (i, k)),
                pl.BlockSpec((None, tk, tn), lambda i, j, k: (layer, k, j))]
    args = [a, b]
    row_stat = pl.BlockSpec((tm, LANES), lambda i, j, k: (i, 0))
    tile = pl.BlockSpec((tm, tn), lambda i, j, k: (i, j))
    if ssq is not None:
        in_specs.append(row_stat)
        args.append(ssq)
    if residual is not None:
        in_specs.append(tile)
        args.append(residual)
    out_specs = pl.BlockSpec((tm, tn), lambda i, j, k: (i, out_col(j)))
    out_shape = jax.ShapeDtypeStruct((m, n), out_dtype)
    if next_gain is not None:
        in_specs.append(pl.BlockSpec((1, tn), lambda i, j, k: (0, j)))
        args.append(next_gain.reshape(1, n).astype(F32))
        out_specs = [out_specs, tile, row_stat]
        out_shape = [out_shape, jax.ShapeDtypeStruct((m, n), BF16), jax.ShapeDtypeStruct((m, LANES), F32)]
    scratch = [pltpu.VMEM((tm, tn), F32)] if nk > 1 else []
    return pl.pallas_call(
        functools.partial(_matmul_kernel, nk=nk, has_res=residual is not None,
                          in_scale_d=kdim if ssq is not None else 0, norm_out=next_gain is not None),
        grid=(m // tm, n // tn, nk),
        in_specs=in_specs,
        out_specs=out_specs,
        out_shape=out_shape,
        scratch_shapes=scratch,
        compiler_params=_cparams("parallel", "arbitrary", "arbitrary"),
        name=name,
    )(*args)


def _shift_rows(x, prev_row, next_row):
    tm = x.shape[0]
    row = lax.broadcasted_iota(jnp.int32, x.shape, 0)
    down = jnp.where(row == 0, prev_row, pltpu.roll(x, 1, axis=0))
    up = jnp.where(row == tm - 1, next_row, pltpu.roll(x, tm - 1, axis=0))
    return down, up


def _hyconv_kernel(x_ref, xp_ref, xn_ref, w_ref, b_ref, o_ref, *, n_tiles):
    i = pl.program_id(0)
    x = x_ref[...].astype(F32)
    prev_row = xp_ref[...].astype(F32)[HALO - 1:HALO] * (i > 0).astype(F32)
    next_row = xn_ref[...].astype(F32)[0:1] * (i < n_tiles - 1).astype(F32)
    down, up = _shift_rows(x, prev_row, next_row)
    w = w_ref[...]
    o_ref[...] = (down * w[0:1] + x * w[1:2] + up * w[2:3] + b_ref[...]).astype(o_ref.dtype)


def _hyena_conv(z, conv_w, conv_b, seq, width):
    tm = _tile(seq, 512)
    n_tiles = seq // tm
    n_halo = seq // HALO
    per = tm // HALO
    return pl.pallas_call(
        functools.partial(_hyconv_kernel, n_tiles=n_tiles),
        grid=(n_tiles, 3),
        in_specs=[
            pl.BlockSpec((tm, width), lambda i, p: (i, p)),
            pl.BlockSpec((HALO, width), lambda i, p: (jnp.maximum(i * per - 1, 0), p)),
            pl.BlockSpec((HALO, width), lambda i, p: (jnp.minimum((i + 1) * per, n_halo - 1), p)),
            pl.BlockSpec((3, width), lambda i, p: (0, p)),
            pl.BlockSpec((1, width), lambda i, p: (0, p)),
        ],
        out_specs=pl.BlockSpec((None, tm, width), lambda i, p: (p, i, 0)),
        out_shape=jax.ShapeDtypeStruct((3, seq, width), BF16),
        compiler_params=_cparams("parallel", "parallel"),
        name="hyena_conv3",
    )(z, z, z, conv_w.astype(F32), conv_b.reshape(1, -1).astype(F32))


def _filter_kernel(bands_ref, w1_ref, b1_ref, w2_ref, b2_ref, freq_ref, w3_ref, delta_ref,
                   w3b_ref, k_ref, sum_ref, *, seq, tr, width):
    i = pl.program_id(0)
    n = 2 * seq
    r = i * tr + lax.broadcasted_iota(jnp.int32, (tr, FEAT_LANES), 0)
    pos = jnp.where(r < seq, r, n - r).astype(F32)
    t = pos / float(seq - 1)
    lane = lax.broadcasted_iota(jnp.int32, (tr, FEAT_LANES), 1)
    ang = (2.0 * math.pi / seq) * pos * bands_ref[...]
    quarter = jnp.where((lane >= HY_BANDS) & (lane < 2 * HY_BANDS), 0.5 * math.pi, 0.0)
    feats = jnp.where(lane < 2 * HY_BANDS, jnp.cos(ang + quarter),
                      jnp.where(lane == 2 * HY_BANDS, t, 0.0))
    freq = freq_ref[...]
    h = jnp.sin(freq * (_dot3(feats, w1_ref[...]) + b1_ref[...]))
    h = jnp.sin(freq * (_dot3(h, w2_ref[...]) + b2_ref[...]))
    h_hi, h_lo = _split(h)
    r1 = i * tr + lax.broadcasted_iota(jnp.int32, (tr, 1), 0)
    t1 = jnp.where(r1 < seq, r1, n - r1).astype(F32) / float(seq - 1)
    window = jnp.where(r1 == seq, 0.0, jnp.exp(-t1 * delta_ref[...]))
    parts = []
    for o, w3 in enumerate((w3_ref[...], w3b_ref[...])):
        w_hi, w_lo = _split(w3)
        kf = (_dot(h_hi, w_hi) + (_dot(h_lo, w_hi) + _dot(h_hi, w_lo))) * window
        k_ref[:, o * width:(o + 1) * width] = kf.astype(k_ref.dtype)
        parts.append(jnp.sum(jnp.abs(kf), axis=0, keepdims=True))
    part = jnp.concatenate(parts, axis=1)

    @pl.when(i == 0)
    def _():
        sum_ref[...] = part

    @pl.when(i > 0)
    def _():
        sum_ref[...] += part


def _hyena_filters(seq, width, w1, b1, w2, b2, w3, freq):
    n = 2 * seq
    tr = _tile(seq, 512)
    nb = n // tr
    fw = w1.shape[1]
    bands = jnp.linspace(1e-4, HY_BANDS - 1, HY_BANDS, dtype=F32)
    bands = jnp.concatenate([bands, bands, jnp.zeros((FEAT_LANES - 2 * HY_BANDS,), F32)]).reshape(1, -1)
    w1 = w1.astype(F32)
    w1p = jnp.concatenate([w1[1:], w1[:1], jnp.zeros((FEAT_LANES - HY_EMB, fw), F32)], axis=0)
    deltas = jnp.abs(jnp.linspace(HY_MIN_DECAY, HY_MAX_DECAY, width, dtype=F32)).reshape(1, -1)
    row = lambda a: a.reshape(1, -1).astype(F32)
    const = lambda shape: pl.BlockSpec(shape, lambda i: (0, 0))
    w3_spec = lambda o: pl.BlockSpec((fw, width), lambda i: (0, 2 * o + (i >= nb // 2).astype(jnp.int32)))
    assert HY_ORDER == 2
    w3 = w3.astype(F32)
    return pl.pallas_call(
        functools.partial(_filter_kernel, seq=seq, tr=tr, width=width),
        grid=(nb,),
        in_specs=[
            const((1, FEAT_LANES)), const((FEAT_LANES, fw)), const((1, fw)), const((fw, fw)),
            const((1, fw)), const((1, fw)), w3_spec(0), const((1, width)), w3_spec(1),
        ],
        out_specs=[pl.BlockSpec((tr, HY_ORDER * width), lambda i: (i, 0)),
                   pl.BlockSpec((1, HY_ORDER * width), lambda i: (0, 0))],
        out_shape=[jax.ShapeDtypeStruct((n, HY_ORDER * width), BF16),
                   jax.ShapeDtypeStruct((1, HY_ORDER * width), F32)],
        compiler_params=_cparams("arbitrary"),
        name="hyena_filters",
    )(bands, w1p, row(b1), w2.astype(F32), row(b2), row(freq), w3, deltas, w3)


def _fft_tables(seq):
    n = 2 * seq
    n2 = DFT_INNER
    n1 = n // n2
    r = n1 // 2
    k1 = jnp.arange(n1, dtype=jnp.int32)
    th = (2.0 * math.pi / n1) * ((k1[:, None] * k1[None, :]) % n1).astype(F32)
    c, s = jnp.cos(th), jnp.sin(th)
    outer = jnp.concatenate([c, -s], axis=0)
    j = jnp.arange(n2, dtype=jnp.int32)
    ph = (2.0 * math.pi / n) * ((k1[:, None] * j[None, :]) % n).astype(F32)
    th2 = (2.0 * math.pi / n2) * ((j[:, None] * j[None, :]) % n2).astype(F32)
    cr, ci = jnp.cos(th2), -jnp.sin(th2)
    inner = jnp.concatenate([jnp.concatenate([cr, ci], axis=1),
                             jnp.concatenate([-ci, cr], axis=1)], axis=0)
    inv_inner = jnp.concatenate([jnp.concatenate([cr, -ci], axis=1),
                                 jnp.concatenate([ci, cr], axis=1)], axis=0) / n2
    return dict(n1=n1, n2=n2, outer=_split(outer), outer_half=_split(outer[:, :r]),
                inv_outer_re=_split(c[:r] / n1), inv_outer_im=_split(-s[:r] / n1),
                tw_re=jnp.cos(ph), tw_im=-jnp.sin(ph),
                inner=_split(inner), inv_inner=_split(inv_inner))


def _mat_dot(m_hi, m_lo, x, passes):
    x_hi = x.astype(BF16)
    out = _dot(m_hi, x_hi)
    if passes >= 2:
        out = out + _dot(m_hi, (x - x_hi.astype(F32)).astype(BF16))
    if passes >= 3:
        out = out + _dot(m_lo, x_hi)
    return out


def _dot_mat(x, m_hi, m_lo, passes):
    x_hi = x.astype(BF16)
    out = _dot(x_hi, m_hi)
    if passes >= 2:
        out = out + _dot((x - x_hi.astype(F32)).astype(BF16), m_hi)
    if passes >= 3:
        out = out + _dot(x_hi, m_lo)
    return out


def _forward_to_scratch(x_ref, a_scr, f1h, f1l, tw_re, tw_im, n1, cb, passes):
    half = DFT_INNER
    for c in range(0, cb, 2):
        p = _mat_dot(f1h, f1l, jnp.concatenate([x_ref[c], x_ref[c + 1]], axis=1), passes)
        for q in range(2):
            ar, ai = p[:n1, q * half:(q + 1) * half], p[n1:, q * half:(q + 1) * half]
            a_scr[c + q, :, :half] = ar * tw_re - ai * tw_im
            a_scr[c + q, :, half:] = ar * tw_im + ai * tw_re


def _filter_spectrum_kernel(k_ref, inv_ref, f1h_ref, f1l_ref, twr_ref, twi_ref, mh_ref, ml_ref,
                            kr_ref, ki_ref, a_scr, *, n1, cb):
    half = DFT_INNER
    _forward_to_scratch(k_ref, a_scr, f1h_ref[...], f1l_ref[...], twr_ref[...], twi_ref[...], n1, cb, FFT_PASSES)
    x = _dot_mat(a_scr[...].reshape(cb * n1, 2 * half), mh_ref[...], ml_ref[...], FFT_PASSES)
    x = x.reshape(cb, n1, 2 * half)
    inv = inv_ref[...]
    kr_ref[...] = (x[:, :, :half] * inv).astype(kr_ref.dtype)
    ki_ref[...] = (x[:, :, half:] * inv).astype(ki_ref.dtype)


def _const_spec(arr):
    return pl.BlockSpec(arr.shape, lambda j: (0,) * arr.ndim)


def _filter_spectrum(kf_t, inv_sum, tables):
    ch, n1, n2 = kf_t.shape
    cb = FFT_CB
    f1h, f1l = tables["outer"]
    mh, ml = tables["inner"]
    consts = [f1h, f1l, tables["tw_re"], tables["tw_im"], mh, ml]
    blk = pl.BlockSpec((cb, n1, n2), lambda j: (j, 0, 0))
    out = jax.ShapeDtypeStruct((ch, n1, n2), BF16)
    return pl.pallas_call(
        functools.partial(_filter_spectrum_kernel, n1=n1, cb=cb),
        grid=(ch // cb,),
        in_specs=[blk, pl.BlockSpec((cb, 1, n2), lambda j: (j, 0, 0))] + [_const_spec(a) for a in consts],
        out_specs=[blk, blk],
        out_shape=[out, out],
        scratch_shapes=[pltpu.VMEM((cb, n1, 2 * n2), F32)],
        compiler_params=_cparams("parallel"),
        name="filter_spectrum",
    )(kf_t, inv_sum, *consts)


def _fftconv_kernel(u_ref, gate_ref, kr_ref, ki_ref, skip_ref, f1h_ref, f1l_ref, twr_ref, twi_ref,
                    mh_ref, ml_ref, nh_ref, nl_ref, grh_ref, grl_ref, gih_ref, gil_ref,
                    o_ref, a_scr, *, n1, cb, passes):
    half = DFT_INNER
    tw_re, tw_im = twr_ref[...], twi_ref[...]
    _forward_to_scratch(u_ref, a_scr, f1h_ref[...], f1l_ref[...], tw_re, tw_im, n1, cb, passes)
    x = _dot_mat(a_scr[...].reshape(cb * n1, 2 * half), mh_ref[...], ml_ref[...], passes)
    xr, xi = x[:, :half], x[:, half:]
    kr = kr_ref[...].astype(F32).reshape(cb * n1, half)
    ki = ki_ref[...].astype(F32).reshape(cb * n1, half)
    y = jnp.concatenate([xr * kr - xi * ki, xr * ki + xi * kr], axis=1)
    b = _dot_mat(y, nh_ref[...], nl_ref[...], passes)
    a_scr[...] = b.reshape(cb, n1, 2 * half)
    grh, grl, gih, gil = grh_ref[...], grl_ref[...], gih_ref[...], gil_ref[...]
    for c in range(0, cb, 2):
        re, im = [], []
        for q in range(2):
            br, bi = a_scr[c + q, :, :half], a_scr[c + q, :, half:]
            re.append(br * tw_re + bi * tw_im)
            im.append(bi * tw_re - br * tw_im)
        conv = (_mat_dot(grh, grl, jnp.concatenate(re, axis=1), passes)
                + _mat_dot(gih, gil, jnp.concatenate(im, axis=1), passes))
        for q in range(2):
            cq = conv[:, q * half:(q + 1) * half]
            o_ref[c + q] = (gate_ref[c + q] * (cq + u_ref[c + q] * skip_ref[c + q])).astype(o_ref.dtype)


def _fftconv(u_t, u_part, gate_t, gate_part, kr, ki, order, skip_t, tables, width, out_dtype):
    n1, n2 = tables["n1"], tables["n2"]
    rows = n1 // 2
    cb = FFT_CB
    nb = width // cb
    consts = [*tables["outer_half"], tables["tw_re"], tables["tw_im"], *tables["inner"],
              *tables["inv_inner"], *tables["inv_outer_re"], *tables["inv_outer_im"]]
    data = lambda part: pl.BlockSpec((cb, rows, n2), lambda j: (part * nb + j, 0, 0))
    spec = pl.BlockSpec((cb, n1, n2), lambda j: (order * nb + j, 0, 0))
    return pl.pallas_call(
        functools.partial(_fftconv_kernel, n1=n1, cb=cb, passes=FFT_PASSES),
        grid=(nb,),
        in_specs=[data(u_part), data(gate_part), spec, spec,
                  pl.BlockSpec((cb, 1, n2), lambda j: (j, 0, 0))] + [_const_spec(a) for a in consts],
        out_specs=pl.BlockSpec((cb, rows, n2), lambda j: (j, 0, 0)),
        out_shape=jax.ShapeDtypeStruct((width, rows, n2), out_dtype),
        scratch_shapes=[pltpu.VMEM((cb, n1, 2 * n2), F32)],
        compiler_params=_cparams("parallel"),
        name="hyena_fftconv",
    )(u_t, gate_t, kr, ki, skip_t, *consts)


def _lane_rows(v):
    return jnp.broadcast_to(v.astype(F32).reshape(-1, 1, 1), (v.shape[0], 1, DFT_INNER))


def _hyena_mixer(z, p, tables, seq, width):
    n1, n2 = tables["n1"], tables["n2"]
    rows = n1 // 2
    zc = _hyena_conv(z, p["hy_conv_w"], p["hy_conv_b"], seq, width)
    zc_t = jnp.transpose(zc, (0, 2, 1)).reshape(3 * width, rows, n2)
    kf, colsum = _hyena_filters(seq, width, p["hy_filt_w1"], p["hy_filt_b1"], p["hy_filt_w2"],
                                p["hy_filt_b2"], p["hy_filt_w3"], p["hy_filt_freq"])
    kf_t = kf.T.reshape(HY_ORDER * width, n1, n2)
    kr, ki = _filter_spectrum(kf_t, _lane_rows(1.0 / colsum.reshape(-1)), tables)
    skip = p["hy_skip"]
    y1 = _fftconv(zc_t, 0, zc_t, 1, kr, ki, 0, _lane_rows(skip[0]), tables, width, F32)
    y = _fftconv(y1, 0, zc_t, 2, kr, ki, 1, _lane_rows(skip[1]), tables, width, BF16)
    return y.reshape(width, seq).T


def _attn_kernel(q_ref, kp_ref, kc_ref, kn_ref, vp_ref, vc_ref, vn_ref, sink_ref, o_ref, *, seq, group):
    i = pl.program_id(0)
    h = pl.program_id(1)
    blk = WINDOW
    k_all = jnp.concatenate([kp_ref[...], kc_ref[...], kn_ref[...]], axis=0)
    v_all = jnp.concatenate([vp_ref[...], vc_ref[...], vn_ref[...]], axis=0)
    row = lax.broadcasted_iota(jnp.int32, (blk, 3 * blk), 0)
    col = lax.broadcasted_iota(jnp.int32, (blk, 3 * blk), 1)
    dist = jnp.abs(row + blk - col)
    distf = dist.astype(F32)
    for qb in range(ATTN_BLOCKS):
        rows = slice(qb * blk, (qb + 1) * blk)
        q = q_ref[rows, :]
        q4 = jnp.concatenate([q[:, g * HEAD_DIM:(g + 1) * HEAD_DIM] for g in range(group)], axis=0)
        k = k_all[qb * blk:(qb + 3) * blk]
        v = v_all[qb * blk:(qb + 3) * blk]
        s = lax.dot_general(q4, k, (((1,), (1,)), ((), ())), preferred_element_type=F32)
        s = s * (HEAD_DIM ** -0.5)
        k_pos = (i * ATTN_BLOCKS + qb - 1) * blk + col
        valid = (dist <= WINDOW) & (k_pos >= 0) & (k_pos < seq)
        outs = []
        for g in range(group):
            sg = s[g * blk:(g + 1) * blk] - sink_ref[KV_HEADS + h, g] * distf
            sg = jnp.where(valid, sg, NEG)
            sink = sink_ref[h, g]
            m = jnp.maximum(jnp.max(sg, axis=-1, keepdims=True), sink)
            e = jnp.exp(sg - m)
            denom = jnp.sum(e, axis=-1, keepdims=True) + jnp.exp(sink - m)
            pg = (e / denom).astype(v.dtype)
            outs.append(_dot(pg, v))
        o_ref[rows, :] = jnp.concatenate(outs, axis=1).astype(o_ref.dtype)


def _attention(z, sink, seq, width, q_off, k_off, v_off):
    group = width // HEAD_DIM // KV_HEADS
    gw = group * HEAD_DIM
    ab = ATTN_BLOCKS
    rows = ab * WINDOW
    nb = seq // WINDOW
    qb, kb, vb = q_off // gw, k_off // HEAD_DIM, v_off // HEAD_DIM
    assert q_off % gw == 0 and k_off % HEAD_DIM == 0 and v_off % HEAD_DIM == 0 and seq % rows == 0
    halo = lambda base, f: pl.BlockSpec((WINDOW, HEAD_DIM), lambda i, h: (f(i), base + h))
    prev = lambda i: jnp.maximum(i * ab - 1, 0)
    nxt = lambda i: jnp.minimum((i + 1) * ab, nb - 1)
    main = lambda base: pl.BlockSpec((rows, HEAD_DIM), lambda i, h: (i, base + h))
    n_q = KV_HEADS * group
    slopes = 2.0 ** (-8.0 * (jnp.arange(n_q, dtype=F32) + 1.0) / n_q)
    scalars = jnp.concatenate([sink.astype(F32), slopes]).reshape(2 * KV_HEADS, group)
    return pl.pallas_call(
        functools.partial(_attn_kernel, seq=seq, group=group),
        grid=(seq // rows, KV_HEADS),
        in_specs=[pl.BlockSpec((rows, gw), lambda i, h: (i, qb + h)),
                  halo(kb, prev), main(kb), halo(kb, nxt),
                  halo(vb, prev), main(vb), halo(vb, nxt),
                  pl.BlockSpec(memory_space=pltpu.SMEM)],
        out_specs=pl.BlockSpec((rows, gw), lambda i, h: (i, h)),
        out_shape=jax.ShapeDtypeStruct((seq, width), BF16),
        compiler_params=_cparams("parallel", "parallel"),
        name="window_attention",
    )(z, z, z, z, z, z, z, scalars)


def _retention_kernel(*refs, heads, reverse, final, cps):
    if final:
        cd_ref, dec_ref, kw_ref, qw_ref, q_ref, k_ref, v_ref, prev_ref, g_ref, o_ref, s_ref = refs
    else:
        cd_ref, dec_ref, kw_ref, qw_ref, q_ref, k_ref, v_ref, o_ref, s_ref = refs
    hg = pl.program_id(0)
    n = pl.program_id(1)
    c = RET_CHUNK

    @pl.when(n == 0)
    def _():
        s_ref[...] = jnp.zeros_like(s_ref)

    chunk_order = range(cps - 1, -1, -1) if reverse else range(cps)
    for hd in range(heads):
        sl = slice(hd * HEAD_DIM, (hd + 1) * HEAD_DIM)
        decay, k_w, q_w = dec_ref[hd], kw_ref[hd], qw_ref[hd]
        chunk_decay = cd_ref[hg * heads + hd]
        state = s_ref[hd]
        for cc in chunk_order:
            rows = slice(cc * c, (cc + 1) * c)
            q, k, v = q_ref[rows, sl], k_ref[rows, sl], v_ref[rows, sl]
            qk = lax.dot_general(q, k, (((1,), (1,)), ((), ())), preferred_element_type=F32)
            inner = _dot((qk * decay).astype(v.dtype), v)
            cross = _dot((q.astype(F32) * q_w).astype(BF16), state.astype(BF16))
            kv = lax.dot_general((k.astype(F32) * k_w).astype(BF16), v, (((0,), (0,)), ((), ())),
                                 preferred_element_type=F32)
            state = state * chunk_decay + kv
            o = inner + cross
            if final:
                o = o + prev_ref[rows, sl]
                o = o * lax.rsqrt(jnp.mean(o * o, axis=-1, keepdims=True) + EPS)
                g = g_ref[rows, sl].astype(F32)
                o = g * _sigmoid(g) * o
            o_ref[rows, sl] = o.astype(o_ref.dtype)
        s_ref[hd] = state


def _retention_pass(z, seq, width, offs, reverse, prev=None):
    c = RET_CHUNK
    cps = RET_CHUNKS_PER_STEP
    rows = cps * c
    nb = seq // rows
    gw = math.gcd(width, *offs)
    assert gw % HEAD_DIM == 0 and seq % rows == 0
    heads = gw // HEAD_DIM
    q_off, k_off, v_off, g_off = offs
    hidx = jnp.arange(width // HEAD_DIM, dtype=F32)
    idx = jnp.arange(c, dtype=F32)
    rel = idx[:, None] - idx[None, :]
    scale = HEAD_DIM ** -0.5
    if reverse:
        lg = jnp.log(1.0 - 2.0 ** (-5.5 - hidx))[:, None]
        order = lambda n: nb - 1 - n
        rel, k_pow, q_pow = -rel, idx, c - idx
    else:
        lg = jnp.log(1.0 - 2.0 ** (-5.0 - hidx))[:, None]
        order = lambda n: n
        k_pow, q_pow = c - 1.0 - idx, idx + 1.0
    decay = jnp.where(rel >= 0, jnp.exp(jnp.maximum(rel, 0.0)[None] * lg[:, :, None]), 0.0) * scale
    lanes = lambda a: jnp.broadcast_to(a[:, :, None], a.shape + (HEAD_DIM,))
    k_w = lanes(jnp.exp(k_pow[None] * lg) * scale)
    q_w = lanes(jnp.exp(q_pow[None] * lg))
    chunk_decay = jnp.exp(c * lg[:, 0])
    final = prev is not None
    zspec = lambda off: pl.BlockSpec((rows, gw), lambda g, n: (order(n), off // gw + g))
    own = pl.BlockSpec((rows, gw), lambda g, n: (order(n), g))
    table = pl.BlockSpec((heads, c, HEAD_DIM), lambda g, n: (g, 0, 0))
    in_specs = [pl.BlockSpec(memory_space=pltpu.SMEM), table, table, table,
                zspec(q_off), zspec(k_off), zspec(v_off)]
    args = [chunk_decay, decay, k_w, q_w, z, z, z]
    if final:
        in_specs += [own, zspec(g_off)]
        args += [prev, z]
    return pl.pallas_call(
        functools.partial(_retention_kernel, heads=heads, reverse=reverse, final=final, cps=cps),
        grid=(width // gw, nb),
        in_specs=in_specs,
        out_specs=own,
        out_shape=jax.ShapeDtypeStruct((seq, width), BF16 if final else F32),
        scratch_shapes=[pltpu.VMEM((heads, HEAD_DIM, HEAD_DIM), F32)],
        compiler_params=_cparams("parallel", "arbitrary"),
        name="retention_fwd" if final else "retention_bwd",
    )(*args)


def _retention(z, seq, width, offs):
    o_bwd = _retention_pass(z, seq, width, offs, reverse=True)
    return _retention_pass(z, seq, width, offs, reverse=False, prev=o_bwd)


def _merge_kernel(ya_ref, yb_ref, yc_ref, wa_ref, wb_ref, wc_ref, ga_ref, gb_ref, gc_ref, o_ref):
    acc = None
    for y_ref, w_ref, g_ref in ((ya_ref, wa_ref, ga_ref), (yb_ref, wb_ref, gb_ref), (yc_ref, wc_ref, gc_ref)):
        t = _sigmoid(g_ref[...].astype(F32)) * _dot(y_ref[...], w_ref[...])
        acc = t if acc is None else acc + t
    o_ref[...] = acc.astype(o_ref.dtype)


def _merge(ya, yb, yc, w_branch, layer, z, gate_off, d_model):
    seq, width = ya.shape
    tm, tn = _tile(seq, 1024), _tile(d_model, 512)
    assert gate_off % tn == 0
    ys = pl.BlockSpec((tm, width), lambda i, j: (i, 0))
    ws = lambda b: pl.BlockSpec((None, None, width, tn), lambda i, j: (layer, b, 0, j))
    gs = lambda b: pl.BlockSpec((tm, tn), lambda i, j: (i, (gate_off + b * d_model) // tn + j))
    return pl.pallas_call(
        _merge_kernel,
        grid=(seq // tm, d_model // tn),
        in_specs=[ys, ys, ys, ws(0), ws(1), ws(2), gs(0), gs(1), gs(2)],
        out_specs=pl.BlockSpec((tm, tn), lambda i, j: (i, j)),
        out_shape=jax.ShapeDtypeStruct((seq, d_model), BF16),
        compiler_params=_cparams("parallel", "parallel"),
        name="branch_merge",
    )(ya, yb, yc, w_branch, w_branch, w_branch, z, z, z)


def _ffn_in_kernel(h_ref, hp_ref, hn_ref, s_ref, sp_ref, sn_ref, wg_ref, wu_ref, cw_ref, cb_ref, *rest,
                   n_tiles, d):
    o_ref = rest[-1]
    i = pl.program_id(0)
    first = (i > 0).astype(F32)
    last = (i < n_tiles - 1).astype(F32)
    r = _row_scale(s_ref[...], d)
    wg = wg_ref[...]
    g = _dot(h_ref[...], wg) * r
    prev_row = _dot(hp_ref[...], wg)[HALO - 1:HALO] * (_row_scale(sp_ref[...], d)[SUBLANES - 1:SUBLANES] * first)
    next_row = _dot(hn_ref[...], wg)[0:1] * (_row_scale(sn_ref[...], d)[0:1] * last)
    down, up = _shift_rows(g, prev_row, next_row)
    cw = cw_ref[...]
    c = down * cw[0:1] + g * cw[1:2] + up * cw[2:3] + cb_ref[...]
    gelu = 0.5 * c * (1.0 + lax.erf(c * (2.0 ** -0.5)))
    o_ref[...] = (gelu * (_dot(h_ref[...], wu_ref[...]) * r)).astype(o_ref.dtype)


def _ffn_in_span(h, ssq, w_gate, w_up, conv_w, conv_b, layer, tn, col0, n_col_tiles, prev_out):
    seq, d = h.shape
    ff = w_gate.shape[2]
    tm = _tile(seq, 1024)
    n_tiles, n_halo, per = seq // tm, seq // HALO, tm // HALO
    n_sub, per_sub = seq // SUBLANES, tm // SUBLANES
    assert col0 % tn == 0
    j0 = col0 // tn
    in_specs = [
        pl.BlockSpec((tm, d), lambda i, j: (i, 0)),
        pl.BlockSpec((HALO, d), lambda i, j: (jnp.maximum(i * per - 1, 0), 0)),
        pl.BlockSpec((HALO, d), lambda i, j: (jnp.minimum((i + 1) * per, n_halo - 1), 0)),
        pl.BlockSpec((tm, LANES), lambda i, j: (i, 0)),
        pl.BlockSpec((SUBLANES, LANES), lambda i, j: (jnp.maximum(i * per_sub - 1, 0), 0)),
        pl.BlockSpec((SUBLANES, LANES), lambda i, j: (jnp.minimum((i + 1) * per_sub, n_sub - 1), 0)),
        pl.BlockSpec((None, d, tn), lambda i, j: (layer, 0, j0 + j)),
        pl.BlockSpec((None, d, tn), lambda i, j: (layer, 0, j0 + j)),
        pl.BlockSpec((None, 3, tn), lambda i, j: (layer, 0, j0 + j)),
        pl.BlockSpec((None, 1, tn), lambda i, j: (layer, 0, j0 + j)),
    ]
    args = [h, h, h, ssq, ssq, ssq, w_gate, w_up, conv_w, conv_b]
    aliases = {}
    if prev_out is not None:
        in_specs.append(pl.BlockSpec(memory_space=pl.ANY))
        args.append(prev_out)
        aliases = {len(args) - 1: 0}
    return pl.pallas_call(
        functools.partial(_ffn_in_kernel, n_tiles=n_tiles, d=d),
        grid=(n_tiles, n_col_tiles),
        in_specs=in_specs,
        out_specs=pl.BlockSpec((tm, tn), lambda i, j: (i, j0 + j)),
        out_shape=jax.ShapeDtypeStruct((seq, ff), BF16),
        input_output_aliases=aliases,
        compiler_params=_cparams("parallel", "parallel"),
        name="ffn_in",
    )(*args)


def _ffn_in(h, ssq, w_gate, w_up, conv_w, conv_b, layer):
    ff = w_gate.shape[2]
    tn = min(FFN_TILE, ff)
    main = (ff // tn) * tn
    out = _ffn_in_span(h, ssq, w_gate, w_up, conv_w, conv_b, layer, tn, 0, main // tn, None)
    if main < ff:
        tail = math.gcd(ff - main, main)
        assert tail % LANES == 0
        out = _ffn_in_span(h, ssq, w_gate, w_up, conv_w, conv_b, layer, tail, main, (ff - main) // tail, out)
    return out


MATMUL_WEIGHTS = ("w_in", "w_branch", "w_out", "w_ffn_in", "w_ffn_gate", "w_ffn_up", "w_ffn_out",
                  "ffn_conv_w", "ffn_conv_b")


def _prepare_weights(w):
    ff = w["w_ffn_out"].shape[1]
    w_ffn_in = w["w_ffn_in"]
    return dict(
        w_in=w["w_in"].astype(BF16),
        w_branch=w["w_branch"].astype(BF16),
        w_out=w["w_out"].astype(BF16),
        w_ffn_gate=w_ffn_in[..., :ff].astype(BF16),
        w_ffn_up=w_ffn_in[..., ff:].astype(BF16),
        w_ffn_out=w["w_ffn_out"].astype(BF16),
        ffn_conv_w=w["ffn_conv_w"].astype(F32),
        ffn_conv_b=w["ffn_conv_b"].astype(F32)[:, None, :],
    )


def _layer(x, l, p, tables, normed, next_gain):
    seq, d = x.shape
    width = d // 4
    kvw = KV_HEADS * HEAD_DIM
    aq, r_offs = 3 * width, [4 * width, 5 * width, 6 * width, 7 * width]
    ak, av, gates = 8 * width, 8 * width + kvw, 8 * width + 2 * kvw
    tn = 2 * kvw
    assert (4 * width) % tn == 0
    kv_old, kv_new, r_end = 4 * width // tn, 8 * width // tn, (8 * width + 2 * kvw) // tn

    def z_tile(j):
        return jnp.where(j < kv_old, j, jnp.where(j == kv_old, kv_new, jnp.where(j < r_end, j - 1, j)))

    small = {name: arr[l] for name, arr in p.items() if name not in MATMUL_WEIGHTS}
    if normed is None:
        h, ssq = _rmsnorm(x, small["norm_mix"], BF16), None
    else:
        h, ssq = normed
    z = _matmul(h, p["w_in"], l, tm=1024, tn=tn, out_dtype=BF16, ssq=ssq, out_col=z_tile, name="in_proj")
    y_a = _hyena_mixer(z, small, tables, seq, width)
    y_b = _attention(z, small["attn_sink"], seq, width, aq, ak, av)
    y_c = _retention(z, seq, width, r_offs)
    merged = _merge(y_a, y_b, y_c, p["w_branch"], l, z, gates, d)
    x, h, ssq = _matmul(merged, p["w_out"], l, tm=1024, tn=512, out_dtype=F32, residual=x,
                        next_gain=small["norm_ffn"], name="out_proj")
    act = _ffn_in(h, ssq, p["w_ffn_gate"], p["w_ffn_up"], p["ffn_conv_w"], p["ffn_conv_b"], l)
    ffn_out = functools.partial(_matmul, act, p["w_ffn_out"], l, tm=1024, tn=512, tk=act.shape[1] // 2,
                                out_dtype=F32, residual=x, name="ffn_out")
    if next_gain is None:
        return ffn_out(), None
    x, h, ssq = ffn_out(next_gain=next_gain)
    return x, (h, ssq)


def _trunk(x, p, norm_final):
    b, seq, d = x.shape
    assert b == 1
    tables = _fft_tables(seq)
    x = x.reshape(seq, d)
    depth = p["w_in"].shape[0]
    normed = None
    for l in range(depth):
        next_gain = p["norm_mix"][l + 1] if l + 1 < depth else None
        x, normed = _layer(x, l, p, tables, normed, next_gain)
    return _rmsnorm(x, norm_final, F32).reshape(b, seq, d)


def kernel(x_prompt, x_sample, norm_mix, w_in, hy_conv_w, hy_conv_b, hy_filt_w1, hy_filt_b1, hy_filt_w2, hy_filt_b2, hy_filt_w3, hy_filt_freq, hy_skip, attn_sink, w_branch, w_out, norm_ffn, w_ffn_in, ffn_conv_w, ffn_conv_b, w_ffn_out, norm_final):
    w = dict(norm_mix=norm_mix, w_in=w_in, hy_conv_w=hy_conv_w, hy_conv_b=hy_conv_b,
             hy_filt_w1=hy_filt_w1, hy_filt_b1=hy_filt_b1, hy_filt_w2=hy_filt_w2,
             hy_filt_b2=hy_filt_b2, hy_filt_w3=hy_filt_w3, hy_filt_freq=hy_filt_freq,
             hy_skip=hy_skip, attn_sink=attn_sink, w_branch=w_branch, w_out=w_out,
             norm_ffn=norm_ffn, w_ffn_in=w_ffn_in, ffn_conv_w=ffn_conv_w,
             ffn_conv_b=ffn_conv_b, w_ffn_out=w_ffn_out)
    p = dict(w)
    p.update(_prepare_weights(w))
    return (_trunk(x_prompt, p, norm_final), _trunk(x_sample, p, norm_final))
```

```python
import functools
import math

import jax
import jax.numpy as jnp
from jax import lax
from jax.experimental import pallas as pl
from jax.experimental.pallas import tpu as pltpu

F32 = jnp.float32
BF16 = jnp.bfloat16

LANES = 128
SUBLANES = 8
EPS = 1e-6
NEG = -1e30
HEAD_DIM = 128
KV_HEADS = 2
WINDOW = 128
ATTN_BLOCKS = 8
RET_CHUNK = 128
RET_CHUNKS_PER_STEP = 4
HY_ORDER = 2
HY_EMB = 33
HY_BANDS = (HY_EMB - 1) // 2
HY_MIN_DECAY = math.log(1e-2) / 1.5
HY_MAX_DECAY = math.log(1e-2) / 0.3
DFT_INNER = 128
FFT_CB = 16
FFT_PASSES = 1
FEAT_LANES = 128
HALO = 16
FFN_TILE = 512
VMEM_LIMIT = 56 * 1024 * 1024


def _cparams(*sem, vmem=VMEM_LIMIT):
    return pltpu.CompilerParams(dimension_semantics=sem, vmem_limit_bytes=vmem)


def _tile(n, pref):
    if n <= pref:
        return n
    t = (pref // 128) * 128
    while t >= 128:
        if n % t == 0:
            return t
        t -= 128
    return n


def _dot(a, b):
    return jnp.dot(a, b, preferred_element_type=F32)


def _sigmoid(x):
    return 0.5 * jnp.tanh(0.5 * x) + 0.5


def _split(x):
    hi = x.astype(BF16)
    lo = (x - hi.astype(F32)).astype(BF16)
    return hi, lo


def _dot3(a, b):
    a_hi, a_lo = _split(a)
    b_hi, b_lo = _split(b)
    return _dot(a_hi, b_hi) + (_dot(a_lo, b_hi) + _dot(a_hi, b_lo))


def _rmsnorm_kernel(x_ref, g_ref, o_ref):
    x = x_ref[...]
    ms = jnp.mean(x * x, axis=-1, keepdims=True)
    o_ref[...] = (x * lax.rsqrt(ms + EPS) * g_ref[...]).astype(o_ref.dtype)


def _rmsnorm(x, g, out_dtype):
    m, d = x.shape
    tm = _tile(m, 256)
    return pl.pallas_call(
        _rmsnorm_kernel,
        grid=(m // tm,),
        in_specs=[pl.BlockSpec((tm, d), lambda i: (i, 0)),
                  pl.BlockSpec((1, d), lambda i: (0, 0))],
        out_specs=pl.BlockSpec((tm, d), lambda i: (i, 0)),
        out_shape=jax.ShapeDtypeStruct((m, d), out_dtype),
        compiler_params=_cparams("parallel"),
        name="rmsnorm",
    )(x, g.reshape(1, d).astype(F32))


def _row_scale(ssq, d):
    return lax.rsqrt(ssq[:, 0:1] * (1.0 / d) + EPS)


def _matmul_kernel(*refs, nk, has_res, in_scale_d, norm_out):
    refs = list(refs)
    a_ref, b_ref = refs[:2]
    pos = 2
    s_ref = r_ref = gn_ref = xg_ref = so_ref = None
    if in_scale_d:
        s_ref, pos = refs[pos], pos + 1
    if has_res:
        r_ref, pos = refs[pos], pos + 1
    if norm_out:
        gn_ref, pos = refs[pos], pos + 1
    o_ref, pos = refs[pos], pos + 1
    if norm_out:
        xg_ref, so_ref = refs[pos], refs[pos + 1]
    p = _dot(a_ref[...], b_ref[...])
    j = pl.program_id(1)

    def finish(acc):
        if in_scale_d:
            acc = acc * _row_scale(s_ref[...], in_scale_d)
        if has_res:
            acc = acc + r_ref[...]
        o_ref[...] = acc.astype(o_ref.dtype)
        if norm_out:
            xg_ref[...] = (acc * gn_ref[...]).astype(xg_ref.dtype)
            part = jnp.broadcast_to(jnp.sum(acc * acc, axis=1, keepdims=True), so_ref.shape)

            @pl.when(j == 0)
            def _():
                so_ref[...] = part

            @pl.when(j > 0)
            def _():
                so_ref[...] += part

    if nk == 1:
        finish(p)
        return
    acc_ref = refs[-1]
    k = pl.program_id(2)

    @pl.when(k == 0)
    def _():
        acc_ref[...] = p

    @pl.when(k > 0)
    def _():
        acc_ref[...] += p

    @pl.when(k == nk - 1)
    def _():
        finish(acc_ref[...])


def _matmul(a, b, layer, *, tm, tn, tk=None, out_dtype, ssq=None, residual=None, next_gain=None,
            out_col=lambda j: j, name):
    m, kdim = a.shape
    n = b.shape[2]
    tm, tn = _tile(m, tm), _tile(n, tn)
    tk = kdim if tk is None else _tile(kdim, tk)
    nk = kdim // tk
    in_specs = [pl.BlockSpec((tm, tk), lambda i, j, k: (i, k)),
                pl.BlockSpec((None, tk, tn), lambda i, j, k: (layer, k, j))]
    args = [a, b]
    row_stat = pl.BlockSpec((tm, LANES), lambda i, j, k: (i, 0))
    tile = pl.BlockSpec((tm, tn), lambda i, j, k: (i, j))
    if ssq is not None:
        in_specs.append(row_stat)
        args.append(ssq)
    if residual is not None:
        in_specs.append(tile)
        args.append(residual)
    out_specs = pl.BlockSpec((tm, tn), lambda i, j, k: (i, out_col(j)))
    out_shape = jax.ShapeDtypeStruct((m, n), out_dtype)
    if next_gain is not None:
        in_specs.append(pl.BlockSpec((1, tn), lambda i, j, k: (0, j)))
        args.append(next_gain.reshape(1, n).astype(F32))
        out_specs = [out_specs, tile, row_stat]
        out_shape = [out_shape, jax.ShapeDtypeStruct((m, n), BF16), jax.ShapeDtypeStruct((m, LANES), F32)]
    scratch = [pltpu.VMEM((tm, tn), F32)] if nk > 1 else []
    return pl.pallas_call(
        functools.partial(_matmul_kernel, nk=nk, has_res=residual is not None,
                          in_scale_d=kdim if ssq is not None else 0, norm_out=next_gain is not None),
        grid=(m // tm, n // tn, nk),
        in_specs=in_specs,
        out_specs=out_specs,
        out_shape=out_shape,
        scratch_shapes=scratch,
        compiler_params=_cparams("parallel", "arbitrary", "arbitrary"),
        name=name,
    )(*args)


def _shift_rows(x, prev_row, next_row):
    tm = x.shape[0]
    row = lax.broadcasted_iota(jnp.int32, x.shape, 0)
    down = jnp.where(row == 0, prev_row, pltpu.roll(x, 1, axis=0))
    up = jnp.where(row == tm - 1, next_row, pltpu.roll(x, tm - 1, axis=0))
    return down, up


def _hyconv_kernel(x_ref, xp_ref, xn_ref, w_ref, b_ref, o_ref, *, n_tiles):
    i = pl.program_id(0)
    x = x_ref[...].astype(F32)
    prev_row = xp_ref[...].astype(F32)[HALO - 1:HALO] * (i > 0).astype(F32)
    next_row = xn_ref[...].astype(F32)[0:1] * (i < n_tiles - 1).astype(F32)
    down, up = _shift_rows(x, prev_row, next_row)
    w = w_ref[...]
    o_ref[...] = (down * w[0:1] + x * w[1:2] + up * w[2:3] + b_ref[...]).astype(o_ref.dtype)


def _hyena_conv(z, conv_w, conv_b, seq, width):
    tm = _tile(seq, 512)
    n_tiles = seq // tm
    n_halo = seq // HALO
    per = tm // HALO
    return pl.pallas_call(
        functools.partial(_hyconv_kernel, n_tiles=n_tiles),
        grid=(n_tiles, 3),
        in_specs=[
            pl.BlockSpec((tm, width), lambda i, p: (i, p)),
            pl.BlockSpec((HALO, width), lambda i, p: (jnp.maximum(i * per - 1, 0), p)),
            pl.BlockSpec((HALO, width), lambda i, p: (jnp.minimum((i + 1) * per, n_halo - 1), p)),
            pl.BlockSpec((3, width), lambda i, p: (0, p)),
            pl.BlockSpec((1, width), lambda i, p: (0, p)),
        ],
        out_specs=pl.BlockSpec((None, tm, width), lambda i, p: (p, i, 0)),
        out_shape=jax.ShapeDtypeStruct((3, seq, width), BF16),
        compiler_params=_cparams("parallel", "parallel"),
        name="hyena_conv3",
    )(z, z, z, conv_w.astype(F32), conv_b.reshape(1, -1).astype(F32))


def _filter_kernel(bands_ref, w1_ref, b1_ref, w2_ref, b2_ref, freq_ref, w3_ref, delta_ref, rev_ref,
                   lo_ref, hi_ref, sum_ref, *, seq, tr, width):
    i = pl.program_id(0)
    ext = tr + SUBLANES
    pos = (i * tr + lax.broadcasted_iota(jnp.int32, (ext, FEAT_LANES), 0)).astype(F32)
    t = pos / float(seq - 1)
    lane = lax.broadcasted_iota(jnp.int32, (ext, FEAT_LANES), 1)
    ang = (2.0 * math.pi / seq) * pos * bands_ref[...]
    quarter = jnp.where((lane >= HY_BANDS) & (lane < 2 * HY_BANDS), 0.5 * math.pi, 0.0)
    feats = jnp.where(lane < 2 * HY_BANDS, jnp.cos(ang + quarter),
                      jnp.where(lane == 2 * HY_BANDS, t, 0.0))
    freq = freq_ref[...]
    h = jnp.sin(freq * (_dot3(feats, w1_ref[...]) + b1_ref[...]))
    h = jnp.sin(freq * (_dot3(h, w2_ref[...]) + b2_ref[...]))
    h_hi, h_lo = _split(h)
    rev = rev_ref[...]
    hb_hi, hb_lo = _dot(rev, h_hi).astype(BF16), _dot(rev, h_lo).astype(BF16)
    hf_hi, hf_lo = h_hi[:tr], h_lo[:tr]
    delta = delta_ref[...]
    m = lax.broadcasted_iota(jnp.int32, (tr, 1), 0)
    pos_f = i * tr + m
    pos_b = i * tr + tr - m
    win_f = jnp.exp(-(pos_f.astype(F32) / float(seq - 1)) * delta)
    win_b = jnp.where(pos_b == seq, 0.0, jnp.exp(-(pos_b.astype(F32) / float(seq - 1)) * delta))
    parts = []
    for o in range(HY_ORDER):
        w_hi, w_lo = _split(w3_ref[:, 2 * o * width:(2 * o + 2) * width])
        wf_hi, wb_hi, wf_lo, wb_lo = w_hi[:, :width], w_hi[:, width:], w_lo[:, :width], w_lo[:, width:]
        kf = (_dot(hf_hi, wf_hi) + (_dot(hf_lo, wf_hi) + _dot(hf_hi, wf_lo))) * win_f
        kb = (_dot(hb_hi, wb_hi) + (_dot(hb_lo, wb_hi) + _dot(hb_hi, wb_lo))) * win_b
        lo_ref[:, o * width:(o + 1) * width] = kf.astype(lo_ref.dtype)
        hi_ref[:, o * width:(o + 1) * width] = kb.astype(hi_ref.dtype)
        parts.append(jnp.sum(jnp.abs(kf), axis=0, keepdims=True) + jnp.sum(jnp.abs(kb), axis=0, keepdims=True))
    part = jnp.concatenate(parts, axis=1)

    @pl.when(i == 0)
    def _():
        sum_ref[...] = part

    @pl.when(i > 0)
    def _():
        sum_ref[...] += part


def _hyena_filters(seq, width, w1, b1, w2, b2, w3, freq):
    tr = _tile(seq, 512)
    nb = seq // tr
    ext = tr + SUBLANES
    fw = w1.shape[1]
    bands = jnp.linspace(1e-4, HY_BANDS - 1, HY_BANDS, dtype=F32)
    bands = jnp.concatenate([bands, bands, jnp.zeros((FEAT_LANES - 2 * HY_BANDS,), F32)]).reshape(1, -1)
    w1 = w1.astype(F32)
    w1p = jnp.concatenate([w1[1:], w1[:1], jnp.zeros((FEAT_LANES - HY_EMB, fw), F32)], axis=0)
    deltas = jnp.abs(jnp.linspace(HY_MIN_DECAY, HY_MAX_DECAY, width, dtype=F32)).reshape(1, -1)
    row = lambda a: a.reshape(1, -1).astype(F32)
    const = lambda shape: pl.BlockSpec(shape, lambda i: (0, 0))
    rev = (jnp.arange(ext)[None, :] == tr - jnp.arange(tr)[:, None]).astype(BF16)
    half = jax.ShapeDtypeStruct((seq, HY_ORDER * width), BF16)
    return pl.pallas_call(
        functools.partial(_filter_kernel, seq=seq, tr=tr, width=width),
        grid=(nb,),
        in_specs=[
            const((1, FEAT_LANES)), const((FEAT_LANES, fw)), const((1, fw)), const((fw, fw)),
            const((1, fw)), const((1, fw)), const((fw, 2 * HY_ORDER * width)), const((1, width)),
            const((tr, ext)),
        ],
        out_specs=[pl.BlockSpec((tr, HY_ORDER * width), lambda i: (i, 0)),
                   pl.BlockSpec((tr, HY_ORDER * width), lambda i: (nb - 1 - i, 0)),
                   pl.BlockSpec((1, HY_ORDER * width), lambda i: (0, 0))],
        out_shape=[half, half, jax.ShapeDtypeStruct((1, HY_ORDER * width), F32)],
        compiler_params=_cparams("arbitrary"),
        name="hyena_filters",
    )(bands, w1p, row(b1), w2.astype(F32), row(b2), row(freq), w3.astype(F32), deltas, rev)


def _fft_tables(seq):
    n = 2 * seq
    n2 = DFT_INNER
    n1 = n // n2
    r = n1 // 2
    k1 = jnp.arange(n1, dtype=jnp.int32)
    th = (2.0 * math.pi / n1) * ((k1[:, None] * k1[None, :]) % n1).astype(F32)
    c, s = jnp.cos(th), jnp.sin(th)
    outer = jnp.concatenate([c, -s], axis=0)
    j = jnp.arange(n2, dtype=jnp.int32)
    ph = (2.0 * math.pi / n) * ((k1[:, None] * j[None, :]) % n).astype(F32)
    th2 = (2.0 * math.pi / n2) * ((j[:, None] * j[None, :]) % n2).astype(F32)
    cr, ci = jnp.cos(th2), -jnp.sin(th2)
    inner = jnp.concatenate([jnp.concatenate([cr, ci], axis=1),
                             jnp.concatenate([-ci, cr], axis=1)], axis=0)
    inv_inner = jnp.concatenate([jnp.concatenate([cr, -ci], axis=1),
                                 jnp.concatenate([ci, cr], axis=1)], axis=0) / n2
    return dict(n1=n1, n2=n2, outer=_split(outer), outer_half=_split(outer[:, :r]),
                inv_outer_re=_split(c[:r] / n1), inv_outer_im=_split(-s[:r] / n1),
                tw_re=jnp.cos(ph), tw_im=-jnp.sin(ph),
                inner=_split(inner), inv_inner=_split(inv_inner))


def _mat_dot(m_hi, m_lo, x, passes):
    x_hi = x.astype(BF16)
    out = _dot(m_hi, x_hi)
    if passes >= 2:
        out = out + _dot(m_hi, (x - x_hi.astype(F32)).astype(BF16))
    if passes >= 3:
        out = out + _dot(m_lo, x_hi)
    return out


def _dot_mat(x, m_hi, m_lo, passes):
    x_hi = x.astype(BF16)
    out = _dot(x_hi, m_hi)
    if passes >= 2:
        out = out + _dot((x - x_hi.astype(F32)).astype(BF16), m_hi)
    if passes >= 3:
        out = out + _dot(x_hi, m_lo)
    return out


def _forward_to_scratch(load, a_scr, f1h, f1l, tw_re, tw_im, n1, cb, passes):
    half = DFT_INNER
    for c in range(0, cb, 2):
        p = _mat_dot(f1h, f1l, jnp.concatenate([load(c), load(c + 1)], axis=1), passes)
        for q in range(2):
            ar, ai = p[:n1, q * half:(q + 1) * half], p[n1:, q * half:(q + 1) * half]
            a_scr[c + q, :, :half] = ar * tw_re - ai * tw_im
            a_scr[c + q, :, half:] = ar * tw_im + ai * tw_re


def _filter_spectrum_kernel(lo_ref, hi_ref, inv_ref, f1h_ref, f1l_ref, twr_ref, twi_ref, mh_ref, ml_ref,
                            kr_ref, ki_ref, a_scr, *, n1, cb):
    half = DFT_INNER
    load = lambda c: jnp.concatenate([lo_ref[c], hi_ref[c]], axis=0)
    _forward_to_scratch(load, a_scr, f1h_ref[...], f1l_ref[...], twr_ref[...], twi_ref[...], n1, cb, FFT_PASSES)
    x = _dot_mat(a_scr[...].reshape(cb * n1, 2 * half), mh_ref[...], ml_ref[...], FFT_PASSES)
    x = x.reshape(cb, n1, 2 * half)
    inv = inv_ref[...]
    kr_ref[...] = (x[:, :, :half] * inv).astype(kr_ref.dtype)
    ki_ref[...] = (x[:, :, half:] * inv).astype(ki_ref.dtype)


def _const_spec(arr):
    return pl.BlockSpec(arr.shape, lambda j: (0,) * arr.ndim)


def _filter_spectrum(lo_t, hi_t, inv_sum, tables):
    ch, rows, n2 = lo_t.shape
    n1 = 2 * rows
    cb = FFT_CB
    f1h, f1l = tables["outer"]
    mh, ml = tables["inner"]
    consts = [f1h, f1l, tables["tw_re"], tables["tw_im"], mh, ml]
    blk = pl.BlockSpec((cb, n1, n2), lambda j: (j, 0, 0))
    half_blk = pl.BlockSpec((cb, rows, n2), lambda j: (j, 0, 0))
    out = jax.ShapeDtypeStruct((ch, n1, n2), BF16)
    return pl.pallas_call(
        functools.partial(_filter_spectrum_kernel, n1=n1, cb=cb),
        grid=(ch // cb,),
        in_specs=[half_blk, half_blk, pl.BlockSpec((cb, 1, n2), lambda j: (j, 0, 0))]
        + [_const_spec(a) for a in consts],
        out_specs=[blk, blk],
        out_shape=[out, out],
        scratch_shapes=[pltpu.VMEM((cb, n1, 2 * n2), F32)],
        compiler_params=_cparams("parallel"),
        name="filter_spectrum",
    )(lo_t, hi_t, inv_sum, *consts)


def _fftconv_kernel(u_ref, gate_ref, kr_ref, ki_ref, skip_ref, f1h_ref, f1l_ref, twr_ref, twi_ref,
                    mh_ref, ml_ref, nh_ref, nl_ref, grh_ref, grl_ref, gih_ref, gil_ref,
                    o_ref, a_scr, *, n1, cb, passes):
    half = DFT_INNER
    tw_re, tw_im = twr_ref[...], twi_ref[...]
    _forward_to_scratch(lambda c: u_ref[c], a_scr, f1h_ref[...], f1l_ref[...], tw_re, tw_im, n1, cb, passes)
    x = _dot_mat(a_scr[...].reshape(cb * n1, 2 * half), mh_ref[...], ml_ref[...], passes)
    xr, xi = x[:, :half], x[:, half:]
    kr = kr_ref[...].astype(F32).reshape(cb * n1, half)
    ki = ki_ref[...].astype(F32).reshape(cb * n1, half)
    y = jnp.concatenate([xr * kr - xi * ki, xr * ki + xi * kr], axis=1)
    b = _dot_mat(y, nh_ref[...], nl_ref[...], passes)
    a_scr[...] = b.reshape(cb, n1, 2 * half)
    grh, grl, gih, gil = grh_ref[...], grl_ref[...], gih_ref[...], gil_ref[...]
    for c in range(0, cb, 2):
        re, im = [], []
        for q in range(2):
            br, bi = a_scr[c + q, :, :half], a_scr[c + q, :, half:]
            re.append(br * tw_re + bi * tw_im)
            im.append(bi * tw_re - br * tw_im)
        conv = (_mat_dot(grh, grl, jnp.concatenate(re, axis=1), passes)
                + _mat_dot(gih, gil, jnp.concatenate(im, axis=1), passes))
        for q in range(2):
            cq = conv[:, q * half:(q + 1) * half]
            o_ref[c + q] = (gate_ref[c + q] * (cq + u_ref[c + q] * skip_ref[c + q])).astype(o_ref.dtype)


def _fftconv(u_t, u_part, gate_t, gate_part, kr, ki, order, skip_t, tables, width, out_dtype):
    n1, n2 = tables["n1"], tables["n2"]
    rows = n1 // 2
    cb = FFT_CB
    nb = width // cb
    consts = [*tables["outer_half"], tables["tw_re"], tables["tw_im"], *tables["inner"],
              *tables["inv_inner"], *tables["inv_outer_re"], *tables["inv_outer_im"]]
    data = lambda part: pl.BlockSpec((cb, rows, n2), lambda j: (part * nb + j, 0, 0))
    spec = pl.BlockSpec((cb, n1, n2), lambda j: (order * nb + j, 0, 0))
    return pl.pallas_call(
        functools.partial(_fftconv_kernel, n1=n1, cb=cb, passes=FFT_PASSES),
        grid=(nb,),
        in_specs=[data(u_part), data(gate_part), spec, spec,
                  pl.BlockSpec((cb, 1, n2), lambda j: (j, 0, 0))] + [_const_spec(a) for a in consts],
        out_specs=pl.BlockSpec((cb, rows, n2), lambda j: (j, 0, 0)),
        out_shape=jax.ShapeDtypeStruct((width, rows, n2), out_dtype),
        scratch_shapes=[pltpu.VMEM((cb, n1, 2 * n2), F32)],
        compiler_params=_cparams("parallel"),
        name="hyena_fftconv",
    )(u_t, gate_t, kr, ki, skip_t, *consts)


def _lane_rows(v):
    return jnp.broadcast_to(v.astype(F32).reshape(-1, 1, 1), (v.shape[0], 1, DFT_INNER))


def _hyena_mixer(z, p, tables, seq, width):
    n1, n2 = tables["n1"], tables["n2"]
    rows = n1 // 2
    zc = _hyena_conv(z, p["hy_conv_w"], p["hy_conv_b"], seq, width)
    zc_t = jnp.transpose(zc, (0, 2, 1)).reshape(3 * width, rows, n2)
    k_lo, k_hi, colsum = _hyena_filters(seq, width, p["hy_filt_w1"], p["hy_filt_b1"], p["hy_filt_w2"],
                                        p["hy_filt_b2"], p["hy_filt_w3"], p["hy_filt_freq"])
    to_channels = lambda a: a.T.reshape(HY_ORDER * width, rows, n2)
    kr, ki = _filter_spectrum(to_channels(k_lo), to_channels(k_hi), _lane_rows(1.0 / colsum.reshape(-1)), tables)
    skip = p["hy_skip"]
    y1 = _fftconv(zc_t, 0, zc_t, 1, kr, ki, 0, _lane_rows(skip[0]), tables, width, F32)
    y = _fftconv(y1, 0, zc_t, 2, kr, ki, 1, _lane_rows(skip[1]), tables, width, BF16)
    return y.reshape(width, seq).T


def _attn_kernel(q_ref, kp_ref, kc_ref, kn_ref, vp_ref, vc_ref, vn_ref, sink_ref, o_ref, *, seq, group):
    i = pl.program_id(0)
    h = pl.program_id(1)
    blk = WINDOW
    k_all = jnp.concatenate([kp_ref[...], kc_ref[...], kn_ref[...]], axis=0)
    v_all = jnp.concatenate([vp_ref[...], vc_ref[...], vn_ref[...]], axis=0)
    row = lax.broadcasted_iota(jnp.int32, (blk, 3 * blk), 0)
    col = lax.broadcasted_iota(jnp.int32, (blk, 3 * blk), 1)
    dist = jnp.abs(row + blk - col)
    distf = dist.astype(F32)
    for qb in range(ATTN_BLOCKS):
        rows = slice(qb * blk, (qb + 1) * blk)
        q = q_ref[rows, :]
        q4 = jnp.concatenate([q[:, g * HEAD_DIM:(g + 1) * HEAD_DIM] for g in range(group)], axis=0)
        k = k_all[qb * blk:(qb + 3) * blk]
        v = v_all[qb * blk:(qb + 3) * blk]
        s = lax.dot_general(q4, k, (((1,), (1,)), ((), ())), preferred_element_type=F32)
        s = s * (HEAD_DIM ** -0.5)
        k_pos = (i * ATTN_BLOCKS + qb - 1) * blk + col
        valid = (dist <= WINDOW) & (k_pos >= 0) & (k_pos < seq)
        outs = []
        for g in range(group):
            sg = s[g * blk:(g + 1) * blk] - sink_ref[KV_HEADS + h, g] * distf
            sg = jnp.where(valid, sg, NEG)
            sink = sink_ref[h, g]
            m = jnp.maximum(jnp.max(sg, axis=-1, keepdims=True), sink)
            e = jnp.exp(sg - m)
            denom = jnp.sum(e, axis=-1, keepdims=True) + jnp.exp(sink - m)
            pg = (e / denom).astype(v.dtype)
            outs.append(_dot(pg, v))
        o_ref[rows, :] = jnp.concatenate(outs, axis=1).astype(o_ref.dtype)


def _attention(z, sink, seq, width, q_off, k_off, v_off):
    group = width // HEAD_DIM // KV_HEADS
    gw = group * HEAD_DIM
    ab = ATTN_BLOCKS
    rows = ab * WINDOW
    nb = seq // WINDOW
    qb, kb, vb = q_off // gw, k_off // HEAD_DIM, v_off // HEAD_DIM
    assert q_off % gw == 0 and k_off % HEAD_DIM == 0 and v_off % HEAD_DIM == 0 and seq % rows == 0
    halo = lambda base, f: pl.BlockSpec((WINDOW, HEAD_DIM), lambda i, h: (f(i), base + h))
    prev = lambda i: jnp.maximum(i * ab - 1, 0)
    nxt = lambda i: jnp.minimum((i + 1) * ab, nb - 1)
    main = lambda base: pl.BlockSpec((rows, HEAD_DIM), lambda i, h: (i, base + h))
    n_q = KV_HEADS * group
    slopes = 2.0 ** (-8.0 * (jnp.arange(n_q, dtype=F32) + 1.0) / n_q)
    scalars = jnp.concatenate([sink.astype(F32), slopes]).reshape(2 * KV_HEADS, group)
    return pl.pallas_call(
        functools.partial(_attn_kernel, seq=seq, group=group),
        grid=(seq // rows, KV_HEADS),
        in_specs=[pl.BlockSpec((rows, gw), lambda i, h: (i, qb + h)),
                  halo(kb, prev), main(kb), halo(kb, nxt),
                  halo(vb, prev), main(vb), halo(vb, nxt),
                  pl.BlockSpec(memory_space=pltpu.SMEM)],
        out_specs=pl.BlockSpec((rows, gw), lambda i, h: (i, h)),
        out_shape=jax.ShapeDtypeStruct((seq, width), BF16),
        compiler_params=_cparams("parallel", "parallel"),
        name="window_attention",
    )(z, z, z, z, z, z, z, scalars)


def _retention_kernel(*refs, heads, reverse, final, cps):
    if final:
        cd_ref, dec_ref, kw_ref, qw_ref, q_ref, k_ref, v_ref, prev_ref, g_ref, o_ref, s_ref = refs
    else:
        cd_ref, dec_ref, kw_ref, qw_ref, q_ref, k_ref, v_ref, o_ref, s_ref = refs
    hg = pl.program_id(0)
    n = pl.program_id(1)
    c = RET_CHUNK

    @pl.when(n == 0)
    def _():
        s_ref[...] = jnp.zeros_like(s_ref)

    chunk_order = range(cps - 1, -1, -1) if reverse else range(cps)
    for hd in range(heads):
        sl = slice(hd * HEAD_DIM, (hd + 1) * HEAD_DIM)
        decay, k_w, q_w = dec_ref[hd], kw_ref[hd], qw_ref[hd]
        chunk_decay = cd_ref[hg * heads + hd]
        state = s_ref[hd]
        for cc in chunk_order:
            rows = slice(cc * c, (cc + 1) * c)
            q, k, v = q_ref[rows, sl], k_ref[rows, sl], v_ref[rows, sl]
            qk = lax.dot_general(q, k, (((1,), (1,)), ((), ())), preferred_element_type=F32)
            inner = _dot((qk * decay).astype(v.dtype), v)
            cross = _dot((q.astype(F32) * q_w).astype(BF16), state.astype(BF16))
            kv = lax.dot_general((k.astype(F32) * k_w).astype(BF16), v, (((0,), (0,)), ((), ())),
                                 preferred_element_type=F32)
            state = state * chunk_decay + kv
            o = inner + cross
            if final:
                o = o + prev_ref[rows, sl]
                o = o * lax.rsqrt(jnp.mean(o * o, axis=-1, keepdims=True) + EPS)
                g = g_ref[rows, sl].astype(F32)
                o = g * _sigmoid(g) * o
            o_ref[rows, sl] = o.astype(o_ref.dtype)
        s_ref[hd] = state


def _retention_pass(z, seq, width, offs, reverse, prev=None):
    c = RET_CHUNK
    cps = RET_CHUNKS_PER_STEP
    rows = cps * c
    nb = seq // rows
    gw = math.gcd(width, *offs)
    assert gw % HEAD_DIM == 0 and seq % rows == 0
    heads = gw // HEAD_DIM
    q_off, k_off, v_off, g_off = offs
    hidx = jnp.arange(width // HEAD_DIM, dtype=F32)
    idx = jnp.arange(c, dtype=F32)
    rel = idx[:, None] - idx[None, :]
    scale = HEAD_DIM ** -0.5
    if reverse:
        lg = jnp.log(1.0 - 2.0 ** (-5.5 - hidx))[:, None]
        order = lambda n: nb - 1 - n
        rel, k_pow, q_pow = -rel, idx, c - idx
    else:
        lg = jnp.log(1.0 - 2.0 ** (-5.0 - hidx))[:, None]
        order = lambda n: n
        k_pow, q_pow = c - 1.0 - idx, idx + 1.0
    decay = jnp.where(rel >= 0, jnp.exp(jnp.maximum(rel, 0.0)[None] * lg[:, :, None]), 0.0) * scale
    lanes = lambda a: jnp.broadcast_to(a[:, :, None], a.shape + (HEAD_DIM,))
    k_w = lanes(jnp.exp(k_pow[None] * lg) * scale)
    q_w = lanes(jnp.exp(q_pow[None] * lg))
    chunk_decay = jnp.exp(c * lg[:, 0])
    final = prev is not None
    zspec = lambda off: pl.BlockSpec((rows, gw), lambda g, n: (order(n), off // gw + g))
    own = pl.BlockSpec((rows, gw), lambda g, n: (order(n), g))
    table = pl.BlockSpec((heads, c, HEAD_DIM), lambda g, n: (g, 0, 0))
    in_specs = [pl.BlockSpec(memory_space=pltpu.SMEM), table, table, table,
                zspec(q_off), zspec(k_off), zspec(v_off)]
    args = [chunk_decay, decay, k_w, q_w, z, z, z]
    if final:
        in_specs += [own, zspec(g_off)]
        args += [prev, z]
    return pl.pallas_call(
        functools.partial(_retention_kernel, heads=heads, reverse=reverse, final=final, cps=cps),
        grid=(width // gw, nb),
        in_specs=in_specs,
        out_specs=own,
        out_shape=jax.ShapeDtypeStruct((seq, width), BF16 if final else F32),
        scratch_shapes=[pltpu.VMEM((heads, HEAD_DIM, HEAD_DIM), F32)],
        compiler_params=_cparams("parallel", "arbitrary"),
        name="retention_fwd" if final else "retention_bwd",
    )(*args)


def _retention(z, seq, width, offs):
    o_bwd = _retention_pass(z, seq, width, offs, reverse=True)
    return _retention_pass(z, seq, width, offs, reverse=False, prev=o_bwd)


def _merge_kernel(ya_ref, yb_ref, yc_ref, wa_ref, wb_ref, wc_ref, ga_ref, gb_ref, gc_ref, o_ref):
    acc = None
    for y_ref, w_ref, g_ref in ((ya_ref, wa_ref, ga_ref), (yb_ref, wb_ref, gb_ref), (yc_ref, wc_ref, gc_ref)):
        t = _sigmoid(g_ref[...].astype(F32)) * _dot(y_ref[...], w_ref[...])
        acc = t if acc is None else acc + t
    o_ref[...] = acc.astype(o_ref.dtype)


def _merge(ya, yb, yc, w_branch, layer, z, gate_off, d_model):
    seq, width = ya.shape
    tm, tn = _tile(seq, 1024), _tile(d_model, 512)
    assert gate_off % tn == 0
    ys = pl.BlockSpec((tm, width), lambda i, j: (i, 0))
    ws = lambda b: pl.BlockSpec((None, None, width, tn), lambda i, j: (layer, b, 0, j))
    gs = lambda b: pl.BlockSpec((tm, tn), lambda i, j: (i, (gate_off + b * d_model) // tn + j))
    return pl.pallas_call(
        _merge_kernel,
        grid=(seq // tm, d_model // tn),
        in_specs=[ys, ys, ys, ws(0), ws(1), ws(2), gs(0), gs(1), gs(2)],
        out_specs=pl.BlockSpec((tm, tn), lambda i, j: (i, j)),
        out_shape=jax.ShapeDtypeStruct((seq, d_model), BF16),
        compiler_params=_cparams("parallel", "parallel"),
        name="branch_merge",
    )(ya, yb, yc, w_branch, w_branch, w_branch, z, z, z)


def _ffn_in_kernel(h_ref, hp_ref, hn_ref, s_ref, sp_ref, sn_ref, wg_ref, wu_ref, cw_ref, cb_ref, *rest,
                   n_tiles, d):
    o_ref = rest[-1]
    i = pl.program_id(0)
    first = (i > 0).astype(F32)
    last = (i < n_tiles - 1).astype(F32)
    r = _row_scale(s_ref[...], d)
    wg = wg_ref[...]
    g = _dot(h_ref[...], wg) * r
    prev_row = _dot(hp_ref[...], wg)[HALO - 1:HALO] * (_row_scale(sp_ref[...], d)[SUBLANES - 1:SUBLANES] * first)
    next_row = _dot(hn_ref[...], wg)[0:1] * (_row_scale(sn_ref[...], d)[0:1] * last)
    down, up = _shift_rows(g, prev_row, next_row)
    cw = cw_ref[...]
    c = down * cw[0:1] + g * cw[1:2] + up * cw[2:3] + cb_ref[...]
    gelu = 0.5 * c * (1.0 + lax.erf(c * (2.0 ** -0.5)))
    o_ref[...] = (gelu * (_dot(h_ref[...], wu_ref[...]) * r)).astype(o_ref.dtype)


def _ffn_in_span(h, ssq, w_gate, w_up, conv_w, conv_b, layer, tn, col0, n_col_tiles, prev_out):
    seq, d = h.shape
    ff = w_gate.shape[2]
    tm = _tile(seq, 1024)
    n_tiles, n_halo, per = seq // tm, seq // HALO, tm // HALO
    n_sub, per_sub = seq // SUBLANES, tm // SUBLANES
    assert col0 % tn == 0
    j0 = col0 // tn
    in_specs = [
        pl.BlockSpec((tm, d), lambda i, j: (i, 0)),
        pl.BlockSpec((HALO, d), lambda i, j: (jnp.maximum(i * per - 1, 0), 0)),
        pl.BlockSpec((HALO, d), lambda i, j: (jnp.minimum((i + 1) * per, n_halo - 1), 0)),
        pl.BlockSpec((tm, LANES), lambda i, j: (i, 0)),
        pl.BlockSpec((SUBLANES, LANES), lambda i, j: (jnp.maximum(i * per_sub - 1, 0), 0)),
        pl.BlockSpec((SUBLANES, LANES), lambda i, j: (jnp.minimum((i + 1) * per_sub, n_sub - 1), 0)),
        pl.BlockSpec((None, d, tn), lambda i, j: (layer, 0, j0 + j)),
        pl.BlockSpec((None, d, tn), lambda i, j: (layer, 0, j0 + j)),
        pl.BlockSpec((None, 3, tn), lambda i, j: (layer, 0, j0 + j)),
        pl.BlockSpec((None, 1, tn), lambda i, j: (layer, 0, j0 + j)),
    ]
    args = [h, h, h, ssq, ssq, ssq, w_gate, w_up, conv_w, conv_b]
    aliases = {}
    if prev_out is not None:
        in_specs.append(pl.BlockSpec(memory_space=pl.ANY))
        args.append(prev_out)
        aliases = {len(args) - 1: 0}
    return pl.pallas_call(
        functools.partial(_ffn_in_kernel, n_tiles=n_tiles, d=d),
        grid=(n_tiles, n_col_tiles),
        in_specs=in_specs,
        out_specs=pl.BlockSpec((tm, tn), lambda i, j: (i, j0 + j)),
        out_shape=jax.ShapeDtypeStruct((seq, ff), BF16),
        input_output_aliases=aliases,
        compiler_params=_cparams("parallel", "parallel"),
        name="ffn_in",
    )(*args)


def _ffn_in(h, ssq, w_gate, w_up, conv_w, conv_b, layer):
    ff = w_gate.shape[2]
    tn = min(FFN_TILE, ff)
    main = (ff // tn) * tn
    out = _ffn_in_span(h, ssq, w_gate, w_up, conv_w, conv_b, layer, tn, 0, main // tn, None)
    if main < ff:
        tail = math.gcd(ff - main, main)
        assert tail % LANES == 0
        out = _ffn_in_span(h, ssq, w_gate, w_up, conv_w, conv_b, layer, tail, main, (ff - main) // tail, out)
    return out


MATMUL_WEIGHTS = ("w_in", "w_branch", "w_out", "w_ffn_in", "w_ffn_gate", "w_ffn_up", "w_ffn_out",
                  "ffn_conv_w", "ffn_conv_b")


def _prepare_weights(w):
    ff = w["w_ffn_out"].shape[1]
    w_ffn_in = w["w_ffn_in"]
    return dict(
        w_in=w["w_in"].astype(BF16),
        w_branch=w["w_branch"].astype(BF16),
        w_out=w["w_out"].astype(BF16),
        w_ffn_gate=w_ffn_in[..., :ff].astype(BF16),
        w_ffn_up=w_ffn_in[..., ff:].astype(BF16),
        w_ffn_out=w["w_ffn_out"].astype(BF16),
        ffn_conv_w=w["ffn_conv_w"].astype(F32),
        ffn_conv_b=w["ffn_conv_b"].astype(F32)[:, None, :],
    )


def _layer(x, l, p, tables, normed, next_gain):
    seq, d = x.shape
    width = d // 4
    kvw = KV_HEADS * HEAD_DIM
    aq, r_offs = 3 * width, [4 * width, 5 * width, 6 * width, 7 * width]
    ak, av, gates = 8 * width, 8 * width + kvw, 8 * width + 2 * kvw
    tn = 2 * kvw
    assert (4 * width) % tn == 0
    kv_old, kv_new, r_end = 4 * width // tn, 8 * width // tn, (8 * width + 2 * kvw) // tn

    def z_tile(j):
        return jnp.where(j < kv_old, j, jnp.where(j == kv_old, kv_new, jnp.where(j < r_end, j - 1, j)))

    small = {name: arr[l] for name, arr in p.items() if name not in MATMUL_WEIGHTS}
    if normed is None:
        h, ssq = _rmsnorm(x, small["norm_mix"], BF16), None
    else:
        h, ssq = normed
    z = _matmul(h, p["w_in"], l, tm=1024, tn=tn, out_dtype=BF16, ssq=ssq, out_col=z_tile, name="in_proj")
    y_a = _hyena_mixer(z, small, tables, seq, width)
    y_b = _attention(z, small["attn_sink"], seq, width, aq, ak, av)
    y_c = _retention(z, seq, width, r_offs)
    merged = _merge(y_a, y_b, y_c, p["w_branch"], l, z, gates, d)
    x, h, ssq = _matmul(merged, p["w_out"], l, tm=1024, tn=512, out_dtype=F32, residual=x,
                        next_gain=small["norm_ffn"], name="out_proj")
    act = _ffn_in(h, ssq, p["w_ffn_gate"], p["w_ffn_up"], p["ffn_conv_w"], p["ffn_conv_b"], l)
    ffn_out = functools.partial(_matmul, act, p["w_ffn_out"], l, tm=1024, tn=512, tk=act.shape[1] // 2,
                                out_dtype=F32, residual=x, name="ffn_out")
    if next_gain is None:
        return ffn_out(), None
    x, h, ssq = ffn_out(next_gain=next_gain)
    return x, (h, ssq)


def _trunk(x, p, norm_final):
    b, seq, d = x.shape
    assert b == 1
    tables = _fft_tables(seq)
    x = x.reshape(seq, d)
    depth = p["w_in"].shape[0]
    normed = None
    for l in range(depth):
        next_gain = p["norm_mix"][l + 1] if l + 1 < depth else None
        x, normed = _layer(x, l, p, tables, normed, next_gain)
    return _rmsnorm(x, norm_final, F32).reshape(b, seq, d)


def kernel(x_prompt, x_sample, norm_mix, w_in, hy_conv_w, hy_conv_b, hy_filt_w1, hy_filt_b1, hy_filt_w2, hy_filt_b2, hy_filt_w3, hy_filt_freq, hy_skip, attn_sink, w_branch, w_out, norm_ffn, w_ffn_in, ffn_conv_w, ffn_conv_b, w_ffn_out, norm_final):
    w = dict(norm_mix=norm_mix, w_in=w_in, hy_conv_w=hy_conv_w, hy_conv_b=hy_conv_b,
             hy_filt_w1=hy_filt_w1, hy_filt_b1=hy_filt_b1, hy_filt_w2=hy_filt_w2,
             hy_filt_b2=hy_filt_b2, hy_filt_w3=hy_filt_w3, hy_filt_freq=hy_filt_freq,
             hy_skip=hy_skip, attn_sink=attn_sink, w_branch=w_branch, w_out=w_out,
             norm_ffn=norm_ffn, w_ffn_in=w_ffn_in, ffn_conv_w=ffn_conv_w,
             ffn_conv_b=ffn_conv_b, w_ffn_out=w_ffn_out)
    p = dict(w)
    p.update(_prepare_weights(w))
    return (_trunk(x_prompt, p, norm_final), _trunk(x_sample, p, norm_final))
```

```python
import functools
import math

import jax
import jax.numpy as jnp
from jax import lax
from jax.experimental import pallas as pl
from jax.experimental.pallas import tpu as pltpu

F32 = jnp.float32
BF16 = jnp.bfloat16

LANES = 128
SUBLANES = 8
EPS = 1e-6
NEG = -1e30
HEAD_DIM = 128
KV_HEADS = 2
WINDOW = 128
ATTN_BLOCKS = 8
RET_CHUNK = 128
RET_CHUNKS_PER_STEP = 4
HY_ORDER = 2
HY_EMB = 33
HY_BANDS = (HY_EMB - 1) // 2
HY_MIN_DECAY = math.log(1e-2) / 1.5
HY_MAX_DECAY = math.log(1e-2) / 0.3
DFT_INNER = 128
FFT_CB = 16
FFT_PASSES = 1
FEAT_LANES = 128
HALO = 16
FFN_TILE = 512
VMEM_LIMIT = 56 * 1024 * 1024


def _cparams(*sem, vmem=VMEM_LIMIT):
    return pltpu.CompilerParams(dimension_semantics=sem, vmem_limit_bytes=vmem)


def _tile(n, pref):
    if n <= pref:
        return n
    t = (pref // 128) * 128
    while t >= 128:
        if n % t == 0:
            return t
        t -= 128
    return n


def _dot(a, b):
    return jnp.dot(a, b, preferred_element_type=F32)


def _sigmoid(x):
    return 0.5 * jnp.tanh(0.5 * x) + 0.5


def _split(x):
    hi = x.astype(BF16)
    lo = (x - hi.astype(F32)).astype(BF16)
    return hi, lo


def _dot3(a, b):
    a_hi, a_lo = _split(a)
    b_hi, b_lo = _split(b)
    return _dot(a_hi, b_hi) + (_dot(a_lo, b_hi) + _dot(a_hi, b_lo))


def _rmsnorm_kernel(x_ref, g_ref, o_ref):
    x = x_ref[...]
    ms = jnp.mean(x * x, axis=-1, keepdims=True)
    o_ref[...] = (x * lax.rsqrt(ms + EPS) * g_ref[...]).astype(o_ref.dtype)


def _rmsnorm(x, g, out_dtype):
    m, d = x.shape
    tm = _tile(m, 256)
    return pl.pallas_call(
        _rmsnorm_kernel,
        grid=(m // tm,),
        in_specs=[pl.BlockSpec((tm, d), lambda i: (i, 0)),
                  pl.BlockSpec((1, d), lambda i: (0, 0))],
        out_specs=pl.BlockSpec((tm, d), lambda i: (i, 0)),
        out_shape=jax.ShapeDtypeStruct((m, d), out_dtype),
        compiler_params=_cparams("parallel"),
        name="rmsnorm",
    )(x, g.reshape(1, d).astype(F32))


def _row_scale(ssq, d):
    return lax.rsqrt(ssq[:, 0:1] * (1.0 / d) + EPS)


def _matmul_kernel(*refs, nk, has_res, in_scale_d, norm_out):
    refs = list(refs)
    a_ref, b_ref = refs[:2]
    pos = 2
    s_ref = r_ref = gn_ref = xg_ref = so_ref = None
    if in_scale_d:
        s_ref, pos = refs[pos], pos + 1
    if has_res:
        r_ref, pos = refs[pos], pos + 1
    if norm_out:
        gn_ref, pos = refs[pos], pos + 1
    o_ref, pos = refs[pos], pos + 1
    if norm_out:
        xg_ref, so_ref = refs[pos], refs[pos + 1]
    p = _dot(a_ref[...], b_ref[...])
    j = pl.program_id(1)

    def finish(acc):
        if in_scale_d:
            acc = acc * _row_scale(s_ref[...], in_scale_d)
        if has_res:
            acc = acc + r_ref[...]
        o_ref[...] = acc.astype(o_ref.dtype)
        if norm_out:
            xg_ref[...] = (acc * gn_ref[...]).astype(xg_ref.dtype)
            part = jnp.broadcast_to(jnp.sum(acc * acc, axis=1, keepdims=True), so_ref.shape)

            @pl.when(j == 0)
            def _():
                so_ref[...] = part

            @pl.when(j > 0)
            def _():
                so_ref[...] += part

    if nk == 1:
        finish(p)
        return
    acc_ref = refs[-1]
    k = pl.program_id(2)

    @pl.when(k == 0)
    def _():
        acc_ref[...] = p

    @pl.when(k > 0)
    def _():
        acc_ref[...] += p

    @pl.when(k == nk - 1)
    def _():
        finish(acc_ref[...])


def _matmul(a, b, layer, *, tm, tn, tk=None, out_dtype, ssq=None, residual=None, next_gain=None,
            out_col=lambda j: j, name):
    m, kdim = a.shape
    n = b.shape[2]
    tm, tn = _tile(m, tm), _tile(n, tn)
    tk = kdim if tk is None else _tile(kdim, tk)
    nk = kdim // tk
    in_specs = [pl.BlockSpec((tm, tk), lambda i, j, k: (i, k)),
                pl.BlockSpec((None, tk, tn), lambda i, j, k: (layer, k, j))]
    args = [a, b]
    row_stat = pl.BlockSpec((tm, LANES), lambda i, j, k: (i, 0))
    tile = pl.BlockSpec((tm, tn), lambda i, j, k: (i, j))
    if ssq is not None:
        in_specs.append(row_stat)
        args.append(ssq)
    if residual is not None:
        in_specs.append(tile)
        args.append(residual)
    out_specs = pl.BlockSpec((tm, tn), lambda i, j, k: (i, out_col(j)))
    out_shape = jax.ShapeDtypeStruct((m, n), out_dtype)
    if next_gain is not None:
        in_specs.append(pl.BlockSpec((1, tn), lambda i, j, k: (0, j)))
        args.append(next_gain.reshape(1, n).astype(F32))
        out_specs = [out_specs, tile, row_stat]
        out_shape = [out_shape, jax.ShapeDtypeStruct((m, n), BF16), jax.ShapeDtypeStruct((m, LANES), F32)]
    scratch = [pltpu.VMEM((tm, tn), F32)] if nk > 1 else []
    return pl.pallas_call(
        functools.partial(_matmul_kernel, nk=nk, has_res=residual is not None,
                          in_scale_d=kdim if ssq is not None else 0, norm_out=next_gain is not None),
        grid=(m // tm, n // tn, nk),
        in_specs=in_specs,
        out_specs=out_specs,
        out_shape=out_shape,
        scratch_shapes=scratch,
        compiler_params=_cparams("parallel", "arbitrary", "arbitrary"),
        name=name,
    )(*args)


def _shift_rows(x, prev_row, next_row):
    tm = x.shape[0]
    row = lax.broadcasted_iota(jnp.int32, x.shape, 0)
    down = jnp.where(row == 0, prev_row, pltpu.roll(x, 1, axis=0))
    up = jnp.where(row == tm - 1, next_row, pltpu.roll(x, tm - 1, axis=0))
    return down, up


def _hyconv_kernel(x_ref, xp_ref, xn_ref, w_ref, b_ref, o_ref, *, n_tiles):
    i = pl.program_id(0)
    x = x_ref[...].astype(F32)
    prev_row = xp_ref[...].astype(F32)[HALO - 1:HALO] * (i > 0).astype(F32)
    next_row = xn_ref[...].astype(F32)[0:1] * (i < n_tiles - 1).astype(F32)
    down, up = _shift_rows(x, prev_row, next_row)
    w = w_ref[...]
    o_ref[...] = (down * w[0:1] + x * w[1:2] + up * w[2:3] + b_ref[...]).astype(o_ref.dtype)


def _hyena_conv(z, conv_w, conv_b, seq, width):
    tm = _tile(seq, 512)
    n_tiles = seq // tm
    n_halo = seq // HALO
    per = tm // HALO
    return pl.pallas_call(
        functools.partial(_hyconv_kernel, n_tiles=n_tiles),
        grid=(n_tiles, 3),
        in_specs=[
            pl.BlockSpec((tm, width), lambda i, p: (i, p)),
            pl.BlockSpec((HALO, width), lambda i, p: (jnp.maximum(i * per - 1, 0), p)),
            pl.BlockSpec((HALO, width), lambda i, p: (jnp.minimum((i + 1) * per, n_halo - 1), p)),
            pl.BlockSpec((3, width), lambda i, p: (0, p)),
            pl.BlockSpec((1, width), lambda i, p: (0, p)),
        ],
        out_specs=pl.BlockSpec((None, tm, width), lambda i, p: (p, i, 0)),
        out_shape=jax.ShapeDtypeStruct((3, seq, width), BF16),
        compiler_params=_cparams("parallel", "parallel"),
        name="hyena_conv3",
    )(z, z, z, conv_w.astype(F32), conv_b.reshape(1, -1).astype(F32))


def _filter_kernel(bands_ref, w1_ref, b1_ref, w2_ref, b2_ref, freq_ref, w3_ref, delta_ref, rev_ref,
                   lo_ref, hi_ref, sum_ref, *, seq, tr, width):
    i = pl.program_id(0)
    ext = tr + SUBLANES
    pos = (i * tr + lax.broadcasted_iota(jnp.int32, (ext, FEAT_LANES), 0)).astype(F32)
    t = pos / float(seq - 1)
    lane = lax.broadcasted_iota(jnp.int32, (ext, FEAT_LANES), 1)
    ang = (2.0 * math.pi / seq) * pos * bands_ref[...]
    quarter = jnp.where((lane >= HY_BANDS) & (lane < 2 * HY_BANDS), 0.5 * math.pi, 0.0)
    feats = jnp.where(lane < 2 * HY_BANDS, jnp.cos(ang + quarter),
                      jnp.where(lane == 2 * HY_BANDS, t, 0.0))
    freq = freq_ref[...]
    h = jnp.sin(freq * (_dot3(feats, w1_ref[...]) + b1_ref[...]))
    h = jnp.sin(freq * (_dot3(h, w2_ref[...]) + b2_ref[...]))
    h_hi, h_lo = _split(h)
    rev = rev_ref[...]
    hb_hi, hb_lo = _dot(rev, h_hi).astype(BF16), _dot(rev, h_lo).astype(BF16)
    hf_hi, hf_lo = h_hi[:tr], h_lo[:tr]
    delta = delta_ref[...]
    m = lax.broadcasted_iota(jnp.int32, (tr, 1), 0)
    pos_f = i * tr + m
    pos_b = i * tr + tr - m
    win_f = jnp.exp(-(pos_f.astype(F32) / float(seq - 1)) * delta)
    win_b = jnp.where(pos_b == seq, 0.0, jnp.exp(-(pos_b.astype(F32) / float(seq - 1)) * delta))
    parts = []
    for o in range(HY_ORDER):
        w_hi, w_lo = _split(w3_ref[:, 2 * o * width:(2 * o + 2) * width])
        wf_hi, wb_hi, wf_lo, wb_lo = w_hi[:, :width], w_hi[:, width:], w_lo[:, :width], w_lo[:, width:]
        kf = (_dot(hf_hi, wf_hi) + (_dot(hf_lo, wf_hi) + _dot(hf_hi, wf_lo))) * win_f
        kb = (_dot(hb_hi, wb_hi) + (_dot(hb_lo, wb_hi) + _dot(hb_hi, wb_lo))) * win_b
        lo_ref[:, o * width:(o + 1) * width] = kf.astype(lo_ref.dtype)
        hi_ref[:, o * width:(o + 1) * width] = kb.astype(hi_ref.dtype)
        parts.append(jnp.sum(jnp.abs(kf), axis=0, keepdims=True) + jnp.sum(jnp.abs(kb), axis=0, keepdims=True))
    part = jnp.concatenate(parts, axis=1)

    @pl.when(i == 0)
    def _():
        sum_ref[...] = part

    @pl.when(i > 0)
    def _():
        sum_ref[...] += part


def _hyena_filters(seq, width, w1, b1, w2, b2, w3, freq):
    tr = _tile(seq, 512)
    nb = seq // tr
    ext = tr + SUBLANES
    fw = w1.shape[1]
    bands = jnp.linspace(1e-4, HY_BANDS - 1, HY_BANDS, dtype=F32)
    bands = jnp.concatenate([bands, bands, jnp.zeros((FEAT_LANES - 2 * HY_BANDS,), F32)]).reshape(1, -1)
    w1 = w1.astype(F32)
    w1p = jnp.concatenate([w1[1:], w1[:1], jnp.zeros((FEAT_LANES - HY_EMB, fw), F32)], axis=0)
    deltas = jnp.abs(jnp.linspace(HY_MIN_DECAY, HY_MAX_DECAY, width, dtype=F32)).reshape(1, -1)
    row = lambda a: a.reshape(1, -1).astype(F32)
    const = lambda shape: pl.BlockSpec(shape, lambda i: (0, 0))
    rev = (jnp.arange(ext)[None, :] == tr - jnp.arange(tr)[:, None]).astype(BF16)
    half = jax.ShapeDtypeStruct((seq, HY_ORDER * width), BF16)
    return pl.pallas_call(
        functools.partial(_filter_kernel, seq=seq, tr=tr, width=width),
        grid=(nb,),
        in_specs=[
            const((1, FEAT_LANES)), const((FEAT_LANES, fw)), const((1, fw)), const((fw, fw)),
            const((1, fw)), const((1, fw)), const((fw, 2 * HY_ORDER * width)), const((1, width)),
            const((tr, ext)),
        ],
        out_specs=[pl.BlockSpec((tr, HY_ORDER * width), lambda i: (i, 0)),
                   pl.BlockSpec((tr, HY_ORDER * width), lambda i: (nb - 1 - i, 0)),
                   pl.BlockSpec((1, HY_ORDER * width), lambda i: (0, 0))],
        out_shape=[half, half, jax.ShapeDtypeStruct((1, HY_ORDER * width), F32)],
        compiler_params=_cparams("arbitrary"),
        name="hyena_filters",
    )(bands, w1p, row(b1), w2.astype(F32), row(b2), row(freq), w3.astype(F32), deltas, rev)


def _fft_tables(seq):
    n = 2 * seq
    n2 = DFT_INNER
    n1 = n // n2
    r = n1 // 2
    k1 = jnp.arange(n1, dtype=jnp.int32)
    th = (2.0 * math.pi / n1) * ((k1[:, None] * k1[None, :]) % n1).astype(F32)
    c, s = jnp.cos(th), jnp.sin(th)
    outer = jnp.concatenate([c, -s], axis=0)
    j = jnp.arange(n2, dtype=jnp.int32)
    ph = (2.0 * math.pi / n) * ((k1[:, None] * j[None, :]) % n).astype(F32)
    th2 = (2.0 * math.pi / n2) * ((j[:, None] * j[None, :]) % n2).astype(F32)
    cr, ci = jnp.cos(th2), -jnp.sin(th2)
    inner = jnp.concatenate([jnp.concatenate([cr, ci], axis=1),
                             jnp.concatenate([-ci, cr], axis=1)], axis=0)
    inv_inner = jnp.concatenate([jnp.concatenate([cr, -ci], axis=1),
                                 jnp.concatenate([ci, cr], axis=1)], axis=0) / n2
    return dict(n1=n1, n2=n2, outer=_split(outer), outer_half=_split(outer[:, :r]),
                inv_outer_re=_split(c[:r] / n1), inv_outer_im=_split(-s[:r] / n1),
                tw_re=jnp.cos(ph), tw_im=-jnp.sin(ph),
                inner=_split(inner), inv_inner=_split(inv_inner))


def _mat_dot(m_hi, m_lo, x, passes):
    x_hi = x.astype(BF16)
    out = _dot(m_hi, x_hi)
    if passes >= 2:
        out = out + _dot(m_hi, (x - x_hi.astype(F32)).astype(BF16))
    if passes >= 3:
        out = out + _dot(m_lo, x_hi)
    return out


def _dot_mat(x, m_hi, m_lo, passes):
    x_hi = x.astype(BF16)
    out = _dot(x_hi, m_hi)
    if passes >= 2:
        out = out + _dot((x - x_hi.astype(F32)).astype(BF16), m_hi)
    if passes >= 3:
        out = out + _dot(x_hi, m_lo)
    return out


def _forward_to_scratch(load, a_scr, f1h, f1l, tw_re, tw_im, n1, cb, passes):
    half = DFT_INNER
    for c in range(0, cb, 2):
        p = _mat_dot(f1h, f1l, jnp.concatenate([load(c), load(c + 1)], axis=1), passes)
        for q in range(2):
            ar, ai = p[:n1, q * half:(q + 1) * half], p[n1:, q * half:(q + 1) * half]
            a_scr[c + q, :, :half] = ar * tw_re - ai * tw_im
            a_scr[c + q, :, half:] = ar * tw_im + ai * tw_re


def _filter_spectrum_kernel(lo_ref, hi_ref, inv_ref, f1h_ref, f1l_ref, twr_ref, twi_ref, mh_ref, ml_ref,
                            kr_ref, ki_ref, a_scr, *, n1, cb):
    half = DFT_INNER
    load = lambda c: jnp.concatenate([lo_ref[c], hi_ref[c]], axis=0)
    _forward_to_scratch(load, a_scr, f1h_ref[...], f1l_ref[...], twr_ref[...], twi_ref[...], n1, cb, FFT_PASSES)
    x = _dot_mat(a_scr[...].reshape(cb * n1, 2 * half), mh_ref[...], ml_ref[...], FFT_PASSES)
    x = x.reshape(cb, n1, 2 * half)
    inv = inv_ref[...]
    kr_ref[...] = (x[:, :, :half] * inv).astype(kr_ref.dtype)
    ki_ref[...] = (x[:, :, half:] * inv).astype(ki_ref.dtype)


def _const_spec(arr):
    return pl.BlockSpec(arr.shape, lambda j: (0,) * arr.ndim)


def _filter_spectrum(lo_t, hi_t, inv_sum, tables):
    ch, rows, n2 = lo_t.shape
    n1 = 2 * rows
    cb = FFT_CB
    f1h, f1l = tables["outer"]
    mh, ml = tables["inner"]
    consts = [f1h, f1l, tables["tw_re"], tables["tw_im"], mh, ml]
    blk = pl.BlockSpec((cb, n1, n2), lambda j: (j, 0, 0))
    half_blk = pl.BlockSpec((cb, rows, n2), lambda j: (j, 0, 0))
    out = jax.ShapeDtypeStruct((ch, n1, n2), BF16)
    return pl.pallas_call(
        functools.partial(_filter_spectrum_kernel, n1=n1, cb=cb),
        grid=(ch // cb,),
        in_specs=[half_blk, half_blk, pl.BlockSpec((cb, 1, n2), lambda j: (j, 0, 0))]
        + [_const_spec(a) for a in consts],
        out_specs=[blk, blk],
        out_shape=[out, out],
        scratch_shapes=[pltpu.VMEM((cb, n1, 2 * n2), F32)],
        compiler_params=_cparams("parallel"),
        name="filter_spectrum",
    )(lo_t, hi_t, inv_sum, *consts)


def _fftconv_kernel(u_ref, gate_ref, kr_ref, ki_ref, skip_ref, f1h_ref, f1l_ref, twr_ref, twi_ref,
                    mh_ref, ml_ref, nh_ref, nl_ref, grh_ref, grl_ref, gih_ref, gil_ref,
                    o_ref, a_scr, *, n1, cb, passes):
    half = DFT_INNER
    tw_re, tw_im = twr_ref[...], twi_ref[...]
    _forward_to_scratch(lambda c: u_ref[c], a_scr, f1h_ref[...], f1l_ref[...], tw_re, tw_im, n1, cb, passes)
    x = _dot_mat(a_scr[...].reshape(cb * n1, 2 * half), mh_ref[...], ml_ref[...], passes)
    xr, xi = x[:, :half], x[:, half:]
    kr = kr_ref[...].astype(F32).reshape(cb * n1, half)
    ki = ki_ref[...].astype(F32).reshape(cb * n1, half)
    y = jnp.concatenate([xr * kr - xi * ki, xr * ki + xi * kr], axis=1)
    b = _dot_mat(y, nh_ref[...], nl_ref[...], passes)
    a_scr[...] = b.reshape(cb, n1, 2 * half)
    grh, grl, gih, gil = grh_ref[...], grl_ref[...], gih_ref[...], gil_ref[...]
    for c in range(0, cb, 2):
        re, im = [], []
        for q in range(2):
            br, bi = a_scr[c + q, :, :half], a_scr[c + q, :, half:]
            re.append(br * tw_re + bi * tw_im)
            im.append(bi * tw_re - br * tw_im)
        conv = (_mat_dot(grh, grl, jnp.concatenate(re, axis=1), passes)
                + _mat_dot(gih, gil, jnp.concatenate(im, axis=1), passes))
        for q in range(2):
            cq = conv[:, q * half:(q + 1) * half]
            o_ref[c + q] = (gate_ref[c + q] * (cq + u_ref[c + q] * skip_ref[c + q])).astype(o_ref.dtype)


def _fftconv(u_t, u_part, gate_t, gate_part, kr, ki, order, skip_t, tables, width, out_dtype):
    n1, n2 = tables["n1"], tables["n2"]
    rows = n1 // 2
    cb = FFT_CB
    nb = width // cb
    consts = [*tables["outer_half"], tables["tw_re"], tables["tw_im"], *tables["inner"],
              *tables["inv_inner"], *tables["inv_outer_re"], *tables["inv_outer_im"]]
    data = lambda part: pl.BlockSpec((cb, rows, n2), lambda j: (part * nb + j, 0, 0))
    spec = pl.BlockSpec((cb, n1, n2), lambda j: (order * nb + j, 0, 0))
    return pl.pallas_call(
        functools.partial(_fftconv_kernel, n1=n1, cb=cb, passes=FFT_PASSES),
        grid=(nb,),
        in_specs=[data(u_part), data(gate_part), spec, spec,
                  pl.BlockSpec((cb, 1, n2), lambda j: (j, 0, 0))] + [_const_spec(a) for a in consts],
        out_specs=pl.BlockSpec((cb, rows, n2), lambda j: (j, 0, 0)),
        out_shape=jax.ShapeDtypeStruct((width, rows, n2), out_dtype),
        scratch_shapes=[pltpu.VMEM((cb, n1, 2 * n2), F32)],
        compiler_params=_cparams("parallel"),
        name="hyena_fftconv",
    )(u_t, gate_t, kr, ki, skip_t, *consts)


def _lane_rows(v):
    return jnp.broadcast_to(v.astype(F32).reshape(-1, 1, 1), (v.shape[0], 1, DFT_INNER))


def _hyena_mixer(z, p, tables, seq, width):
    n1, n2 = tables["n1"], tables["n2"]
    rows = n1 // 2
    zc = _hyena_conv(z, p["hy_conv_w"], p["hy_conv_b"], seq, width)
    zc_t = jnp.transpose(zc, (0, 2, 1)).reshape(3 * width, rows, n2)
    k_lo, k_hi, colsum = _hyena_filters(seq, width, p["hy_filt_w1"], p["hy_filt_b1"], p["hy_filt_w2"],
                                        p["hy_filt_b2"], p["hy_filt_w3"], p["hy_filt_freq"])
    to_channels = lambda a: a.T.reshape(HY_ORDER * width, rows, n2)
    kr, ki = _filter_spectrum(to_channels(k_lo), to_channels(k_hi), _lane_rows(1.0 / colsum.reshape(-1)), tables)
    skip = p["hy_skip"]
    y1 = _fftconv(zc_t, 0, zc_t, 1, kr, ki, 0, _lane_rows(skip[0]), tables, width, F32)
    y = _fftconv(y1, 0, zc_t, 2, kr, ki, 1, _lane_rows(skip[1]), tables, width, BF16)
    return y.reshape(width, seq).T


def _attn_kernel(q_ref, kp_ref, kc_ref, kn_ref, vp_ref, vc_ref, vn_ref, sink_ref, o_ref, *, seq, group):
    i = pl.program_id(0)
    h = pl.program_id(1)
    blk = WINDOW
    k_all = jnp.concatenate([kp_ref[...], kc_ref[...], kn_ref[...]], axis=0)
    v_all = jnp.concatenate([vp_ref[...], vc_ref[...], vn_ref[...]], axis=0)
    row = lax.broadcasted_iota(jnp.int32, (blk, 3 * blk), 0)
    col = lax.broadcasted_iota(jnp.int32, (blk, 3 * blk), 1)
    dist = jnp.abs(row + blk - col)
    distf = dist.astype(F32)
    for qb in range(ATTN_BLOCKS):
        rows = slice(qb * blk, (qb + 1) * blk)
        q = q_ref[rows, :]
        q4 = jnp.concatenate([q[:, g * HEAD_DIM:(g + 1) * HEAD_DIM] for g in range(group)], axis=0)
        k = k_all[qb * blk:(qb + 3) * blk]
        v = v_all[qb * blk:(qb + 3) * blk]
        s = lax.dot_general(q4, k, (((1,), (1,)), ((), ())), preferred_element_type=F32)
        s = s * (HEAD_DIM ** -0.5)
        k_pos = (i * ATTN_BLOCKS + qb - 1) * blk + col
        valid = (dist <= WINDOW) & (k_pos >= 0) & (k_pos < seq)
        outs = []
        for g in range(group):
            sg = s[g * blk:(g + 1) * blk] - sink_ref[KV_HEADS + h, g] * distf
            sg = jnp.where(valid, sg, NEG)
            sink = sink_ref[h, g]
            m = jnp.maximum(jnp.max(sg, axis=-1, keepdims=True), sink)
            e = jnp.exp(sg - m)
            denom = jnp.sum(e, axis=-1, keepdims=True) + jnp.exp(sink - m)
            pg = (e / denom).astype(v.dtype)
            outs.append(_dot(pg, v))
        o_ref[rows, :] = jnp.concatenate(outs, axis=1).astype(o_ref.dtype)


def _attention(z, sink, seq, width, q_off, k_off, v_off):
    group = width // HEAD_DIM // KV_HEADS
    gw = group * HEAD_DIM
    ab = ATTN_BLOCKS
    rows = ab * WINDOW
    nb = seq // WINDOW
    qb, kb, vb = q_off // gw, k_off // HEAD_DIM, v_off // HEAD_DIM
    assert q_off % gw == 0 and k_off % HEAD_DIM == 0 and v_off % HEAD_DIM == 0 and seq % rows == 0
    halo = lambda base, f: pl.BlockSpec((WINDOW, HEAD_DIM), lambda i, h: (f(i), base + h))
    prev = lambda i: jnp.maximum(i * ab - 1, 0)
    nxt = lambda i: jnp.minimum((i + 1) * ab, nb - 1)
    main = lambda base: pl.BlockSpec((rows, HEAD_DIM), lambda i, h: (i, base + h))
    n_q = KV_HEADS * group
    slopes = 2.0 ** (-8.0 * (jnp.arange(n_q, dtype=F32) + 1.0) / n_q)
    scalars = jnp.concatenate([sink.astype(F32), slopes]).reshape(2 * KV_HEADS, group)
    return pl.pallas_call(
        functools.partial(_attn_kernel, seq=seq, group=group),
        grid=(seq // rows, KV_HEADS),
        in_specs=[pl.BlockSpec((rows, gw), lambda i, h: (i, qb + h)),
                  halo(kb, prev), main(kb), halo(kb, nxt),
                  halo(vb, prev), main(vb), halo(vb, nxt),
                  pl.BlockSpec(memory_space=pltpu.SMEM)],
        out_specs=pl.BlockSpec((rows, gw), lambda i, h: (i, h)),
        out_shape=jax.ShapeDtypeStruct((seq, width), BF16),
        compiler_params=_cparams("parallel", "parallel"),
        name="window_attention",
    )(z, z, z, z, z, z, z, scalars)


def _retention_kernel(*refs, heads, reverse, final, cps):
    if final:
        cd_ref, dec_ref, kw_ref, qw_ref, q_ref, k_ref, v_ref, prev_ref, g_ref, o_ref, s_ref = refs
    else:
        cd_ref, dec_ref, kw_ref, qw_ref, q_ref, k_ref, v_ref, o_ref, s_ref = refs
    hg = pl.program_id(0)
    n = pl.program_id(1)
    c = RET_CHUNK

    @pl.when(n == 0)
    def _():
        s_ref[...] = jnp.zeros_like(s_ref)

    chunk_order = range(cps - 1, -1, -1) if reverse else range(cps)
    for hd in range(heads):
        sl = slice(hd * HEAD_DIM, (hd + 1) * HEAD_DIM)
        decay, k_w, q_w = dec_ref[hd], kw_ref[hd], qw_ref[hd]
        chunk_decay = cd_ref[hg * heads + hd]
        state = s_ref[hd]
        for cc in chunk_order:
            rows = slice(cc * c, (cc + 1) * c)
            q, k, v = q_ref[rows, sl], k_ref[rows, sl], v_ref[rows, sl]
            qk = lax.dot_general(q, k, (((1,), (1,)), ((), ())), preferred_element_type=F32)
            inner = _dot((qk * decay).astype(v.dtype), v)
            cross = _dot((q.astype(F32) * q_w).astype(BF16), state.astype(BF16))
            kv = lax.dot_general((k.astype(F32) * k_w).astype(BF16), v, (((0,), (0,)), ((), ())),
                                 preferred_element_type=F32)
            state = state * chunk_decay + kv
            o = inner + cross
            if final:
                o = o + prev_ref[rows, sl]
                o = o * lax.rsqrt(jnp.mean(o * o, axis=-1, keepdims=True) + EPS)
                g = g_ref[rows, sl].astype(F32)
                o = g * _sigmoid(g) * o
            o_ref[rows, sl] = o.astype(o_ref.dtype)
        s_ref[hd] = state


def _retention_pass(z, seq, width, offs, reverse, prev=None):
    c = RET_CHUNK
    cps = RET_CHUNKS_PER_STEP
    rows = cps * c
    nb = seq // rows
    gw = math.gcd(width, *offs)
    assert gw % HEAD_DIM == 0 and seq % rows == 0
    heads = gw // HEAD_DIM
    q_off, k_off, v_off, g_off = offs
    hidx = jnp.arange(width // HEAD_DIM, dtype=F32)
    idx = jnp.arange(c, dtype=F32)
    rel = idx[:, None] - idx[None, :]
    scale = HEAD_DIM ** -0.5
    if reverse:
        lg = jnp.log(1.0 - 2.0 ** (-5.5 - hidx))[:, None]
        order = lambda n: nb - 1 - n
        rel, k_pow, q_pow = -rel, idx, c - idx
    else:
        lg = jnp.log(1.0 - 2.0 ** (-5.0 - hidx))[:, None]
        order = lambda n: n
        k_pow, q_pow = c - 1.0 - idx, idx + 1.0
    decay = jnp.where(rel >= 0, jnp.exp(jnp.maximum(rel, 0.0)[None] * lg[:, :, None]), 0.0) * scale
    lanes = lambda a: jnp.broadcast_to(a[:, :, None], a.shape + (HEAD_DIM,))
    k_w = lanes(jnp.exp(k_pow[None] * lg) * scale)
    q_w = lanes(jnp.exp(q_pow[None] * lg))
    chunk_decay = jnp.exp(c * lg[:, 0])
    final = prev is not None
    zspec = lambda off: pl.BlockSpec((rows, gw), lambda g, n: (order(n), off // gw + g))
    own = pl.BlockSpec((rows, gw), lambda g, n: (order(n), g))
    table = pl.BlockSpec((heads, c, HEAD_DIM), lambda g, n: (g, 0, 0))
    in_specs = [pl.BlockSpec(memory_space=pltpu.SMEM), table, table, table,
                zspec(q_off), zspec(k_off), zspec(v_off)]
    args = [chunk_decay, decay, k_w, q_w, z, z, z]
    if final:
        in_specs += [own, zspec(g_off)]
        args += [prev, z]
    return pl.pallas_call(
        functools.partial(_retention_kernel, heads=heads, reverse=reverse, final=final, cps=cps),
        grid=(width // gw, nb),
        in_specs=in_specs,
        out_specs=own,
        out_shape=jax.ShapeDtypeStruct((seq, width), BF16 if final else F32),
        scratch_shapes=[pltpu.VMEM((heads, HEAD_DIM, HEAD_DIM), F32)],
        compiler_params=_cparams("parallel", "arbitrary"),
        name="retention_fwd" if final else "retention_bwd",
    )(*args)


def _retention(z, seq, width, offs):
    o_bwd = _retention_pass(z, seq, width, offs, reverse=True)
    return _retention_pass(z, seq, width, offs, reverse=False, prev=o_bwd)


def _merge_kernel(ya_ref, yb_ref, yc_ref, wa_ref, wb_ref, wc_ref, ga_ref, gb_ref, gc_ref, o_ref):
    acc = None
    for y_ref, w_ref, g_ref in ((ya_ref, wa_ref, ga_ref), (yb_ref, wb_ref, gb_ref), (yc_ref, wc_ref, gc_ref)):
        t = _sigmoid(g_ref[...].astype(F32)) * _dot(y_ref[...], w_ref[...])
        acc = t if acc is None else acc + t
    o_ref[...] = acc.astype(o_ref.dtype)


def _merge(ya, yb, yc, w_branch, layer, z, gate_off, d_model):
    seq, width = ya.shape
    tm, tn = _tile(seq, 1024), _tile(d_model, 512)
    assert gate_off % tn == 0
    ys = pl.BlockSpec((tm, width), lambda i, j: (i, 0))
    ws = lambda b: pl.BlockSpec((None, None, width, tn), lambda i, j: (layer, b, 0, j))
    gs = lambda b: pl.BlockSpec((tm, tn), lambda i, j: (i, (gate_off + b * d_model) // tn + j))
    return pl.pallas_call(
        _merge_kernel,
        grid=(seq // tm, d_model // tn),
        in_specs=[ys, ys, ys, ws(0), ws(1), ws(2), gs(0), gs(1), gs(2)],
        out_specs=pl.BlockSpec((tm, tn), lambda i, j: (i, j)),
        out_shape=jax.ShapeDtypeStruct((seq, d_model), BF16),
        compiler_params=_cparams("parallel", "parallel"),
        name="branch_merge",
    )(ya, yb, yc, w_branch, w_branch, w_branch, z, z, z)


def _ffn_in_kernel(h_ref, hp_ref, hn_ref, s_ref, sp_ref, sn_ref, wg_ref, wu_ref, cw_ref, cb_ref, *rest,
                   n_tiles, d):
    o_ref = rest[-1]
    i = pl.program_id(0)
    first = (i > 0).astype(F32)
    last = (i < n_tiles - 1).astype(F32)
    r = _row_scale(s_ref[...], d)
    wg = wg_ref[...]
    g = _dot(h_ref[...], wg) * r
    prev_row = _dot(hp_ref[...], wg)[HALO - 1:HALO] * (_row_scale(sp_ref[...], d)[SUBLANES - 1:SUBLANES] * first)
    next_row = _dot(hn_ref[...], wg)[0:1] * (_row_scale(sn_ref[...], d)[0:1] * last)
    down, up = _shift_rows(g, prev_row, next_row)
    cw = cw_ref[...]
    c = down * cw[0:1] + g * cw[1:2] + up * cw[2:3] + cb_ref[...]
    gelu = 0.5 * c * (1.0 + lax.erf(c * (2.0 ** -0.5)))
    o_ref[...] = (gelu * (_dot(h_ref[...], wu_ref[...]) * r)).astype(o_ref.dtype)


def _ffn_in_span(h, ssq, w_gate, w_up, conv_w, conv_b, layer, tn, col0, n_col_tiles, prev_out):
    seq, d = h.shape
    ff = w_gate.shape[2]
    tm = _tile(seq, 1024)
    n_tiles, n_halo, per = seq // tm, seq // HALO, tm // HALO
    n_sub, per_sub = seq // SUBLANES, tm // SUBLANES
    assert col0 % tn == 0
    j0 = col0 // tn
    in_specs = [
        pl.BlockSpec((tm, d), lambda i, j: (i, 0)),
        pl.BlockSpec((HALO, d), lambda i, j: (jnp.maximum(i * per - 1, 0), 0)),
        pl.BlockSpec((HALO, d), lambda i, j: (jnp.minimum((i + 1) * per, n_halo - 1), 0)),
        pl.BlockSpec((tm, LANES), lambda i, j: (i, 0)),
        pl.BlockSpec((SUBLANES, LANES), lambda i, j: (jnp.maximum(i * per_sub - 1, 0), 0)),
        pl.BlockSpec((SUBLANES, LANES), lambda i, j: (jnp.minimum((i + 1) * per_sub, n_sub - 1), 0)),
        pl.BlockSpec((None, d, tn), lambda i, j: (layer, 0, j0 + j)),
        pl.BlockSpec((None, d, tn), lambda i, j: (layer, 0, j0 + j)),
        pl.BlockSpec((None, 3, tn), lambda i, j: (layer, 0, j0 + j)),
        pl.BlockSpec((None, 1, tn), lambda i, j: (layer, 0, j0 + j)),
    ]
    args = [h, h, h, ssq, ssq, ssq, w_gate, w_up, conv_w, conv_b]
    aliases = {}
    if prev_out is not None:
        in_specs.append(pl.BlockSpec(memory_space=pl.ANY))
        args.append(prev_out)
        aliases = {len(args) - 1: 0}
    return pl.pallas_call(
        functools.partial(_ffn_in_kernel, n_tiles=n_tiles, d=d),
        grid=(n_tiles, n_col_tiles),
        in_specs=in_specs,
        out_specs=pl.BlockSpec((tm, tn), lambda i, j: (i, j0 + j)),
        out_shape=jax.ShapeDtypeStruct((seq, ff), BF16),
        input_output_aliases=aliases,
        compiler_params=_cparams("parallel", "parallel"),
        name="ffn_in",
    )(*args)


def _ffn_in(h, ssq, w_gate, w_up, conv_w, conv_b, layer):
    ff = w_gate.shape[2]
    tn = min(FFN_TILE, ff)
    main = (ff // tn) * tn
    out = _ffn_in_span(h, ssq, w_gate, w_up, conv_w, conv_b, layer, tn, 0, main // tn, None)
    if main < ff:
        tail = math.gcd(ff - main, main)
        assert tail % LANES == 0
        out = _ffn_in_span(h, ssq, w_gate, w_up, conv_w, conv_b, layer, tail, main, (ff - main) // tail, out)
    return out


MATMUL_WEIGHTS = ("w_in", "w_branch", "w_out", "w_ffn_in", "w_ffn_gate", "w_ffn_up", "w_ffn_out",
                  "ffn_conv_w", "ffn_conv_b")


def _prepare_weights(w):
    ff = w["w_ffn_out"].shape[1]
    w_ffn_in = w["w_ffn_in"]
    return dict(
        w_in=w["w_in"].astype(BF16),
        w_branch=w["w_branch"].astype(BF16),
        w_out=w["w_out"].astype(BF16),
        w_ffn_gate=w_ffn_in[..., :ff].astype(BF16),
        w_ffn_up=w_ffn_in[..., ff:].astype(BF16),
        w_ffn_out=w["w_ffn_out"].astype(BF16),
        ffn_conv_w=w["ffn_conv_w"].astype(F32),
        ffn_conv_b=w["ffn_conv_b"].astype(F32)[:, None, :],
    )


def _layer(x, l, p, tables, normed, next_gain):
    seq, d = x.shape
    width = d // 4
    kvw = KV_HEADS * HEAD_DIM
    aq, r_offs = 3 * width, [4 * width, 5 * width, 6 * width, 7 * width]
    ak, av, gates = 8 * width, 8 * width + kvw, 8 * width + 2 * kvw
    tn = 2 * kvw
    assert (4 * width) % tn == 0
    kv_old, kv_new, r_end = 4 * width // tn, 8 * width // tn, (8 * width + 2 * kvw) // tn

    def z_tile(j):
        return jnp.where(j < kv_old, j, jnp.where(j == kv_old, kv_new, jnp.where(j < r_end, j - 1, j)))

    small = {name: arr[l] for name, arr in p.items() if name not in MATMUL_WEIGHTS}
    if normed is None:
        h, ssq = _rmsnorm(x, small["norm_mix"], BF16), None
    else:
        h, ssq = normed
    z = _matmul(h, p["w_in"], l, tm=2048, tn=tn, out_dtype=BF16, ssq=ssq, out_col=z_tile, name="in_proj")
    y_a = _hyena_mixer(z, small, tables, seq, width)
    y_b = _attention(z, small["attn_sink"], seq, width, aq, ak, av)
    y_c = _retention(z, seq, width, r_offs)
    merged = _merge(y_a, y_b, y_c, p["w_branch"], l, z, gates, d)
    x, h, ssq = _matmul(merged, p["w_out"], l, tm=1024, tn=512, out_dtype=F32, residual=x,
                        next_gain=small["norm_ffn"], name="out_proj")
    act = _ffn_in(h, ssq, p["w_ffn_gate"], p["w_ffn_up"], p["ffn_conv_w"], p["ffn_conv_b"], l)
    ffn_out = functools.partial(_matmul, act, p["w_ffn_out"], l, tm=1024, tn=512, tk=act.shape[1] // 2,
                                out_dtype=F32, residual=x, name="ffn_out")
    if next_gain is None:
        return ffn_out(), None
    x, h, ssq = ffn_out(next_gain=next_gain)
    return x, (h, ssq)


def _trunk(x, p, norm_final):
    b, seq, d = x.shape
    assert b == 1
    tables = _fft_tables(seq)
    x = x.reshape(seq, d)
    depth = p["w_in"].shape[0]
    normed = None
    for l in range(depth):
        next_gain = p["norm_mix"][l + 1] if l + 1 < depth else None
        x, normed = _layer(x, l, p, tables, normed, next_gain)
    return _rmsnorm(x, norm_final, F32).reshape(b, seq, d)


def kernel(x_prompt, x_sample, norm_mix, w_in, hy_conv_w, hy_conv_b, hy_filt_w1, hy_filt_b1, hy_filt_w2, hy_filt_b2, hy_filt_w3, hy_filt_freq, hy_skip, attn_sink, w_branch, w_out, norm_ffn, w_ffn_in, ffn_conv_w, ffn_conv_b, w_ffn_out, norm_final):
    w = dict(norm_mix=norm_mix, w_in=w_in, hy_conv_w=hy_conv_w, hy_conv_b=hy_conv_b,
             hy_filt_w1=hy_filt_w1, hy_filt_b1=hy_filt_b1, hy_filt_w2=hy_filt_w2,
             hy_filt_b2=hy_filt_b2, hy_filt_w3=hy_filt_w3, hy_filt_freq=hy_filt_freq,
             hy_skip=hy_skip, attn_sink=attn_sink, w_branch=w_branch, w_out=w_out,
             norm_ffn=norm_ffn, w_ffn_in=w_ffn_in, ffn_conv_w=ffn_conv_w,
             ffn_conv_b=ffn_conv_b, w_ffn_out=w_ffn_out)
    p = dict(w)
    p.update(_prepare_weights(w))
    return (_trunk(x_prompt, p, norm_final), _trunk(x_sample, p, norm_final))
```

```python
import functools
import math

import jax
import jax.numpy as jnp
from jax import lax
from jax.experimental import pallas as pl
from jax.experimental.pallas import tpu as pltpu

F32 = jnp.float32
BF16 = jnp.bfloat16

LANES = 128
SUBLANES = 8
EPS = 1e-6
NEG = -1e30
HEAD_DIM = 128
KV_HEADS = 2
WINDOW = 128
ATTN_BLOCKS = 8
RET_CHUNK = 128
RET_CHUNKS_PER_STEP = 4
HY_ORDER = 2
HY_EMB = 33
HY_BANDS = (HY_EMB - 1) // 2
HY_MIN_DECAY = math.log(1e-2) / 1.5
HY_MAX_DECAY = math.log(1e-2) / 0.3
DFT_INNER = 128
FFT_CB = 16
FFT_PASSES = 1
FEAT_LANES = 128
HALO = 16
FFN_TILE = 512
VMEM_LIMIT = 56 * 1024 * 1024


def _cparams(*sem, vmem=VMEM_LIMIT):
    return pltpu.CompilerParams(dimension_semantics=sem, vmem_limit_bytes=vmem)


def _tile(n, pref):
    if n <= pref:
        return n
    t = (pref // 128) * 128
    while t >= 128:
        if n % t == 0:
            return t
        t -= 128
    return n


def _dot(a, b):
    return jnp.dot(a, b, preferred_element_type=F32)


def _sigmoid(x):
    return 0.5 * jnp.tanh(0.5 * x) + 0.5


def _split(x):
    hi = x.astype(BF16)
    lo = (x - hi.astype(F32)).astype(BF16)
    return hi, lo


def _dot3(a, b):
    a_hi, a_lo = _split(a)
    b_hi, b_lo = _split(b)
    return _dot(a_hi, b_hi) + (_dot(a_lo, b_hi) + _dot(a_hi, b_lo))


def _rmsnorm_kernel(x_ref, g_ref, o_ref):
    x = x_ref[...]
    ms = jnp.mean(x * x, axis=-1, keepdims=True)
    o_ref[...] = (x * lax.rsqrt(ms + EPS) * g_ref[...]).astype(o_ref.dtype)


def _rmsnorm(x, g, out_dtype):
    m, d = x.shape
    tm = _tile(m, 256)
    return pl.pallas_call(
        _rmsnorm_kernel,
        grid=(m // tm,),
        in_specs=[pl.BlockSpec((tm, d), lambda i: (i, 0)),
                  pl.BlockSpec((1, d), lambda i: (0, 0))],
        out_specs=pl.BlockSpec((tm, d), lambda i: (i, 0)),
        out_shape=jax.ShapeDtypeStruct((m, d), out_dtype),
        compiler_params=_cparams("parallel"),
        name="rmsnorm",
    )(x, g.reshape(1, d).astype(F32))


def _row_scale(ssq, d):
    return lax.rsqrt(ssq[:, 0:1] * (1.0 / d) + EPS)


def _matmul_kernel(*refs, nk, has_res, in_scale_d, norm_out):
    refs = list(refs)
    a_ref, b_ref = refs[:2]
    pos = 2
    s_ref = r_ref = gn_ref = xg_ref = so_ref = None
    if in_scale_d:
        s_ref, pos = refs[pos], pos + 1
    if has_res:
        r_ref, pos = refs[pos], pos + 1
    if norm_out:
        gn_ref, pos = refs[pos], pos + 1
    o_ref, pos = refs[pos], pos + 1
    if norm_out:
        xg_ref, so_ref = refs[pos], refs[pos + 1]
    p = _dot(a_ref[...], b_ref[...])
    j = pl.program_id(1)

    def finish(acc):
        if in_scale_d:
            acc = acc * _row_scale(s_ref[...], in_scale_d)
        if has_res:
            acc = acc + r_ref[...]
        o_ref[...] = acc.astype(o_ref.dtype)
        if norm_out:
            xg_ref[...] = (acc * gn_ref[...]).astype(xg_ref.dtype)
            part = jnp.broadcast_to(jnp.sum(acc * acc, axis=1, keepdims=True), so_ref.shape)

            @pl.when(j == 0)
            def _():
                so_ref[...] = part

            @pl.when(j > 0)
            def _():
                so_ref[...] += part

    if nk == 1:
        finish(p)
        return
    acc_ref = refs[-1]
    k = pl.program_id(2)

    @pl.when(k == 0)
    def _():
        acc_ref[...] = p

    @pl.when(k > 0)
    def _():
        acc_ref[...] += p

    @pl.when(k == nk - 1)
    def _():
        finish(acc_ref[...])


def _matmul(a, b, layer, *, tm, tn, tk=None, out_dtype, ssq=None, residual=None, next_gain=None,
            out_col=lambda j: j, name):
    m, kdim = a.shape
    n = b.shape[2]
    tm, tn = _tile(m, tm), _tile(n, tn)
    tk = kdim if tk is None else _tile(kdim, tk)
    nk = kdim // tk
    in_specs = [pl.BlockSpec((tm, tk), lambda i, j, k: (i, k)),
                pl.BlockSpec((None, tk, tn), lambda i, j, k: (layer, k, j))]
    args = [a, b]
    row_stat = pl.BlockSpec((tm, LANES), lambda i, j, k: (i, 0))
    tile = pl.BlockSpec((tm, tn), lambda i, j, k: (i, j))
    if ssq is not None:
        in_specs.append(row_stat)
        args.append(ssq)
    if residual is not None:
        in_specs.append(tile)
        args.append(residual)
    out_specs = pl.BlockSpec((tm, tn), lambda i, j, k: (i, out_col(j)))
    out_shape = jax.ShapeDtypeStruct((m, n), out_dtype)
    if next_gain is not None:
        in_specs.append(pl.BlockSpec((1, tn), lambda i, j, k: (0, j)))
        args.append(next_gain.reshape(1, n).astype(F32))
        out_specs = [out_specs, tile, row_stat]
        out_shape = [out_shape, jax.ShapeDtypeStruct((m, n), BF16), jax.ShapeDtypeStruct((m, LANES), F32)]
    scratch = [pltpu.VMEM((tm, tn), F32)] if nk > 1 else []
    return pl.pallas_call(
        functools.partial(_matmul_kernel, nk=nk, has_res=residual is not None,
                          in_scale_d=kdim if ssq is not None else 0, norm_out=next_gain is not None),
        grid=(m // tm, n // tn, nk),
        in_specs=in_specs,
        out_specs=out_specs,
        out_shape=out_shape,
        scratch_shapes=scratch,
        compiler_params=_cparams("parallel", "arbitrary", "arbitrary"),
        name=name,
    )(*args)


def _shift_rows(x, prev_row, next_row):
    tm = x.shape[0]
    row = lax.broadcasted_iota(jnp.int32, x.shape, 0)
    down = jnp.where(row == 0, prev_row, pltpu.roll(x, 1, axis=0))
    up = jnp.where(row == tm - 1, next_row, pltpu.roll(x, tm - 1, axis=0))
    return down, up


def _hyconv_kernel(x_ref, xp_ref, xn_ref, w_ref, b_ref, o_ref, *, n_tiles):
    i = pl.program_id(0)
    x = x_ref[...].astype(F32)
    prev_row = xp_ref[...].astype(F32)[HALO - 1:HALO] * (i > 0).astype(F32)
    next_row = xn_ref[...].astype(F32)[0:1] * (i < n_tiles - 1).astype(F32)
    down, up = _shift_rows(x, prev_row, next_row)
    w = w_ref[...]
    o_ref[...] = (down * w[0:1] + x * w[1:2] + up * w[2:3] + b_ref[...]).astype(o_ref.dtype)


def _hyena_conv(z, conv_w, conv_b, seq, width):
    tm = _tile(seq, 512)
    n_tiles = seq // tm
    n_halo = seq // HALO
    per = tm // HALO
    return pl.pallas_call(
        functools.partial(_hyconv_kernel, n_tiles=n_tiles),
        grid=(n_tiles, 3),
        in_specs=[
            pl.BlockSpec((tm, width), lambda i, p: (i, p)),
            pl.BlockSpec((HALO, width), lambda i, p: (jnp.maximum(i * per - 1, 0), p)),
            pl.BlockSpec((HALO, width), lambda i, p: (jnp.minimum((i + 1) * per, n_halo - 1), p)),
            pl.BlockSpec((3, width), lambda i, p: (0, p)),
            pl.BlockSpec((1, width), lambda i, p: (0, p)),
        ],
        out_specs=pl.BlockSpec((None, tm, width), lambda i, p: (p, i, 0)),
        out_shape=jax.ShapeDtypeStruct((3, seq, width), BF16),
        compiler_params=_cparams("parallel", "parallel"),
        name="hyena_conv3",
    )(z, z, z, conv_w.astype(F32), conv_b.reshape(1, -1).astype(F32))


def _filter_kernel(bands_ref, w1_ref, b1_ref, w2_ref, b2_ref, freq_ref, w3_ref, delta_ref, rev_ref,
                   lo_ref, hi_ref, sum_ref, *, seq, tr, width):
    i = pl.program_id(0)
    ext = tr + SUBLANES
    pos = (i * tr + lax.broadcasted_iota(jnp.int32, (ext, FEAT_LANES), 0)).astype(F32)
    t = pos / float(seq - 1)
    lane = lax.broadcasted_iota(jnp.int32, (ext, FEAT_LANES), 1)
    ang = (2.0 * math.pi / seq) * pos * bands_ref[...]
    quarter = jnp.where((lane >= HY_BANDS) & (lane < 2 * HY_BANDS), 0.5 * math.pi, 0.0)
    feats = jnp.where(lane < 2 * HY_BANDS, jnp.cos(ang + quarter),
                      jnp.where(lane == 2 * HY_BANDS, t, 0.0))
    freq = freq_ref[...]
    h = jnp.sin(freq * (_dot3(feats, w1_ref[...]) + b1_ref[...]))
    h = jnp.sin(freq * (_dot3(h, w2_ref[...]) + b2_ref[...]))
    h_hi, h_lo = _split(h)
    rev = rev_ref[...]
    hb_hi, hb_lo = _dot(rev, h_hi).astype(BF16), _dot(rev, h_lo).astype(BF16)
    hf_hi, hf_lo = h_hi[:tr], h_lo[:tr]
    delta = delta_ref[...]
    m = lax.broadcasted_iota(jnp.int32, (tr, 1), 0)
    pos_f = i * tr + m
    pos_b = i * tr + tr - m
    win_f = jnp.exp(-(pos_f.astype(F32) / float(seq - 1)) * delta)
    win_b = jnp.where(pos_b == seq, 0.0, jnp.exp(-(pos_b.astype(F32) / float(seq - 1)) * delta))
    parts = []
    for o in range(HY_ORDER):
        w_hi, w_lo = _split(w3_ref[:, 2 * o * width:(2 * o + 2) * width])
        wf_hi, wb_hi, wf_lo, wb_lo = w_hi[:, :width], w_hi[:, width:], w_lo[:, :width], w_lo[:, width:]
        kf = (_dot(hf_hi, wf_hi) + (_dot(hf_lo, wf_hi) + _dot(hf_hi, wf_lo))) * win_f
        kb = (_dot(hb_hi, wb_hi) + (_dot(hb_lo, wb_hi) + _dot(hb_hi, wb_lo))) * win_b
        lo_ref[:, o * width:(o + 1) * width] = kf.astype(lo_ref.dtype)
        hi_ref[:, o * width:(o + 1) * width] = kb.astype(hi_ref.dtype)
        parts.append(jnp.sum(jnp.abs(kf), axis=0, keepdims=True) + jnp.sum(jnp.abs(kb), axis=0, keepdims=True))
    part = jnp.concatenate(parts, axis=1)

    @pl.when(i == 0)
    def _():
        sum_ref[...] = part

    @pl.when(i > 0)
    def _():
        sum_ref[...] += part


def _hyena_filters(seq, width, w1, b1, w2, b2, w3, freq):
    tr = _tile(seq, 512)
    nb = seq // tr
    ext = tr + SUBLANES
    fw = w1.shape[1]
    bands = jnp.linspace(1e-4, HY_BANDS - 1, HY_BANDS, dtype=F32)
    bands = jnp.concatenate([bands, bands, jnp.zeros((FEAT_LANES - 2 * HY_BANDS,), F32)]).reshape(1, -1)
    w1 = w1.astype(F32)
    w1p = jnp.concatenate([w1[1:], w1[:1], jnp.zeros((FEAT_LANES - HY_EMB, fw), F32)], axis=0)
    deltas = jnp.abs(jnp.linspace(HY_MIN_DECAY, HY_MAX_DECAY, width, dtype=F32)).reshape(1, -1)
    row = lambda a: a.reshape(1, -1).astype(F32)
    const = lambda shape: pl.BlockSpec(shape, lambda i: (0, 0))
    rev = (jnp.arange(ext)[None, :] == tr - jnp.arange(tr)[:, None]).astype(BF16)
    half = jax.ShapeDtypeStruct((seq, HY_ORDER * width), BF16)
    return pl.pallas_call(
        functools.partial(_filter_kernel, seq=seq, tr=tr, width=width),
        grid=(nb,),
        in_specs=[
            const((1, FEAT_LANES)), const((FEAT_LANES, fw)), const((1, fw)), const((fw, fw)),
            const((1, fw)), const((1, fw)), const((fw, 2 * HY_ORDER * width)), const((1, width)),
            const((tr, ext)),
        ],
        out_specs=[pl.BlockSpec((tr, HY_ORDER * width), lambda i: (i, 0)),
                   pl.BlockSpec((tr, HY_ORDER * width), lambda i: (nb - 1 - i, 0)),
                   pl.BlockSpec((1, HY_ORDER * width), lambda i: (0, 0))],
        out_shape=[half, half, jax.ShapeDtypeStruct((1, HY_ORDER * width), F32)],
        compiler_params=_cparams("arbitrary"),
        name="hyena_filters",
    )(bands, w1p, row(b1), w2.astype(F32), row(b2), row(freq), w3.astype(F32), deltas, rev)


def _fft_tables(seq):
    n = 2 * seq
    n2 = DFT_INNER
    n1 = n // n2
    r = n1 // 2
    k1 = jnp.arange(n1, dtype=jnp.int32)
    th = (2.0 * math.pi / n1) * ((k1[:, None] * k1[None, :]) % n1).astype(F32)
    c, s = jnp.cos(th), jnp.sin(th)
    outer = jnp.concatenate([c, -s], axis=0)
    j = jnp.arange(n2, dtype=jnp.int32)
    ph = (2.0 * math.pi / n) * ((k1[:, None] * j[None, :]) % n).astype(F32)
    th2 = (2.0 * math.pi / n2) * ((j[:, None] * j[None, :]) % n2).astype(F32)
    cr, ci = jnp.cos(th2), -jnp.sin(th2)
    inner = jnp.concatenate([jnp.concatenate([cr, ci], axis=1),
                             jnp.concatenate([-ci, cr], axis=1)], axis=0)
    inv_inner = jnp.concatenate([jnp.concatenate([cr, -ci], axis=1),
                                 jnp.concatenate([ci, cr], axis=1)], axis=0) / n2
    return dict(n1=n1, n2=n2, outer=_split(outer), outer_half=_split(outer[:, :r]),
                inv_outer_re=_split(c[:r] / n1), inv_outer_im=_split(-s[:r] / n1),
                tw_re=jnp.cos(ph), tw_im=-jnp.sin(ph),
                inner=_split(inner), inv_inner=_split(inv_inner))


def _mat_dot(m_hi, m_lo, x, passes):
    x_hi = x.astype(BF16)
    out = _dot(m_hi, x_hi)
    if passes >= 2:
        out = out + _dot(m_hi, (x - x_hi.astype(F32)).astype(BF16))
    if passes >= 3:
        out = out + _dot(m_lo, x_hi)
    return out


def _dot_mat(x, m_hi, m_lo, passes):
    x_hi = x.astype(BF16)
    out = _dot(x_hi, m_hi)
    if passes >= 2:
        out = out + _dot((x - x_hi.astype(F32)).astype(BF16), m_hi)
    if passes >= 3:
        out = out + _dot(x_hi, m_lo)
    return out


def _forward_to_scratch(load, a_scr, f1h, f1l, tw_re, tw_im, n1, cb, passes):
    half = DFT_INNER
    for c in range(0, cb, 2):
        p = _mat_dot(f1h, f1l, jnp.concatenate([load(c), load(c + 1)], axis=1), passes)
        for q in range(2):
            ar, ai = p[:n1, q * half:(q + 1) * half], p[n1:, q * half:(q + 1) * half]
            a_scr[c + q, :, :half] = ar * tw_re - ai * tw_im
            a_scr[c + q, :, half:] = ar * tw_im + ai * tw_re


def _filter_spectrum_kernel(lo_ref, hi_ref, inv_ref, f1h_ref, f1l_ref, twr_ref, twi_ref, mh_ref, ml_ref,
                            kr_ref, ki_ref, a_scr, *, n1, cb):
    half = DFT_INNER
    load = lambda c: jnp.concatenate([lo_ref[c], hi_ref[c]], axis=0)
    _forward_to_scratch(load, a_scr, f1h_ref[...], f1l_ref[...], twr_ref[...], twi_ref[...], n1, cb, FFT_PASSES)
    x = _dot_mat(a_scr[...].reshape(cb * n1, 2 * half), mh_ref[...], ml_ref[...], FFT_PASSES)
    x = x.reshape(cb, n1, 2 * half)
    inv = inv_ref[...]
    kr_ref[...] = (x[:, :, :half] * inv).astype(kr_ref.dtype)
    ki_ref[...] = (x[:, :, half:] * inv).astype(ki_ref.dtype)


def _const_spec(arr):
    return pl.BlockSpec(arr.shape, lambda j: (0,) * arr.ndim)


def _filter_spectrum(lo_t, hi_t, inv_sum, tables):
    ch, rows, n2 = lo_t.shape
    n1 = 2 * rows
    cb = FFT_CB
    f1h, f1l = tables["outer"]
    mh, ml = tables["inner"]
    consts = [f1h, f1l, tables["tw_re"], tables["tw_im"], mh, ml]
    blk = pl.BlockSpec((cb, n1, n2), lambda j: (j, 0, 0))
    half_blk = pl.BlockSpec((cb, rows, n2), lambda j: (j, 0, 0))
    out = jax.ShapeDtypeStruct((ch, n1, n2), BF16)
    return pl.pallas_call(
        functools.partial(_filter_spectrum_kernel, n1=n1, cb=cb),
        grid=(ch // cb,),
        in_specs=[half_blk, half_blk, pl.BlockSpec((cb, 1, n2), lambda j: (j, 0, 0))]
        + [_const_spec(a) for a in consts],
        out_specs=[blk, blk],
        out_shape=[out, out],
        scratch_shapes=[pltpu.VMEM((cb, n1, 2 * n2), F32)],
        compiler_params=_cparams("parallel"),
        name="filter_spectrum",
    )(lo_t, hi_t, inv_sum, *consts)


def _fftconv_kernel(u_ref, gate_ref, kr_ref, ki_ref, skip_ref, f1h_ref, f1l_ref, twr_ref, twi_ref,
                    mh_ref, ml_ref, nh_ref, nl_ref, grh_ref, grl_ref, gih_ref, gil_ref,
                    o_ref, a_scr, *, n1, cb, passes):
    half = DFT_INNER
    tw_re, tw_im = twr_ref[...], twi_ref[...]
    _forward_to_scratch(lambda c: u_ref[c], a_scr, f1h_ref[...], f1l_ref[...], tw_re, tw_im, n1, cb, passes)
    x = _dot_mat(a_scr[...].reshape(cb * n1, 2 * half), mh_ref[...], ml_ref[...], passes)
    xr, xi = x[:, :half], x[:, half:]
    kr = kr_ref[...].astype(F32).reshape(cb * n1, half)
    ki = ki_ref[...].astype(F32).reshape(cb * n1, half)
    y = jnp.concatenate([xr * kr - xi * ki, xr * ki + xi * kr], axis=1)
    b = _dot_mat(y, nh_ref[...], nl_ref[...], passes)
    a_scr[...] = b.reshape(cb, n1, 2 * half)
    grh, grl, gih, gil = grh_ref[...], grl_ref[...], gih_ref[...], gil_ref[...]
    for c in range(0, cb, 2):
        re, im = [], []
        for q in range(2):
            br, bi = a_scr[c + q, :, :half], a_scr[c + q, :, half:]
            re.append(br * tw_re + bi * tw_im)
            im.append(bi * tw_re - br * tw_im)
        conv = (_mat_dot(grh, grl, jnp.concatenate(re, axis=1), passes)
                + _mat_dot(gih, gil, jnp.concatenate(im, axis=1), passes))
        for q in range(2):
            cq = conv[:, q * half:(q + 1) * half]
            o_ref[c + q] = (gate_ref[c + q] * (cq + u_ref[c + q] * skip_ref[c + q])).astype(o_ref.dtype)


def _fftconv(u_t, u_part, gate_t, gate_part, kr, ki, order, skip_t, tables, width, out_dtype):
    n1, n2 = tables["n1"], tables["n2"]
    rows = n1 // 2
    cb = FFT_CB
    nb = width // cb
    consts = [*tables["outer_half"], tables["tw_re"], tables["tw_im"], *tables["inner"],
              *tables["inv_inner"], *tables["inv_outer_re"], *tables["inv_outer_im"]]
    data = lambda part: pl.BlockSpec((cb, rows, n2), lambda j: (part * nb + j, 0, 0))
    spec = pl.BlockSpec((cb, n1, n2), lambda j: (order * nb + j, 0, 0))
    return pl.pallas_call(
        functools.partial(_fftconv_kernel, n1=n1, cb=cb, passes=FFT_PASSES),
        grid=(nb,),
        in_specs=[data(u_part), data(gate_part), spec, spec,
                  pl.BlockSpec((cb, 1, n2), lambda j: (j, 0, 0))] + [_const_spec(a) for a in consts],
        out_specs=pl.BlockSpec((cb, rows, n2), lambda j: (j, 0, 0)),
        out_shape=jax.ShapeDtypeStruct((width, rows, n2), out_dtype),
        scratch_shapes=[pltpu.VMEM((cb, n1, 2 * n2), F32)],
        compiler_params=_cparams("parallel"),
        name="hyena_fftconv",
    )(u_t, gate_t, kr, ki, skip_t, *consts)


def _lane_rows(v):
    return jnp.broadcast_to(v.astype(F32).reshape(-1, 1, 1), (v.shape[0], 1, DFT_INNER))


def _hyena_mixer(z, p, tables, seq, width):
    n1, n2 = tables["n1"], tables["n2"]
    rows = n1 // 2
    zc = _hyena_conv(z, p["hy_conv_w"], p["hy_conv_b"], seq, width)
    zc_t = jnp.transpose(zc, (0, 2, 1)).reshape(3 * width, rows, n2)
    k_lo, k_hi, colsum = _hyena_filters(seq, width, p["hy_filt_w1"], p["hy_filt_b1"], p["hy_filt_w2"],
                                        p["hy_filt_b2"], p["hy_filt_w3"], p["hy_filt_freq"])
    to_channels = lambda a: a.T.reshape(HY_ORDER * width, rows, n2)
    kr, ki = _filter_spectrum(to_channels(k_lo), to_channels(k_hi), _lane_rows(1.0 / colsum.reshape(-1)), tables)
    skip = p["hy_skip"]
    y1 = _fftconv(zc_t, 0, zc_t, 1, kr, ki, 0, _lane_rows(skip[0]), tables, width, F32)
    y = _fftconv(y1, 0, zc_t, 2, kr, ki, 1, _lane_rows(skip[1]), tables, width, BF16)
    return y.reshape(width, seq).T


def _attn_kernel(q_ref, kp_ref, kc_ref, kn_ref, vp_ref, vc_ref, vn_ref, sink_ref, o_ref, *, seq, group):
    i = pl.program_id(0)
    h = pl.program_id(1)
    blk = WINDOW
    k_all = jnp.concatenate([kp_ref[...], kc_ref[...], kn_ref[...]], axis=0)
    v_all = jnp.concatenate([vp_ref[...], vc_ref[...], vn_ref[...]], axis=0)
    row = lax.broadcasted_iota(jnp.int32, (blk, 3 * blk), 0)
    col = lax.broadcasted_iota(jnp.int32, (blk, 3 * blk), 1)
    dist = jnp.abs(row + blk - col)
    distf = dist.astype(F32)
    for qb in range(ATTN_BLOCKS):
        rows = slice(qb * blk, (qb + 1) * blk)
        q = q_ref[rows, :]
        q4 = jnp.concatenate([q[:, g * HEAD_DIM:(g + 1) * HEAD_DIM] for g in range(group)], axis=0)
        k = k_all[qb * blk:(qb + 3) * blk]
        v = v_all[qb * blk:(qb + 3) * blk]
        s = lax.dot_general(q4, k, (((1,), (1,)), ((), ())), preferred_element_type=F32)
        s = s * (HEAD_DIM ** -0.5)
        k_pos = (i * ATTN_BLOCKS + qb - 1) * blk + col
        valid = (dist <= WINDOW) & (k_pos >= 0) & (k_pos < seq)
        outs = []
        for g in range(group):
            sg = s[g * blk:(g + 1) * blk] - sink_ref[KV_HEADS + h, g] * distf
            sg = jnp.where(valid, sg, NEG)
            sink = sink_ref[h, g]
            m = jnp.maximum(jnp.max(sg, axis=-1, keepdims=True), sink)
            e = jnp.exp(sg - m)
            denom = jnp.sum(e, axis=-1, keepdims=True) + jnp.exp(sink - m)
            pg = (e / denom).astype(v.dtype)
            outs.append(_dot(pg, v))
        o_ref[rows, :] = jnp.concatenate(outs, axis=1).astype(o_ref.dtype)


def _attention(z, sink, seq, width, q_off, k_off, v_off):
    group = width // HEAD_DIM // KV_HEADS
    gw = group * HEAD_DIM
    ab = ATTN_BLOCKS
    rows = ab * WINDOW
    nb = seq // WINDOW
    qb, kb, vb = q_off // gw, k_off // HEAD_DIM, v_off // HEAD_DIM
    assert q_off % gw == 0 and k_off % HEAD_DIM == 0 and v_off % HEAD_DIM == 0 and seq % rows == 0
    halo = lambda base, f: pl.BlockSpec((WINDOW, HEAD_DIM), lambda i, h: (f(i), base + h))
    prev = lambda i: jnp.maximum(i * ab - 1, 0)
    nxt = lambda i: jnp.minimum((i + 1) * ab, nb - 1)
    main = lambda base: pl.BlockSpec((rows, HEAD_DIM), lambda i, h: (i, base + h))
    n_q = KV_HEADS * group
    slopes = 2.0 ** (-8.0 * (jnp.arange(n_q, dtype=F32) + 1.0) / n_q)
    scalars = jnp.concatenate([sink.astype(F32), slopes]).reshape(2 * KV_HEADS, group)
    return pl.pallas_call(
        functools.partial(_attn_kernel, seq=seq, group=group),
        grid=(seq // rows, KV_HEADS),
        in_specs=[pl.BlockSpec((rows, gw), lambda i, h: (i, qb + h)),
                  halo(kb, prev), main(kb), halo(kb, nxt),
                  halo(vb, prev), main(vb), halo(vb, nxt),
                  pl.BlockSpec(memory_space=pltpu.SMEM)],
        out_specs=pl.BlockSpec((rows, gw), lambda i, h: (i, h)),
        out_shape=jax.ShapeDtypeStruct((seq, width), BF16),
        compiler_params=_cparams("parallel", "parallel"),
        name="window_attention",
    )(z, z, z, z, z, z, z, scalars)


def _retention_kernel(*refs, heads, reverse, final, cps):
    if final:
        cd_ref, dec_ref, kw_ref, qw_ref, q_ref, k_ref, v_ref, prev_ref, g_ref, o_ref, s_ref = refs
    else:
        cd_ref, dec_ref, kw_ref, qw_ref, q_ref, k_ref, v_ref, o_ref, s_ref = refs
    hg = pl.program_id(0)
    n = pl.program_id(1)
    c = RET_CHUNK

    @pl.when(n == 0)
    def _():
        s_ref[...] = jnp.zeros_like(s_ref)

    chunk_order = range(cps - 1, -1, -1) if reverse else range(cps)
    for hd in range(heads):
        sl = slice(hd * HEAD_DIM, (hd + 1) * HEAD_DIM)
        decay, k_w, q_w = dec_ref[hd], kw_ref[hd], qw_ref[hd]
        chunk_decay = cd_ref[hg * heads + hd]
        state = s_ref[hd]
        for cc in chunk_order:
            rows = slice(cc * c, (cc + 1) * c)
            q, k, v = q_ref[rows, sl], k_ref[rows, sl], v_ref[rows, sl]
            qk = lax.dot_general(q, k, (((1,), (1,)), ((), ())), preferred_element_type=F32)
            inner = _dot((qk * decay).astype(v.dtype), v)
            cross = _dot((q.astype(F32) * q_w).astype(BF16), state.astype(BF16))
            kv = lax.dot_general((k.astype(F32) * k_w).astype(BF16), v, (((0,), (0,)), ((), ())),
                                 preferred_element_type=F32)
            state = state * chunk_decay + kv
            o = inner + cross
            if final:
                o = o + prev_ref[rows, sl]
                o = o * lax.rsqrt(jnp.mean(o * o, axis=-1, keepdims=True) + EPS)
                g = g_ref[rows, sl].astype(F32)
                o = g * _sigmoid(g) * o
            o_ref[rows, sl] = o.astype(o_ref.dtype)
        s_ref[hd] = state


def _retention_pass(z, seq, width, offs, reverse, prev=None):
    c = RET_CHUNK
    cps = RET_CHUNKS_PER_STEP
    rows = cps * c
    nb = seq // rows
    gw = math.gcd(width, *offs)
    assert gw % HEAD_DIM == 0 and seq % rows == 0
    heads = gw // HEAD_DIM
    q_off, k_off, v_off, g_off = offs
    hidx = jnp.arange(width // HEAD_DIM, dtype=F32)
    idx = jnp.arange(c, dtype=F32)
    rel = idx[:, None] - idx[None, :]
    scale = HEAD_DIM ** -0.5
    if reverse:
        lg = jnp.log(1.0 - 2.0 ** (-5.5 - hidx))[:, None]
        order = lambda n: nb - 1 - n
        rel, k_pow, q_pow = -rel, idx, c - idx
    else:
        lg = jnp.log(1.0 - 2.0 ** (-5.0 - hidx))[:, None]
        order = lambda n: n
        k_pow, q_pow = c - 1.0 - idx, idx + 1.0
    decay = jnp.where(rel >= 0, jnp.exp(jnp.maximum(rel, 0.0)[None] * lg[:, :, None]), 0.0) * scale
    lanes = lambda a: jnp.broadcast_to(a[:, :, None], a.shape + (HEAD_DIM,))
    k_w = lanes(jnp.exp(k_pow[None] * lg) * scale)
    q_w = lanes(jnp.exp(q_pow[None] * lg))
    chunk_decay = jnp.exp(c * lg[:, 0])
    final = prev is not None
    zspec = lambda off: pl.BlockSpec((rows, gw), lambda g, n: (order(n), off // gw + g))
    own = pl.BlockSpec((rows, gw), lambda g, n: (order(n), g))
    table = pl.BlockSpec((heads, c, HEAD_DIM), lambda g, n: (g, 0, 0))
    in_specs = [pl.BlockSpec(memory_space=pltpu.SMEM), table, table, table,
                zspec(q_off), zspec(k_off), zspec(v_off)]
    args = [chunk_decay, decay, k_w, q_w, z, z, z]
    if final:
        in_specs += [own, zspec(g_off)]
        args += [prev, z]
    return pl.pallas_call(
        functools.partial(_retention_kernel, heads=heads, reverse=reverse, final=final, cps=cps),
        grid=(width // gw, nb),
        in_specs=in_specs,
        out_specs=own,
        out_shape=jax.ShapeDtypeStruct((seq, width), BF16 if final else F32),
        scratch_shapes=[pltpu.VMEM((heads, HEAD_DIM, HEAD_DIM), F32)],
        compiler_params=_cparams("parallel", "arbitrary"),
        name="retention_fwd" if final else "retention_bwd",
    )(*args)


def _retention(z, seq, width, offs):
    o_bwd = _retention_pass(z, seq, width, offs, reverse=True)
    return _retention_pass(z, seq, width, offs, reverse=False, prev=o_bwd)


def _merge_kernel(ya_ref, yb_ref, yc_ref, wa_ref, wb_ref, wc_ref, ga_ref, gb_ref, gc_ref, o_ref):
    acc = None
    for y_ref, w_ref, g_ref in ((ya_ref, wa_ref, ga_ref), (yb_ref, wb_ref, gb_ref), (yc_ref, wc_ref, gc_ref)):
        t = _sigmoid(g_ref[...].astype(F32)) * _dot(y_ref[...], w_ref[...])
        acc = t if acc is None else acc + t
    o_ref[...] = acc.astype(o_ref.dtype)


def _merge(ya, yb, yc, w_branch, layer, z, gate_off, d_model):
    seq, width = ya.shape
    tm, tn = _tile(seq, 1024), _tile(d_model, 512)
    assert gate_off % tn == 0
    ys = pl.BlockSpec((tm, width), lambda i, j: (i, 0))
    ws = lambda b: pl.BlockSpec((None, None, width, tn), lambda i, j: (layer, b, 0, j))
    gs = lambda b: pl.BlockSpec((tm, tn), lambda i, j: (i, (gate_off + b * d_model) // tn + j))
    return pl.pallas_call(
        _merge_kernel,
        grid=(seq // tm, d_model // tn),
        in_specs=[ys, ys, ys, ws(0), ws(1), ws(2), gs(0), gs(1), gs(2)],
        out_specs=pl.BlockSpec((tm, tn), lambda i, j: (i, j)),
        out_shape=jax.ShapeDtypeStruct((seq, d_model), BF16),
        compiler_params=_cparams("parallel", "parallel"),
        name="branch_merge",
    )(ya, yb, yc, w_branch, w_branch, w_branch, z, z, z)


def _ffn_in_kernel(h_ref, hp_ref, hn_ref, s_ref, sp_ref, sn_ref, wg_ref, wu_ref, cw_ref, cb_ref, *rest,
                   n_tiles, d):
    o_ref = rest[-1]
    i = pl.program_id(0)
    first = (i > 0).astype(F32)
    last = (i < n_tiles - 1).astype(F32)
    r = _row_scale(s_ref[...], d)
    wg = wg_ref[...]
    g = _dot(h_ref[...], wg) * r
    prev_row = _dot(hp_ref[...], wg)[HALO - 1:HALO] * (_row_scale(sp_ref[...], d)[SUBLANES - 1:SUBLANES] * first)
    next_row = _dot(hn_ref[...], wg)[0:1] * (_row_scale(sn_ref[...], d)[0:1] * last)
    down, up = _shift_rows(g, prev_row, next_row)
    cw = cw_ref[...]
    c = down * cw[0:1] + g * cw[1:2] + up * cw[2:3] + cb_ref[...]
    gelu = 0.5 * c * (1.0 + lax.erf(c * (2.0 ** -0.5)))
    o_ref[...] = (gelu * (_dot(h_ref[...], wu_ref[...]) * r)).astype(o_ref.dtype)


def _ffn_in_span(h, ssq, w_gate, w_up, conv_w, conv_b, layer, tn, col0, n_col_tiles, prev_out):
    seq, d = h.shape
    ff = w_gate.shape[2]
    tm = _tile(seq, 1024)
    n_tiles, n_halo, per = seq // tm, seq // HALO, tm // HALO
    n_sub, per_sub = seq // SUBLANES, tm // SUBLANES
    assert col0 % tn == 0
    j0 = col0 // tn
    in_specs = [
        pl.BlockSpec((tm, d), lambda i, j: (i, 0)),
        pl.BlockSpec((HALO, d), lambda i, j: (jnp.maximum(i * per - 1, 0), 0)),
        pl.BlockSpec((HALO, d), lambda i, j: (jnp.minimum((i + 1) * per, n_halo - 1), 0)),
        pl.BlockSpec((tm, LANES), lambda i, j: (i, 0)),
        pl.BlockSpec((SUBLANES, LANES), lambda i, j: (jnp.maximum(i * per_sub - 1, 0), 0)),
        pl.BlockSpec((SUBLANES, LANES), lambda i, j: (jnp.minimum((i + 1) * per_sub, n_sub - 1), 0)),
        pl.BlockSpec((None, d, tn), lambda i, j: (layer, 0, j0 + j)),
        pl.BlockSpec((None, d, tn), lambda i, j: (layer, 0, j0 + j)),
        pl.BlockSpec((None, 3, tn), lambda i, j: (layer, 0, j0 + j)),
        pl.BlockSpec((None, 1, tn), lambda i, j: (layer, 0, j0 + j)),
    ]
    args = [h, h, h, ssq, ssq, ssq, w_gate, w_up, conv_w, conv_b]
    aliases = {}
    if prev_out is not None:
        in_specs.append(pl.BlockSpec(memory_space=pl.ANY))
        args.append(prev_out)
        aliases = {len(args) - 1: 0}
    return pl.pallas_call(
        functools.partial(_ffn_in_kernel, n_tiles=n_tiles, d=d),
        grid=(n_tiles, n_col_tiles),
        in_specs=in_specs,
        out_specs=pl.BlockSpec((tm, tn), lambda i, j: (i, j0 + j)),
        out_shape=jax.ShapeDtypeStruct((seq, ff), BF16),
        input_output_aliases=aliases,
        compiler_params=_cparams("parallel", "parallel"),
        name="ffn_in",
    )(*args)


def _ffn_in(h, ssq, w_gate, w_up, conv_w, conv_b, layer):
    ff = w_gate.shape[2]
    tn = min(FFN_TILE, ff)
    main = (ff // tn) * tn
    out = _ffn_in_span(h, ssq, w_gate, w_up, conv_w, conv_b, layer, tn, 0, main // tn, None)
    if main < ff:
        tail = math.gcd(ff - main, main)
        assert tail % LANES == 0
        out = _ffn_in_span(h, ssq, w_gate, w_up, conv_w, conv_b, layer, tail, main, (ff - main) // tail, out)
    return out


MATMUL_WEIGHTS = ("w_in", "w_branch", "w_out", "w_ffn_in", "w_ffn_gate", "w_ffn_up", "w_ffn_out",
                  "ffn_conv_w", "ffn_conv_b")


def _prepare_weights(w):
    ff = w["w_ffn_out"].shape[1]
    w_ffn_in = w["w_ffn_in"]
    return dict(
        w_in=w["w_in"].astype(BF16),
        w_branch=w["w_branch"].astype(BF16),
        w_out=w["w_out"].astype(BF16),
        w_ffn_gate=w_ffn_in[..., :ff].astype(BF16),
        w_ffn_up=w_ffn_in[..., ff:].astype(BF16),
        w_ffn_out=w["w_ffn_out"].astype(BF16),
        ffn_conv_w=w["ffn_conv_w"].astype(F32),
        ffn_conv_b=w["ffn_conv_b"].astype(F32)[:, None, :],
    )


def _layer(x, l, p, tables, normed, next_gain):
    seq, d = x.shape
    width = d // 4
    kvw = KV_HEADS * HEAD_DIM
    aq, r_offs = 3 * width, [4 * width, 5 * width, 6 * width, 7 * width]
    ak, av, gates = 8 * width, 8 * width + kvw, 8 * width + 2 * kvw
    tn = 2 * kvw
    assert (4 * width) % tn == 0
    kv_old, kv_new, r_end = 4 * width // tn, 8 * width // tn, (8 * width + 2 * kvw) // tn

    def z_tile(j):
        return jnp.where(j < kv_old, j, jnp.where(j == kv_old, kv_new, jnp.where(j < r_end, j - 1, j)))

    small = {name: arr[l] for name, arr in p.items() if name not in MATMUL_WEIGHTS}
    if normed is None:
        h, ssq = _rmsnorm(x, small["norm_mix"], BF16), None
    else:
        h, ssq = normed
    z = _matmul(h, p["w_in"], l, tm=2048, tn=tn, out_dtype=BF16, ssq=ssq, out_col=z_tile, name="in_proj")
    y_a = _hyena_mixer(z, small, tables, seq, width)
    y_b = _attention(z, small["attn_sink"], seq, width, aq, ak, av)
    y_c = _retention(z, seq, width, r_offs)
    merged = _merge(y_a, y_b, y_c, p["w_branch"], l, z, gates, d)
    x, h, ssq = _matmul(merged, p["w_out"], l, tm=1024, tn=512, out_dtype=F32, residual=x,
                        next_gain=small["norm_ffn"], name="out_proj")
    act = _ffn_in(h, ssq, p["w_ffn_gate"], p["w_ffn_up"], p["ffn_conv_w"], p["ffn_conv_b"], l)
    ffn_out = functools.partial(_matmul, act, p["w_ffn_out"], l, tm=512, tn=512,
                                out_dtype=F32, residual=x, name="ffn_out")
    if next_gain is None:
        return ffn_out(), None
    x, h, ssq = ffn_out(next_gain=next_gain)
    return x, (h, ssq)


def _trunk(x, p, norm_final):
    b, seq, d = x.shape
    assert b == 1
    tables = _fft_tables(seq)
    x = x.reshape(seq, d)
    depth = p["w_in"].shape[0]
    normed = None
    for l in range(depth):
        next_gain = p["norm_mix"][l + 1] if l + 1 < depth else None
        x, normed = _layer(x, l, p, tables, normed, next_gain)
    return _rmsnorm(x, norm_final, F32).reshape(b, seq, d)


def kernel(x_prompt, x_sample, norm_mix, w_in, hy_conv_w, hy_conv_b, hy_filt_w1, hy_filt_b1, hy_filt_w2, hy_filt_b2, hy_filt_w3, hy_filt_freq, hy_skip, attn_sink, w_branch, w_out, norm_ffn, w_ffn_in, ffn_conv_w, ffn_conv_b, w_ffn_out, norm_final):
    w = dict(norm_mix=norm_mix, w_in=w_in, hy_conv_w=hy_conv_w, hy_conv_b=hy_conv_b,
             hy_filt_w1=hy_filt_w1, hy_filt_b1=hy_filt_b1, hy_filt_w2=hy_filt_w2,
             hy_filt_b2=hy_filt_b2, hy_filt_w3=hy_filt_w3, hy_filt_freq=hy_filt_freq,
             hy_skip=hy_skip, attn_sink=attn_sink, w_branch=w_branch, w_out=w_out,
             norm_ffn=norm_ffn, w_ffn_in=w_ffn_in, ffn_conv_w=ffn_conv_w,
             ffn_conv_b=ffn_conv_b, w_ffn_out=w_ffn_out)
    p = dict(w)
    p.update(_prepare_weights(w))
    return (_trunk(x_prompt, p, norm_final), _trunk(x_sample, p, norm_final))
```

```python
import functools
import math

import jax
import jax.numpy as jnp
from jax import lax
from jax.experimental import pallas as pl
from jax.experimental.pallas import tpu as pltpu

F32 = jnp.float32
BF16 = jnp.bfloat16

LANES = 128
SUBLANES = 8
EPS = 1e-6
NEG = -1e30
HEAD_DIM = 128
KV_HEADS = 2
WINDOW = 128
ATTN_BLOCKS = 8
RET_CHUNK = 128
RET_CHUNKS_PER_STEP = 4
HY_ORDER = 2
HY_EMB = 33
HY_BANDS = (HY_EMB - 1) // 2
HY_MIN_DECAY = math.log(1e-2) / 1.5
HY_MAX_DECAY = math.log(1e-2) / 0.3
DFT_INNER = 128
FFT_CB = 32
FFT_PASSES = 1
FEAT_LANES = 128
HALO = 16
FFN_TILE = 512
VMEM_LIMIT = 56 * 1024 * 1024


def _cparams(*sem, vmem=VMEM_LIMIT):
    return pltpu.CompilerParams(dimension_semantics=sem, vmem_limit_bytes=vmem)


def _tile(n, pref):
    if n <= pref:
        return n
    t = (pref // 128) * 128
    while t >= 128:
        if n % t == 0:
            return t
        t -= 128
    return n


def _dot(a, b):
    return jnp.dot(a, b, preferred_element_type=F32)


def _sigmoid(x):
    return 0.5 * jnp.tanh(0.5 * x) + 0.5


def _split(x):
    hi = x.astype(BF16)
    lo = (x - hi.astype(F32)).astype(BF16)
    return hi, lo


def _dot3(a, b):
    a_hi, a_lo = _split(a)
    b_hi, b_lo = _split(b)
    return _dot(a_hi, b_hi) + (_dot(a_lo, b_hi) + _dot(a_hi, b_lo))


def _rmsnorm_kernel(x_ref, g_ref, o_ref):
    x = x_ref[...]
    ms = jnp.mean(x * x, axis=-1, keepdims=True)
    o_ref[...] = (x * lax.rsqrt(ms + EPS) * g_ref[...]).astype(o_ref.dtype)


def _rmsnorm(x, g, out_dtype):
    m, d = x.shape
    tm = _tile(m, 256)
    return pl.pallas_call(
        _rmsnorm_kernel,
        grid=(m // tm,),
        in_specs=[pl.BlockSpec((tm, d), lambda i: (i, 0)),
                  pl.BlockSpec((1, d), lambda i: (0, 0))],
        out_specs=pl.BlockSpec((tm, d), lambda i: (i, 0)),
        out_shape=jax.ShapeDtypeStruct((m, d), out_dtype),
        compiler_params=_cparams("parallel"),
        name="rmsnorm",
    )(x, g.reshape(1, d).astype(F32))


def _row_scale(ssq, d):
    return lax.rsqrt(ssq[:, 0:1] * (1.0 / d) + EPS)


def _matmul_kernel(*refs, nk, has_res, in_scale_d, norm_out):
    refs = list(refs)
    a_ref, b_ref = refs[:2]
    pos = 2
    s_ref = r_ref = gn_ref = xg_ref = so_ref = None
    if in_scale_d:
        s_ref, pos = refs[pos], pos + 1
    if has_res:
        r_ref, pos = refs[pos], pos + 1
    if norm_out:
        gn_ref, pos = refs[pos], pos + 1
    o_ref, pos = refs[pos], pos + 1
    if norm_out:
        xg_ref, so_ref = refs[pos], refs[pos + 1]
    p = _dot(a_ref[...], b_ref[...])
    j = pl.program_id(1)

    def finish(acc):
        if in_scale_d:
            acc = acc * _row_scale(s_ref[...], in_scale_d)
        if has_res:
            acc = acc + r_ref[...]
        o_ref[...] = acc.astype(o_ref.dtype)
        if norm_out:
            xg_ref[...] = (acc * gn_ref[...]).astype(xg_ref.dtype)
            part = jnp.broadcast_to(jnp.sum(acc * acc, axis=1, keepdims=True), so_ref.shape)

            @pl.when(j == 0)
            def _():
                so_ref[...] = part

            @pl.when(j > 0)
            def _():
                so_ref[...] += part

    if nk == 1:
        finish(p)
        return
    acc_ref = refs[-1]
    k = pl.program_id(2)

    @pl.when(k == 0)
    def _():
        acc_ref[...] = p

    @pl.when(k > 0)
    def _():
        acc_ref[...] += p

    @pl.when(k == nk - 1)
    def _():
        finish(acc_ref[...])


def _matmul(a, b, layer, *, tm, tn, tk=None, out_dtype, ssq=None, residual=None, next_gain=None,
            out_col=lambda j: j, name):
    m, kdim = a.shape
    n = b.shape[2]
    tm, tn = _tile(m, tm), _tile(n, tn)
    tk = kdim if tk is None else _tile(kdim, tk)
    nk = kdim // tk
    in_specs = [pl.BlockSpec((tm, tk), lambda i, j, k: (i, k)),
                pl.BlockSpec((None, tk, tn), lambda i, j, k: (layer, k, j))]
    args = [a, b]
    row_stat = pl.BlockSpec((tm, LANES), lambda i, j, k: (i, 0))
    tile = pl.BlockSpec((tm, tn), lambda i, j, k: (i, j))
    if ssq is not None:
        in_specs.append(row_stat)
        args.append(ssq)
    if residual is not None:
        in_specs.append(tile)
        args.append(residual)
    out_specs = pl.BlockSpec((tm, tn), lambda i, j, k: (i, out_col(j)))
    out_shape = jax.ShapeDtypeStruct((m, n), out_dtype)
    if next_gain is not None:
        in_specs.append(pl.BlockSpec((1, tn), lambda i, j, k: (0, j)))
        args.append(next_gain.reshape(1, n).astype(F32))
        out_specs = [out_specs, tile, row_stat]
        out_shape = [out_shape, jax.ShapeDtypeStruct((m, n), BF16), jax.ShapeDtypeStruct((m, LANES), F32)]
    scratch = [pltpu.VMEM((tm, tn), F32)] if nk > 1 else []
    return pl.pallas_call(
        functools.partial(_matmul_kernel, nk=nk, has_res=residual is not None,
                          in_scale_d=kdim if ssq is not None else 0, norm_out=next_gain is not None),
        grid=(m // tm, n // tn, nk),
        in_specs=in_specs,
        out_specs=out_specs,
        out_shape=out_shape,
        scratch_shapes=scratch,
        compiler_params=_cparams("parallel", "arbitrary", "arbitrary"),
        name=name,
    )(*args)


def _shift_rows(x, prev_row, next_row):
    tm = x.shape[0]
    row = lax.broadcasted_iota(jnp.int32, x.shape, 0)
    down = jnp.where(row == 0, prev_row, pltpu.roll(x, 1, axis=0))
    up = jnp.where(row == tm - 1, next_row, pltpu.roll(x, tm - 1, axis=0))
    return down, up


def _hyconv_kernel(x_ref, xp_ref, xn_ref, w_ref, b_ref, o_ref, *, n_tiles):
    i = pl.program_id(0)
    x = x_ref[...].astype(F32)
    prev_row = xp_ref[...].astype(F32)[HALO - 1:HALO] * (i > 0).astype(F32)
    next_row = xn_ref[...].astype(F32)[0:1] * (i < n_tiles - 1).astype(F32)
    down, up = _shift_rows(x, prev_row, next_row)
    w = w_ref[...]
    o_ref[...] = (down * w[0:1] + x * w[1:2] + up * w[2:3] + b_ref[...]).astype(o_ref.dtype)


def _hyena_conv(z, conv_w, conv_b, seq, width):
    tm = _tile(seq, 512)
    n_tiles = seq // tm
    n_halo = seq // HALO
    per = tm // HALO
    return pl.pallas_call(
        functools.partial(_hyconv_kernel, n_tiles=n_tiles),
        grid=(n_tiles, 3),
        in_specs=[
            pl.BlockSpec((tm, width), lambda i, p: (i, p)),
            pl.BlockSpec((HALO, width), lambda i, p: (jnp.maximum(i * per - 1, 0), p)),
            pl.BlockSpec((HALO, width), lambda i, p: (jnp.minimum((i + 1) * per, n_halo - 1), p)),
            pl.BlockSpec((3, width), lambda i, p: (0, p)),
            pl.BlockSpec((1, width), lambda i, p: (0, p)),
        ],
        out_specs=pl.BlockSpec((None, tm, width), lambda i, p: (p, i, 0)),
        out_shape=jax.ShapeDtypeStruct((3, seq, width), BF16),
        compiler_params=_cparams("parallel", "parallel"),
        name="hyena_conv3",
    )(z, z, z, conv_w.astype(F32), conv_b.reshape(1, -1).astype(F32))


def _filter_kernel(bands_ref, w1_ref, b1_ref, w2_ref, b2_ref, freq_ref, w3_ref, delta_ref, rev_ref,
                   lo_ref, hi_ref, sum_ref, *, seq, tr, width):
    i = pl.program_id(0)
    ext = tr + SUBLANES
    pos = (i * tr + lax.broadcasted_iota(jnp.int32, (ext, FEAT_LANES), 0)).astype(F32)
    t = pos / float(seq - 1)
    lane = lax.broadcasted_iota(jnp.int32, (ext, FEAT_LANES), 1)
    ang = (2.0 * math.pi / seq) * pos * bands_ref[...]
    quarter = jnp.where((lane >= HY_BANDS) & (lane < 2 * HY_BANDS), 0.5 * math.pi, 0.0)
    feats = jnp.where(lane < 2 * HY_BANDS, jnp.cos(ang + quarter),
                      jnp.where(lane == 2 * HY_BANDS, t, 0.0))
    freq = freq_ref[...]
    h = jnp.sin(freq * (_dot3(feats, w1_ref[...]) + b1_ref[...]))
    h = jnp.sin(freq * (_dot3(h, w2_ref[...]) + b2_ref[...]))
    h_hi, h_lo = _split(h)
    rev = rev_ref[...]
    hb_hi, hb_lo = _dot(rev, h_hi).astype(BF16), _dot(rev, h_lo).astype(BF16)
    hf_hi, hf_lo = h_hi[:tr], h_lo[:tr]
    delta = delta_ref[...]
    m = lax.broadcasted_iota(jnp.int32, (tr, 1), 0)
    pos_f = i * tr + m
    pos_b = i * tr + tr - m
    win_f = jnp.exp(-(pos_f.astype(F32) / float(seq - 1)) * delta)
    win_b = jnp.where(pos_b == seq, 0.0, jnp.exp(-(pos_b.astype(F32) / float(seq - 1)) * delta))
    parts = []
    for o in range(HY_ORDER):
        w_hi, w_lo = _split(w3_ref[:, 2 * o * width:(2 * o + 2) * width])
        wf_hi, wb_hi, wf_lo, wb_lo = w_hi[:, :width], w_hi[:, width:], w_lo[:, :width], w_lo[:, width:]
        kf = (_dot(hf_hi, wf_hi) + (_dot(hf_lo, wf_hi) + _dot(hf_hi, wf_lo))) * win_f
        kb = (_dot(hb_hi, wb_hi) + (_dot(hb_lo, wb_hi) + _dot(hb_hi, wb_lo))) * win_b
        lo_ref[:, o * width:(o + 1) * width] = kf.astype(lo_ref.dtype)
        hi_ref[:, o * width:(o + 1) * width] = kb.astype(hi_ref.dtype)
        parts.append(jnp.sum(jnp.abs(kf), axis=0, keepdims=True) + jnp.sum(jnp.abs(kb), axis=0, keepdims=True))
    part = jnp.concatenate(parts, axis=1)

    @pl.when(i == 0)
    def _():
        sum_ref[...] = part

    @pl.when(i > 0)
    def _():
        sum_ref[...] += part


def _hyena_filters(seq, width, w1, b1, w2, b2, w3, freq):
    tr = _tile(seq, 512)
    nb = seq // tr
    ext = tr + SUBLANES
    fw = w1.shape[1]
    bands = jnp.linspace(1e-4, HY_BANDS - 1, HY_BANDS, dtype=F32)
    bands = jnp.concatenate([bands, bands, jnp.zeros((FEAT_LANES - 2 * HY_BANDS,), F32)]).reshape(1, -1)
    w1 = w1.astype(F32)
    w1p = jnp.concatenate([w1[1:], w1[:1], jnp.zeros((FEAT_LANES - HY_EMB, fw), F32)], axis=0)
    deltas = jnp.abs(jnp.linspace(HY_MIN_DECAY, HY_MAX_DECAY, width, dtype=F32)).reshape(1, -1)
    row = lambda a: a.reshape(1, -1).astype(F32)
    const = lambda shape: pl.BlockSpec(shape, lambda i: (0, 0))
    rev = (jnp.arange(ext)[None, :] == tr - jnp.arange(tr)[:, None]).astype(BF16)
    half = jax.ShapeDtypeStruct((seq, HY_ORDER * width), BF16)
    return pl.pallas_call(
        functools.partial(_filter_kernel, seq=seq, tr=tr, width=width),
        grid=(nb,),
        in_specs=[
            const((1, FEAT_LANES)), const((FEAT_LANES, fw)), const((1, fw)), const((fw, fw)),
            const((1, fw)), const((1, fw)), const((fw, 2 * HY_ORDER * width)), const((1, width)),
            const((tr, ext)),
        ],
        out_specs=[pl.BlockSpec((tr, HY_ORDER * width), lambda i: (i, 0)),
                   pl.BlockSpec((tr, HY_ORDER * width), lambda i: (nb - 1 - i, 0)),
                   pl.BlockSpec((1, HY_ORDER * width), lambda i: (0, 0))],
        out_shape=[half, half, jax.ShapeDtypeStruct((1, HY_ORDER * width), F32)],
        compiler_params=_cparams("arbitrary"),
        name="hyena_filters",
    )(bands, w1p, row(b1), w2.astype(F32), row(b2), row(freq), w3.astype(F32), deltas, rev)


def _fft_tables(seq):
    n = 2 * seq
    n2 = DFT_INNER
    n1 = n // n2
    r = n1 // 2
    k1 = jnp.arange(n1, dtype=jnp.int32)
    th = (2.0 * math.pi / n1) * ((k1[:, None] * k1[None, :]) % n1).astype(F32)
    c, s = jnp.cos(th), jnp.sin(th)
    outer = jnp.concatenate([c, -s], axis=0)
    j = jnp.arange(n2, dtype=jnp.int32)
    ph = (2.0 * math.pi / n) * ((k1[:, None] * j[None, :]) % n).astype(F32)
    th2 = (2.0 * math.pi / n2) * ((j[:, None] * j[None, :]) % n2).astype(F32)
    cr, ci = jnp.cos(th2), -jnp.sin(th2)
    inner = jnp.concatenate([jnp.concatenate([cr, ci], axis=1),
                             jnp.concatenate([-ci, cr], axis=1)], axis=0)
    inv_inner = jnp.concatenate([jnp.concatenate([cr, -ci], axis=1),
                                 jnp.concatenate([ci, cr], axis=1)], axis=0) / n2
    return dict(n1=n1, n2=n2, outer=_split(outer), outer_half=_split(outer[:, :r]),
                inv_outer_re=_split(c[:r] / n1), inv_outer_im=_split(-s[:r] / n1),
                tw_re=jnp.cos(ph), tw_im=-jnp.sin(ph),
                inner=_split(inner), inv_inner=_split(inv_inner))


def _mat_dot(m_hi, m_lo, x, passes):
    x_hi = x.astype(BF16)
    out = _dot(m_hi, x_hi)
    if passes >= 2:
        out = out + _dot(m_hi, (x - x_hi.astype(F32)).astype(BF16))
    if passes >= 3:
        out = out + _dot(m_lo, x_hi)
    return out


def _dot_mat(x, m_hi, m_lo, passes):
    x_hi = x.astype(BF16)
    out = _dot(x_hi, m_hi)
    if passes >= 2:
        out = out + _dot((x - x_hi.astype(F32)).astype(BF16), m_hi)
    if passes >= 3:
        out = out + _dot(x_hi, m_lo)
    return out


def _forward_to_scratch(load, a_scr, f1h, f1l, tw_re, tw_im, n1, cb, passes):
    half = DFT_INNER
    for c in range(0, cb, 2):
        p = _mat_dot(f1h, f1l, jnp.concatenate([load(c), load(c + 1)], axis=1), passes)
        for q in range(2):
            ar, ai = p[:n1, q * half:(q + 1) * half], p[n1:, q * half:(q + 1) * half]
            a_scr[c + q, :, :half] = ar * tw_re - ai * tw_im
            a_scr[c + q, :, half:] = ar * tw_im + ai * tw_re


def _filter_spectrum_kernel(lo_ref, hi_ref, inv_ref, f1h_ref, f1l_ref, twr_ref, twi_ref, mh_ref, ml_ref,
                            kr_ref, ki_ref, a_scr, *, n1, cb):
    half = DFT_INNER
    load = lambda c: jnp.concatenate([lo_ref[c], hi_ref[c]], axis=0)
    _forward_to_scratch(load, a_scr, f1h_ref[...], f1l_ref[...], twr_ref[...], twi_ref[...], n1, cb, FFT_PASSES)
    x = _dot_mat(a_scr[...].reshape(cb * n1, 2 * half), mh_ref[...], ml_ref[...], FFT_PASSES)
    x = x.reshape(cb, n1, 2 * half)
    inv = inv_ref[...]
    kr_ref[...] = (x[:, :, :half] * inv).astype(kr_ref.dtype)
    ki_ref[...] = (x[:, :, half:] * inv).astype(ki_ref.dtype)


def _const_spec(arr):
    return pl.BlockSpec(arr.shape, lambda j: (0,) * arr.ndim)


def _filter_spectrum(lo_t, hi_t, inv_sum, tables):
    ch, rows, n2 = lo_t.shape
    n1 = 2 * rows
    cb = FFT_CB
    f1h, f1l = tables["outer"]
    mh, ml = tables["inner"]
    consts = [f1h, f1l, tables["tw_re"], tables["tw_im"], mh, ml]
    blk = pl.BlockSpec((cb, n1, n2), lambda j: (j, 0, 0))
    half_blk = pl.BlockSpec((cb, rows, n2), lambda j: (j, 0, 0))
    out = jax.ShapeDtypeStruct((ch, n1, n2), BF16)
    return pl.pallas_call(
        functools.partial(_filter_spectrum_kernel, n1=n1, cb=cb),
        grid=(ch // cb,),
        in_specs=[half_blk, half_blk, pl.BlockSpec((cb, 1, n2), lambda j: (j, 0, 0))]
        + [_const_spec(a) for a in consts],
        out_specs=[blk, blk],
        out_shape=[out, out],
        scratch_shapes=[pltpu.VMEM((cb, n1, 2 * n2), F32)],
        compiler_params=_cparams("parallel"),
        name="filter_spectrum",
    )(lo_t, hi_t, inv_sum, *consts)


def _fftconv_kernel(u_ref, gate_ref, kr_ref, ki_ref, skip_ref, f1h_ref, f1l_ref, twr_ref, twi_ref,
                    mh_ref, ml_ref, nh_ref, nl_ref, grh_ref, grl_ref, gih_ref, gil_ref,
                    o_ref, a_scr, *, n1, cb, passes):
    half = DFT_INNER
    tw_re, tw_im = twr_ref[...], twi_ref[...]
    _forward_to_scratch(lambda c: u_ref[c], a_scr, f1h_ref[...], f1l_ref[...], tw_re, tw_im, n1, cb, passes)
    x = _dot_mat(a_scr[...].reshape(cb * n1, 2 * half), mh_ref[...], ml_ref[...], passes)
    xr, xi = x[:, :half], x[:, half:]
    kr = kr_ref[...].astype(F32).reshape(cb * n1, half)
    ki = ki_ref[...].astype(F32).reshape(cb * n1, half)
    y = jnp.concatenate([xr * kr - xi * ki, xr * ki + xi * kr], axis=1)
    b = _dot_mat(y, nh_ref[...], nl_ref[...], passes)
    a_scr[...] = b.reshape(cb, n1, 2 * half)
    grh, grl, gih, gil = grh_ref[...], grl_ref[...], gih_ref[...], gil_ref[...]
    for c in range(0, cb, 2):
        re, im = [], []
        for q in range(2):
            br, bi = a_scr[c + q, :, :half], a_scr[c + q, :, half:]
            re.append(br * tw_re + bi * tw_im)
            im.append(bi * tw_re - br * tw_im)
        conv = (_mat_dot(grh, grl, jnp.concatenate(re, axis=1), passes)
                + _mat_dot(gih, gil, jnp.concatenate(im, axis=1), passes))
        for q in range(2):
            cq = conv[:, q * half:(q + 1) * half]
            o_ref[c + q] = (gate_ref[c + q] * (cq + u_ref[c + q] * skip_ref[c + q])).astype(o_ref.dtype)


def _fftconv(u_t, u_part, gate_t, gate_part, kr, ki, order, skip_t, tables, width, out_dtype):
    n1, n2 = tables["n1"], tables["n2"]
    rows = n1 // 2
    cb = FFT_CB
    nb = width // cb
    consts = [*tables["outer_half"], tables["tw_re"], tables["tw_im"], *tables["inner"],
              *tables["inv_inner"], *tables["inv_outer_re"], *tables["inv_outer_im"]]
    data = lambda part: pl.BlockSpec((cb, rows, n2), lambda j: (part * nb + j, 0, 0))
    spec = pl.BlockSpec((cb, n1, n2), lambda j: (order * nb + j, 0, 0))
    return pl.pallas_call(
        functools.partial(_fftconv_kernel, n1=n1, cb=cb, passes=FFT_PASSES),
        grid=(nb,),
        in_specs=[data(u_part), data(gate_part), spec, spec,
                  pl.BlockSpec((cb, 1, n2), lambda j: (j, 0, 0))] + [_const_spec(a) for a in consts],
        out_specs=pl.BlockSpec((cb, rows, n2), lambda j: (j, 0, 0)),
        out_shape=jax.ShapeDtypeStruct((width, rows, n2), out_dtype),
        scratch_shapes=[pltpu.VMEM((cb, n1, 2 * n2), F32)],
        compiler_params=_cparams("parallel"),
        name="hyena_fftconv",
    )(u_t, gate_t, kr, ki, skip_t, *consts)


def _lane_rows(v):
    return jnp.broadcast_to(v.astype(F32).reshape(-1, 1, 1), (v.shape[0], 1, DFT_INNER))


def _hyena_mixer(z, p, tables, seq, width):
    n1, n2 = tables["n1"], tables["n2"]
    rows = n1 // 2
    zc = _hyena_conv(z, p["hy_conv_w"], p["hy_conv_b"], seq, width)
    zc_t = jnp.transpose(zc, (0, 2, 1)).reshape(3 * width, rows, n2)
    k_lo, k_hi, colsum = _hyena_filters(seq, width, p["hy_filt_w1"], p["hy_filt_b1"], p["hy_filt_w2"],
                                        p["hy_filt_b2"], p["hy_filt_w3"], p["hy_filt_freq"])
    to_channels = lambda a: a.T.reshape(HY_ORDER * width, rows, n2)
    kr, ki = _filter_spectrum(to_channels(k_lo), to_channels(k_hi), _lane_rows(1.0 / colsum.reshape(-1)), tables)
    skip = p["hy_skip"]
    y1 = _fftconv(zc_t, 0, zc_t, 1, kr, ki, 0, _lane_rows(skip[0]), tables, width, F32)
    y = _fftconv(y1, 0, zc_t, 2, kr, ki, 1, _lane_rows(skip[1]), tables, width, BF16)
    return y.reshape(width, seq).T


def _attn_kernel(q_ref, kp_ref, kc_ref, kn_ref, vp_ref, vc_ref, vn_ref, sink_ref, o_ref, *, seq, group):
    i = pl.program_id(0)
    h = pl.program_id(1)
    blk = WINDOW
    k_all = jnp.concatenate([kp_ref[...], kc_ref[...], kn_ref[...]], axis=0)
    v_all = jnp.concatenate([vp_ref[...], vc_ref[...], vn_ref[...]], axis=0)
    row = lax.broadcasted_iota(jnp.int32, (blk, 3 * blk), 0)
    col = lax.broadcasted_iota(jnp.int32, (blk, 3 * blk), 1)
    dist = jnp.abs(row + blk - col)
    distf = dist.astype(F32)
    for qb in range(ATTN_BLOCKS):
        rows = slice(qb * blk, (qb + 1) * blk)
        q = q_ref[rows, :]
        q4 = jnp.concatenate([q[:, g * HEAD_DIM:(g + 1) * HEAD_DIM] for g in range(group)], axis=0)
        k = k_all[qb * blk:(qb + 3) * blk]
        v = v_all[qb * blk:(qb + 3) * blk]
        s = lax.dot_general(q4, k, (((1,), (1,)), ((), ())), preferred_element_type=F32)
        s = s * (HEAD_DIM ** -0.5)
        k_pos = (i * ATTN_BLOCKS + qb - 1) * blk + col
        valid = (dist <= WINDOW) & (k_pos >= 0) & (k_pos < seq)
        outs = []
        for g in range(group):
            sg = s[g * blk:(g + 1) * blk] - sink_ref[KV_HEADS + h, g] * distf
            sg = jnp.where(valid, sg, NEG)
            sink = sink_ref[h, g]
            m = jnp.maximum(jnp.max(sg, axis=-1, keepdims=True), sink)
            e = jnp.exp(sg - m)
            denom = jnp.sum(e, axis=-1, keepdims=True) + jnp.exp(sink - m)
            pg = (e / denom).astype(v.dtype)
            outs.append(_dot(pg, v))
        o_ref[rows, :] = jnp.concatenate(outs, axis=1).astype(o_ref.dtype)


def _attention(z, sink, seq, width, q_off, k_off, v_off):
    group = width // HEAD_DIM // KV_HEADS
    gw = group * HEAD_DIM
    ab = ATTN_BLOCKS
    rows = ab * WINDOW
    nb = seq // WINDOW
    qb, kb, vb = q_off // gw, k_off // HEAD_DIM, v_off // HEAD_DIM
    assert q_off % gw == 0 and k_off % HEAD_DIM == 0 and v_off % HEAD_DIM == 0 and seq % rows == 0
    halo = lambda base, f: pl.BlockSpec((WINDOW, HEAD_DIM), lambda i, h: (f(i), base + h))
    prev = lambda i: jnp.maximum(i * ab - 1, 0)
    nxt = lambda i: jnp.minimum((i + 1) * ab, nb - 1)
    main = lambda base: pl.BlockSpec((rows, HEAD_DIM), lambda i, h: (i, base + h))
    n_q = KV_HEADS * group
    slopes = 2.0 ** (-8.0 * (jnp.arange(n_q, dtype=F32) + 1.0) / n_q)
    scalars = jnp.concatenate([sink.astype(F32), slopes]).reshape(2 * KV_HEADS, group)
    return pl.pallas_call(
        functools.partial(_attn_kernel, seq=seq, group=group),
        grid=(seq // rows, KV_HEADS),
        in_specs=[pl.BlockSpec((rows, gw), lambda i, h: (i, qb + h)),
                  halo(kb, prev), main(kb), halo(kb, nxt),
                  halo(vb, prev), main(vb), halo(vb, nxt),
                  pl.BlockSpec(memory_space=pltpu.SMEM)],
        out_specs=pl.BlockSpec((rows, gw), lambda i, h: (i, h)),
        out_shape=jax.ShapeDtypeStruct((seq, width), BF16),
        compiler_params=_cparams("parallel", "parallel"),
        name="window_attention",
    )(z, z, z, z, z, z, z, scalars)


def _retention_kernel(*refs, heads, reverse, final, cps):
    if final:
        cd_ref, dec_ref, kw_ref, qw_ref, q_ref, k_ref, v_ref, prev_ref, g_ref, o_ref, s_ref = refs
    else:
        cd_ref, dec_ref, kw_ref, qw_ref, q_ref, k_ref, v_ref, o_ref, s_ref = refs
    hg = pl.program_id(0)
    n = pl.program_id(1)
    c = RET_CHUNK

    @pl.when(n == 0)
    def _():
        s_ref[...] = jnp.zeros_like(s_ref)

    chunk_order = range(cps - 1, -1, -1) if reverse else range(cps)
    for hd in range(heads):
        sl = slice(hd * HEAD_DIM, (hd + 1) * HEAD_DIM)
        decay, k_w, q_w = dec_ref[hd], kw_ref[hd], qw_ref[hd]
        chunk_decay = cd_ref[hg * heads + hd]
        state = s_ref[hd]
        for cc in chunk_order:
            rows = slice(cc * c, (cc + 1) * c)
            q, k, v = q_ref[rows, sl], k_ref[rows, sl], v_ref[rows, sl]
            qk = lax.dot_general(q, k, (((1,), (1,)), ((), ())), preferred_element_type=F32)
            inner = _dot((qk * decay).astype(v.dtype), v)
            cross = _dot((q.astype(F32) * q_w).astype(BF16), state.astype(BF16))
            kv = lax.dot_general((k.astype(F32) * k_w).astype(BF16), v, (((0,), (0,)), ((), ())),
                                 preferred_element_type=F32)
            state = state * chunk_decay + kv
            o = inner + cross
            if final:
                o = o + prev_ref[rows, sl]
                o = o * lax.rsqrt(jnp.mean(o * o, axis=-1, keepdims=True) + EPS)
                g = g_ref[rows, sl].astype(F32)
                o = g * _sigmoid(g) * o
            o_ref[rows, sl] = o.astype(o_ref.dtype)
        s_ref[hd] = state


def _retention_pass(z, seq, width, offs, reverse, prev=None):
    c = RET_CHUNK
    cps = RET_CHUNKS_PER_STEP
    rows = cps * c
    nb = seq // rows
    gw = math.gcd(width, *offs)
    assert gw % HEAD_DIM == 0 and seq % rows == 0
    heads = gw // HEAD_DIM
    q_off, k_off, v_off, g_off = offs
    hidx = jnp.arange(width // HEAD_DIM, dtype=F32)
    idx = jnp.arange(c, dtype=F32)
    rel = idx[:, None] - idx[None, :]
    scale = HEAD_DIM ** -0.5
    if reverse:
        lg = jnp.log(1.0 - 2.0 ** (-5.5 - hidx))[:, None]
        order = lambda n: nb - 1 - n
        rel, k_pow, q_pow = -rel, idx, c - idx
    else:
        lg = jnp.log(1.0 - 2.0 ** (-5.0 - hidx))[:, None]
        order = lambda n: n
        k_pow, q_pow = c - 1.0 - idx, idx + 1.0
    decay = jnp.where(rel >= 0, jnp.exp(jnp.maximum(rel, 0.0)[None] * lg[:, :, None]), 0.0) * scale
    lanes = lambda a: jnp.broadcast_to(a[:, :, None], a.shape + (HEAD_DIM,))
    k_w = lanes(jnp.exp(k_pow[None] * lg) * scale)
    q_w = lanes(jnp.exp(q_pow[None] * lg))
    chunk_decay = jnp.exp(c * lg[:, 0])
    final = prev is not None
    zspec = lambda off: pl.BlockSpec((rows, gw), lambda g, n: (order(n), off // gw + g))
    own = pl.BlockSpec((rows, gw), lambda g, n: (order(n), g))
    table = pl.BlockSpec((heads, c, HEAD_DIM), lambda g, n: (g, 0, 0))
    in_specs = [pl.BlockSpec(memory_space=pltpu.SMEM), table, table, table,
                zspec(q_off), zspec(k_off), zspec(v_off)]
    args = [chunk_decay, decay, k_w, q_w, z, z, z]
    if final:
        in_specs += [own, zspec(g_off)]
        args += [prev, z]
    return pl.pallas_call(
        functools.partial(_retention_kernel, heads=heads, reverse=reverse, final=final, cps=cps),
        grid=(width // gw, nb),
        in_specs=in_specs,
        out_specs=own,
        out_shape=jax.ShapeDtypeStruct((seq, width), BF16 if final else F32),
        scratch_shapes=[pltpu.VMEM((heads, HEAD_DIM, HEAD_DIM), F32)],
        compiler_params=_cparams("parallel", "arbitrary"),
        name="retention_fwd" if final else "retention_bwd",
    )(*args)


def _retention(z, seq, width, offs):
    o_bwd = _retention_pass(z, seq, width, offs, reverse=True)
    return _retention_pass(z, seq, width, offs, reverse=False, prev=o_bwd)


def _merge_kernel(ya_ref, yb_ref, yc_ref, wa_ref, wb_ref, wc_ref, ga_ref, gb_ref, gc_ref, o_ref):
    acc = None
    for y_ref, w_ref, g_ref in ((ya_ref, wa_ref, ga_ref), (yb_ref, wb_ref, gb_ref), (yc_ref, wc_ref, gc_ref)):
        t = _sigmoid(g_ref[...].astype(F32)) * _dot(y_ref[...], w_ref[...])
        acc = t if acc is None else acc + t
    o_ref[...] = acc.astype(o_ref.dtype)


def _merge(ya, yb, yc, w_branch, layer, z, gate_off, d_model):
    seq, width = ya.shape
    tm, tn = _tile(seq, 1024), _tile(d_model, 512)
    assert gate_off % tn == 0
    ys = pl.BlockSpec((tm, width), lambda i, j: (i, 0))
    ws = lambda b: pl.BlockSpec((None, None, width, tn), lambda i, j: (layer, b, 0, j))
    gs = lambda b: pl.BlockSpec((tm, tn), lambda i, j: (i, (gate_off + b * d_model) // tn + j))
    return pl.pallas_call(
        _merge_kernel,
        grid=(seq // tm, d_model // tn),
        in_specs=[ys, ys, ys, ws(0), ws(1), ws(2), gs(0), gs(1), gs(2)],
        out_specs=pl.BlockSpec((tm, tn), lambda i, j: (i, j)),
        out_shape=jax.ShapeDtypeStruct((seq, d_model), BF16),
        compiler_params=_cparams("parallel", "parallel"),
        name="branch_merge",
    )(ya, yb, yc, w_branch, w_branch, w_branch, z, z, z)


def _ffn_in_kernel(h_ref, hp_ref, hn_ref, s_ref, sp_ref, sn_ref, wg_ref, wu_ref, cw_ref, cb_ref, *rest,
                   n_tiles, d):
    o_ref = rest[-1]
    i = pl.program_id(0)
    first = (i > 0).astype(F32)
    last = (i < n_tiles - 1).astype(F32)
    r = _row_scale(s_ref[...], d)
    wg = wg_ref[...]
    g = _dot(h_ref[...], wg) * r
    prev_row = _dot(hp_ref[...], wg)[HALO - 1:HALO] * (_row_scale(sp_ref[...], d)[SUBLANES - 1:SUBLANES] * first)
    next_row = _dot(hn_ref[...], wg)[0:1] * (_row_scale(sn_ref[...], d)[0:1] * last)
    down, up = _shift_rows(g, prev_row, next_row)
    cw = cw_ref[...]
    c = down * cw[0:1] + g * cw[1:2] + up * cw[2:3] + cb_ref[...]
    gelu = 0.5 * c * (1.0 + lax.erf(c * (2.0 ** -0.5)))
    o_ref[...] = (gelu * (_dot(h_ref[...], wu_ref[...]) * r)).astype(o_ref.dtype)


def _ffn_in_span(h, ssq, w_gate, w_up, conv_w, conv_b, layer, tn, col0, n_col_tiles, prev_out):
    seq, d = h.shape
    ff = w_gate.shape[2]
    tm = _tile(seq, 1024)
    n_tiles, n_halo, per = seq // tm, seq // HALO, tm // HALO
    n_sub, per_sub = seq // SUBLANES, tm // SUBLANES
    assert col0 % tn == 0
    j0 = col0 // tn
    in_specs = [
        pl.BlockSpec((tm, d), lambda i, j: (i, 0)),
        pl.BlockSpec((HALO, d), lambda i, j: (jnp.maximum(i * per - 1, 0), 0)),
        pl.BlockSpec((HALO, d), lambda i, j: (jnp.minimum((i + 1) * per, n_halo - 1), 0)),
        pl.BlockSpec((tm, LANES), lambda i, j: (i, 0)),
        pl.BlockSpec((SUBLANES, LANES), lambda i, j: (jnp.maximum(i * per_sub - 1, 0), 0)),
        pl.BlockSpec((SUBLANES, LANES), lambda i, j: (jnp.minimum((i + 1) * per_sub, n_sub - 1), 0)),
        pl.BlockSpec((None, d, tn), lambda i, j: (layer, 0, j0 + j)),
        pl.BlockSpec((None, d, tn), lambda i, j: (layer, 0, j0 + j)),
        pl.BlockSpec((None, 3, tn), lambda i, j: (layer, 0, j0 + j)),
        pl.BlockSpec((None, 1, tn), lambda i, j: (layer, 0, j0 + j)),
    ]
    args = [h, h, h, ssq, ssq, ssq, w_gate, w_up, conv_w, conv_b]
    aliases = {}
    if prev_out is not None:
        in_specs.append(pl.BlockSpec(memory_space=pl.ANY))
        args.append(prev_out)
        aliases = {len(args) - 1: 0}
    return pl.pallas_call(
        functools.partial(_ffn_in_kernel, n_tiles=n_tiles, d=d),
        grid=(n_tiles, n_col_tiles),
        in_specs=in_specs,
        out_specs=pl.BlockSpec((tm, tn), lambda i, j: (i, j0 + j)),
        out_shape=jax.ShapeDtypeStruct((seq, ff), BF16),
        input_output_aliases=aliases,
        compiler_params=_cparams("parallel", "parallel"),
        name="ffn_in",
    )(*args)


def _ffn_in(h, ssq, w_gate, w_up, conv_w, conv_b, layer):
    ff = w_gate.shape[2]
    tn = min(FFN_TILE, ff)
    main = (ff // tn) * tn
    out = _ffn_in_span(h, ssq, w_gate, w_up, conv_w, conv_b, layer, tn, 0, main // tn, None)
    if main < ff:
        tail = math.gcd(ff - main, main)
        assert tail % LANES == 0
        out = _ffn_in_span(h, ssq, w_gate, w_up, conv_w, conv_b, layer, tail, main, (ff - main) // tail, out)
    return out


MATMUL_WEIGHTS = ("w_in", "w_branch", "w_out", "w_ffn_in", "w_ffn_gate", "w_ffn_up", "w_ffn_out",
                  "ffn_conv_w", "ffn_conv_b")


def _prepare_weights(w):
    ff = w["w_ffn_out"].shape[1]
    w_ffn_in = w["w_ffn_in"]
    return dict(
        w_in=w["w_in"].astype(BF16),
        w_branch=w["w_branch"].astype(BF16),
        w_out=w["w_out"].astype(BF16),
        w_ffn_gate=w_ffn_in[..., :ff].astype(BF16),
        w_ffn_up=w_ffn_in[..., ff:].astype(BF16),
        w_ffn_out=w["w_ffn_out"].astype(BF16),
        ffn_conv_w=w["ffn_conv_w"].astype(F32),
        ffn_conv_b=w["ffn_conv_b"].astype(F32)[:, None, :],
    )


def _layer(x, l, p, tables, normed, next_gain):
    seq, d = x.shape
    width = d // 4
    kvw = KV_HEADS * HEAD_DIM
    aq, r_offs = 3 * width, [4 * width, 5 * width, 6 * width, 7 * width]
    ak, av, gates = 8 * width, 8 * width + kvw, 8 * width + 2 * kvw
    tn = 2 * kvw
    assert (4 * width) % tn == 0
    kv_old, kv_new, r_end = 4 * width // tn, 8 * width // tn, (8 * width + 2 * kvw) // tn

    def z_tile(j):
        return jnp.where(j < kv_old, j, jnp.where(j == kv_old, kv_new, jnp.where(j < r_end, j - 1, j)))

    small = {name: arr[l] for name, arr in p.items() if name not in MATMUL_WEIGHTS}
    if normed is None:
        h, ssq = _rmsnorm(x, small["norm_mix"], BF16), None
    else:
        h, ssq = normed
    z = _matmul(h, p["w_in"], l, tm=2048, tn=tn, out_dtype=BF16, ssq=ssq, out_col=z_tile, name="in_proj")
    y_a = _hyena_mixer(z, small, tables, seq, width)
    y_b = _attention(z, small["attn_sink"], seq, width, aq, ak, av)
    y_c = _retention(z, seq, width, r_offs)
    merged = _merge(y_a, y_b, y_c, p["w_branch"], l, z, gates, d)
    x, h, ssq = _matmul(merged, p["w_out"], l, tm=1024, tn=512, out_dtype=F32, residual=x,
                        next_gain=small["norm_ffn"], name="out_proj")
    act = _ffn_in(h, ssq, p["w_ffn_gate"], p["w_ffn_up"], p["ffn_conv_w"], p["ffn_conv_b"], l)
    ffn_out = functools.partial(_matmul, act, p["w_ffn_out"], l, tm=512, tn=512,
                                out_dtype=F32, residual=x, name="ffn_out")
    if next_gain is None:
        return ffn_out(), None
    x, h, ssq = ffn_out(next_gain=next_gain)
    return x, (h, ssq)


def _trunk(x, p, norm_final):
    b, seq, d = x.shape
    assert b == 1
    tables = _fft_tables(seq)
    x = x.reshape(seq, d)
    depth = p["w_in"].shape[0]
    normed = None
    for l in range(depth):
        next_gain = p["norm_mix"][l + 1] if l + 1 < depth else None
        x, normed = _layer(x, l, p, tables, normed, next_gain)
    return _rmsnorm(x, norm_final, F32).reshape(b, seq, d)


def kernel(x_prompt, x_sample, norm_mix, w_in, hy_conv_w, hy_conv_b, hy_filt_w1, hy_filt_b1, hy_filt_w2, hy_filt_b2, hy_filt_w3, hy_filt_freq, hy_skip, attn_sink, w_branch, w_out, norm_ffn, w_ffn_in, ffn_conv_w, ffn_conv_b, w_ffn_out, norm_final):
    w = dict(norm_mix=norm_mix, w_in=w_in, hy_conv_w=hy_conv_w, hy_conv_b=hy_conv_b,
             hy_filt_w1=hy_filt_w1, hy_filt_b1=hy_filt_b1, hy_filt_w2=hy_filt_w2,
             hy_filt_b2=hy_filt_b2, hy_filt_w3=hy_filt_w3, hy_filt_freq=hy_filt_freq,
             hy_skip=hy_skip, attn_sink=attn_sink, w_branch=w_branch, w_out=w_out,
             norm_ffn=norm_ffn, w_ffn_in=w_ffn_in, ffn_conv_w=ffn_conv_w,
             ffn_conv_b=ffn_conv_b, w_ffn_out=w_ffn_out)
    p = dict(w)
    p.update(_prepare_weights(w))
    return (_trunk(x_prompt, p, norm_final), _trunk(x_sample, p, norm_final))
```
